```python
import math
import jax, jax.numpy as jnp
from jax import lax
import numpy as np

D_MODEL = 1024
BATCH = 32
SEQ = 256
DEPTH = 2
DEC_BATCH = 4
DEC_SEQ = 4096
PAST_LEN = 256

GRID_W = 64
NORM_EPS = 1e-6
SSD_HEADS = 8
SSD_HEAD_DIM = 64
SSD_STATE = 64
SSD_GROUPS = 2
SSD_CONV = 3
SSD_CHUNK = 64
SSD_WIDTH = SSD_HEADS * SSD_HEAD_DIM
SSD_BC = SSD_GROUPS * SSD_STATE
SSD_XBC = SSD_WIDTH + 2 * SSD_BC
SSD_IN = SSD_WIDTH + SSD_XBC + 2 * SSD_HEADS
NA_HEADS = 8
NA_HEAD_DIM = 64
NA_WIDTH = NA_HEADS * NA_HEAD_DIM
NA_IN = 3 * NA_WIDTH
NA_WIN_ROWS = 8
NA_WIN_COLS = 16
NA_SCALE = NA_HEAD_DIM ** -0.5
GLA_HEADS = 4
GLA_DK = 64
GLA_DV = 128
GLA_QK = GLA_HEADS * GLA_DK
GLA_V = GLA_HEADS * GLA_DV
GLA_RANK = 16
GLA_GATE_NORM = 16.0
HGRN_HEADS = 4
HGRN_DK = 64
HGRN_DV = 128
HGRN_QK = HGRN_HEADS * HGRN_DK
HGRN_V = HGRN_HEADS * HGRN_DV
LIN_CHUNK = 16
IN_L0 = SSD_IN + NA_IN
IN_L1 = 2 * GLA_QK + 2 * GLA_V + 2 * GLA_RANK + 3 * HGRN_QK + 2 * HGRN_V
MIX_OUT = D_MODEL
FFN_DIM = 2816
FFN_CONV = 3

kernel_name = 'hybrid_diffusion_trunk_step'


def _rms(x, w):
    xf = x.astype(jnp.float32)
    y = xf * lax.rsqrt(jnp.mean(xf * xf, axis=-1, keepdims=True) + NORM_EPS)
    return y.astype(x.dtype) * w


def _split(u, sizes):
    offs = [int(o) for o in np.cumsum(sizes)[:-1]]
    return jnp.split(u, offs, axis=-1)


def _heads(x, h):
    b, t, _ = x.shape
    return x.reshape(b, t, h, -1).transpose(0, 2, 1, 3)


def _unheads(x):
    b, h, t, d = x.shape
    return x.transpose(0, 2, 1, 3).reshape(b, t, h * d)


def _dwconv(x, w, b):
    k = w.shape[0]
    p = k // 2
    t = x.shape[1]
    xp = jnp.pad(x, ((0, 0), (p, p), (0, 0)))
    out = b
    for i in range(k):
        out = out + xp[:, i:i + t] * w[i]
    return out


def _chunk_scan(q, k, v, g, s0, chunk):
    out_dtype = q.dtype
    b, h, t, _ = q.shape
    dv = v.shape[-1]
    n = t // chunk

    def blk(a):
        return a.astype(jnp.float32).reshape(b, h, n, chunk, a.shape[-1])

    q, k, v, g = blk(q), blk(k), blk(v), blk(g)
    G = jnp.cumsum(g, axis=3)
    mask = jnp.tril(jnp.ones((chunk, chunk), bool))
    diff = G[:, :, :, :, None, :] - G[:, :, :, None, :, :]
    decay = jnp.exp(jnp.where(mask[:, :, None], diff, -jnp.inf))
    if g.shape[-1] == 1:
        scores = jnp.einsum('bhntd,bhnsd->bhnts', q, k) * decay[..., 0]
    else:
        scores = jnp.einsum('bhntd,bhnsd,bhntsd->bhnts', q, k, decay)
    o_intra = jnp.einsum('bhnts,bhnsv->bhntv', scores, v)
    G_last = G[:, :, :, -1:, :]
    q_in = q * jnp.exp(G)
    k_end = k * jnp.exp(G_last - G)
    d_last = jnp.exp(G_last[:, :, :, 0, :])

    def step(S, xs):
        qc, kc, vc, dc = xs
        o = jnp.einsum('bhtd,bhdv->bhtv', qc, S)
        S = dc[..., None] * S + jnp.einsum('bhtd,bhtv->bhdv', kc, vc)
        return S, o

    xs = (jnp.moveaxis(q_in, 2, 0), jnp.moveaxis(k_end, 2, 0), jnp.moveaxis(v, 2, 0), jnp.moveaxis(d_last, 2, 0))
    S, o_inter = lax.scan(step, s0.astype(jnp.float32), xs)
    o = o_intra + jnp.moveaxis(o_inter, 0, 2)
    return o.reshape(b, h, t, dv).astype(out_dtype), S


def _bidir(q, kf, kb, vf, vb, gf, gb, s0f, s0b, chunk):
    of, sf = _chunk_scan(q, kf, vf, gf, s0f, chunk)
    fl = lambda a: jnp.flip(a, axis=2)
    ob, sb = _chunk_scan(fl(q), fl(kb), fl(vb), fl(gb), s0b, chunk)
    return of + fl(ob), sf, sb


def _ssd(ua, conv_w, conv_b, dt_bias, a_log, d_skip, norm_w, s0f, s0b):
    bsz, t, _ = ua.shape
    z, xbc, dt_raw = _split(ua, [SSD_WIDTH, SSD_XBC, 2 * SSD_HEADS])
    xbc = jax.nn.silu(_dwconv(xbc, conv_w, conv_b))
    xs, bm, cm = _split(xbc, [SSD_WIDTH, SSD_BC, SSD_BC])
    x_h = _heads(xs, SSD_HEADS)
    rep = SSD_HEADS // SSD_GROUPS
    k = jnp.repeat(_heads(bm, SSD_GROUPS), rep, axis=1)
    q = jnp.repeat(_heads(cm, SSD_GROUPS), rep, axis=1)
    dt = jax.nn.softplus(dt_raw.reshape(bsz, t, 2, SSD_HEADS).astype(jnp.float32) + dt_bias)
    dt = dt.transpose(2, 0, 3, 1)
    a = -jnp.exp(a_log.astype(jnp.float32))
    g = (dt * a[:, None, :, None])[..., None]
    vf = x_h * dt[0][..., None]
    vb = x_h * dt[1][..., None]
    y, sf, sb = _bidir(q, k, k, vf, vb, g[0], g[1], s0f, s0b, SSD_CHUNK)
    y = y + d_skip[None, :, None, None] * x_h
    y = _unheads(y)
    return _rms(y * jax.nn.silu(z), norm_w), sf, sb


def _na_context(ub, q_norm, k_norm):
    q, k, v = [_heads(a, NA_HEADS) for a in _split(ub, [NA_WIDTH] * 3)]
    q = _rms(q, q_norm)
    k = _rms(k, k_norm)
    s = jnp.einsum('bhqd,bhkd->bhqk', q, k).astype(jnp.float32) * NA_SCALE
    p = jax.nn.softmax(s, axis=-1).astype(v.dtype)
    o = jnp.einsum('bhqk,bhkd->bhqd', p, v)
    return _unheads(o), k, v


def _na_latent(ub, k_ctx, v_ctx, q_norm, k_norm, rpb):
    bsz, t, _ = ub.shape
    rows = t // GRID_W
    wr = min(NA_WIN_ROWS, rows)
    q, k, v = [_heads(a, NA_HEADS) for a in _split(ub, [NA_WIDTH] * 3)]
    q = _rms(q, q_norm)
    k = _rms(k, k_norm)
    grid = lambda a: a.reshape(bsz, NA_HEADS, rows, GRID_W, NA_HEAD_DIM)
    q, k, v = grid(q), grid(k), grid(v)
    r = jnp.arange(rows)
    row_start = jnp.clip(r - wr // 2, 0, rows - wr)
    row_idx = row_start[:, None] + jnp.arange(wr)[None, :]
    n_loc = wr * GRID_W
    k_loc = k[:, :, row_idx].reshape(bsz, NA_HEADS, rows, n_loc, NA_HEAD_DIM)
    v_loc = v[:, :, row_idx].reshape(bsz, NA_HEADS, rows, n_loc, NA_HEAD_DIM)
    col = jnp.arange(GRID_W)
    col_start = jnp.clip(col - NA_WIN_COLS // 2, 0, GRID_W - NA_WIN_COLS)
    col_ok = (col[None, :] >= col_start[:, None]) & (col[None, :] < col_start[:, None] + NA_WIN_COLS)
    valid = jnp.tile(col_ok, (1, wr))
    d_row = row_idx - r[:, None] + (NA_WIN_ROWS - 1)
    d_col = jnp.clip(col[None, :] - col[:, None], -(NA_WIN_COLS - 1), NA_WIN_COLS - 1) + (NA_WIN_COLS - 1)
    bias = rpb[:, d_row][:, :, :, d_col]
    bias = bias.transpose(0, 1, 3, 2, 4).reshape(NA_HEADS, rows, GRID_W, n_loc)
    s_loc = jnp.einsum('bhrqd,bhrkd->bhrqk', q, k_loc).astype(jnp.float32) * NA_SCALE + bias
    s_loc = jnp.where(valid, s_loc, -jnp.inf)
    s_ctx = jnp.einsum('bhrqd,bhkd->bhrqk', q, k_ctx).astype(jnp.float32) * NA_SCALE
    p = jax.nn.softmax(jnp.concatenate([s_loc, s_ctx], axis=-1), axis=-1).astype(v.dtype)
    o = (jnp.einsum('bhrqk,bhrkd->bhrqd', p[..., :n_loc], v_loc)
         + jnp.einsum('bhrqk,bhkd->bhrqd', p[..., n_loc:], v_ctx))
    return _unheads(o.reshape(bsz, NA_HEADS, t, NA_HEAD_DIM))


def _mixer_ab(h, cache, w_in, w_out, conv_w, conv_b, dt_bias, a_log, d_skip, ssd_norm_w, q_norm, k_norm, rpb):
    u = h @ w_in
    ua, ub = _split(u, [SSD_IN, NA_IN])
    if cache is None:
        z0 = jnp.zeros((h.shape[0], SSD_HEADS, SSD_STATE, SSD_HEAD_DIM), jnp.float32)
        y_a, s_f, s_b = _ssd(ua, conv_w, conv_b, dt_bias, a_log, d_skip, ssd_norm_w, z0, z0)
        y_b, k_c, v_c = _na_context(ub, q_norm, k_norm)
        new = (k_c, v_c, s_f, s_b)
    else:
        k_c, v_c, s_f0, s_b0 = cache
        y_a, _, _ = _ssd(ua, conv_w, conv_b, dt_bias, a_log, d_skip, ssd_norm_w, s_f0, s_b0)
        y_b = _na_latent(ub, k_c, v_c, q_norm, k_norm, rpb)
        new = ()
    return jnp.concatenate([y_a, y_b], axis=-1) @ w_out, new


def _mixer_cd(h, cache, w_in, w_out, gla_wa2, gla_ba2, gla_norm_w, hgrn_lb, hgrn_norm_w):
    bsz = h.shape[0]
    u = h @ w_in
    gq, gk, gv, gg, gaf, gab, hq, hff, hfb, hi, hg = _split(
        u, [GLA_QK, GLA_QK, GLA_V, GLA_V, GLA_RANK, GLA_RANK, HGRN_QK, HGRN_QK, HGRN_QK, HGRN_V, HGRN_V])
    if cache is None:
        zg = jnp.zeros((bsz, GLA_HEADS, GLA_DK, GLA_DV), jnp.float32)
        zh = jnp.zeros((bsz, HGRN_HEADS, HGRN_DK, HGRN_DV), jnp.float32)
        s_gf, s_gb, s_hf, s_hb = zg, zg, zh, zh
    else:
        s_gf, s_gb, s_hf, s_hb = cache
    q = _heads(gq, GLA_HEADS) * GLA_DK ** -0.5
    k = _heads(gk, GLA_HEADS)
    v = _heads(gv, GLA_HEADS)
    la_f = _heads(jax.nn.log_sigmoid((gaf @ gla_wa2[0] + gla_ba2[0]).astype(jnp.float32)) / GLA_GATE_NORM, GLA_HEADS)
    la_b = _heads(jax.nn.log_sigmoid((gab @ gla_wa2[1] + gla_ba2[1]).astype(jnp.float32)) / GLA_GATE_NORM, GLA_HEADS)
    o_c, n_gf, n_gb = _bidir(q, k, k, v, v, la_f, la_b, s_gf, s_gb, LIN_CHUNK)
    y_c = _unheads(_rms(o_c, gla_norm_w) * jax.nn.silu(_heads(gg, GLA_HEADS)))
    f_f = hgrn_lb[0] + (1.0 - hgrn_lb[0]) * jax.nn.sigmoid(hff.astype(jnp.float32))
    f_b = hgrn_lb[1] + (1.0 - hgrn_lb[1]) * jax.nn.sigmoid(hfb.astype(jnp.float32))
    qh = _heads(hq, HGRN_HEADS)
    ih = _heads(hi, HGRN_HEADS)
    o_d, n_hf, n_hb = _bidir(qh, _heads(1.0 - f_f, HGRN_HEADS), _heads(1.0 - f_b, HGRN_HEADS), ih, ih,
                             _heads(jnp.log(f_f), HGRN_HEADS), _heads(jnp.log(f_b), HGRN_HEADS),
                             s_hf, s_hb, LIN_CHUNK)
    y_d = _unheads(_rms(o_d, hgrn_norm_w) * jax.nn.silu(_heads(hg, HGRN_HEADS)))
    new = (n_gf, n_gb, n_hf, n_hb) if cache is None else ()
    return jnp.concatenate([y_c, y_d], axis=-1) @ w_out, new


def _conv_ffn(h, w_up, conv_w, conv_b, w_down):
    u = _dwconv(h @ w_up, conv_w, conv_b)
    a, b = jnp.split(u, 2, axis=-1)
    return (jax.nn.silu(a) * b) @ w_down


def setup_inputs(seed: int = 0) -> dict:
    key = jax.random.key(seed)
    keys = iter(jax.random.split(key, 64))
    D = D_MODEL

    def nrm(shape, scale):
        return jax.random.normal(next(keys), shape, jnp.float32) * scale

    def gain(shape):
        return 1.0 + nrm(shape, 0.02)

    dt0 = jnp.exp(jax.random.uniform(next(keys), (2, SSD_HEADS), jnp.float32, math.log(1e-3), math.log(1e-1)))
    dt_bias = dt0 + jnp.log(-jnp.expm1(-dt0))
    a_log = jnp.log(jax.random.uniform(next(keys), (2, SSD_HEADS), jnp.float32, 1.0, 16.0))
    return {
        'x_prompt': nrm((BATCH, SEQ, D), 1.0),
        'x_sample': nrm((DEC_BATCH, DEC_SEQ, D), 1.0),
        'cache_na_k_l0': nrm((DEC_BATCH, NA_HEADS, PAST_LEN, NA_HEAD_DIM), 1.0),
        'cache_na_v_l0': nrm((DEC_BATCH, NA_HEADS, PAST_LEN, NA_HEAD_DIM), 1.0),
        'state_ssd_fwd_l0': nrm((DEC_BATCH, SSD_HEADS, SSD_STATE, SSD_HEAD_DIM), 0.5),
        'state_ssd_bwd_l0': nrm((DEC_BATCH, SSD_HEADS, SSD_STATE, SSD_HEAD_DIM), 0.5),
        'state_gla_fwd_l1': nrm((DEC_BATCH, GLA_HEADS, GLA_DK, GLA_DV), 0.5),
        'state_gla_bwd_l1': nrm((DEC_BATCH, GLA_HEADS, GLA_DK, GLA_DV), 0.5),
        'state_hgrn_fwd_l1': nrm((DEC_BATCH, HGRN_HEADS, HGRN_DK, HGRN_DV), 0.5),
        'state_hgrn_bwd_l1': nrm((DEC_BATCH, HGRN_HEADS, HGRN_DK, HGRN_DV), 0.5),
        'c': nrm((DEC_BATCH, D), 1.0),
        'c_ctx': nrm((D,), 1.0),
        'w_ada': nrm((DEPTH, D, 6 * D), 0.5 * D ** -0.5),
        'b_ada': nrm((DEPTH, 6 * D), 0.02),
        'norm_w': gain((DEPTH, 2, D)),
        'ffn_w_up': nrm((DEPTH, D, 2 * FFN_DIM), D ** -0.5),
        'ffn_conv_w': nrm((DEPTH, FFN_CONV, 2 * FFN_DIM), 0.5),
        'ffn_conv_b': nrm((DEPTH, 2 * FFN_DIM), 0.02),
        'ffn_w_down': nrm((DEPTH, FFN_DIM, D), FFN_DIM ** -0.5),
        'w_in_l0': nrm((D, IN_L0), D ** -0.5),
        'w_out_l0': nrm((MIX_OUT, D), MIX_OUT ** -0.5),
        'ssd_conv_w_l0': nrm((SSD_CONV, SSD_XBC), 0.5),
        'ssd_conv_b_l0': nrm((SSD_XBC,), 0.02),
        'ssd_dt_bias_l0': dt_bias,
        'ssd_a_log_l0': a_log,
        'ssd_d_l0': gain((SSD_HEADS,)),
        'ssd_norm_w_l0': gain((SSD_WIDTH,)),
        'na_q_norm_l0': gain((NA_HEAD_DIM,)),
        'na_k_norm_l0': gain((NA_HEAD_DIM,)),
        'na_rpb_l0': nrm((NA_HEADS, 2 * NA_WIN_ROWS - 1, 2 * NA_WIN_COLS - 1), 0.02),
        'w_in_l1': nrm((D, IN_L1), D ** -0.5),
        'w_out_l1': nrm((MIX_OUT, D), MIX_OUT ** -0.5),
        'gla_wa2_l1': nrm((2, GLA_RANK, GLA_QK), GLA_RANK ** -0.5),
        'gla_ba2_l1': nrm((2, GLA_QK), 0.1),
        'gla_norm_w_l1': gain((GLA_DV,)),
        'hgrn_lb_logits': nrm((2, DEPTH, HGRN_QK), 0.1),
        'hgrn_norm_w_l1': gain((HGRN_DV,)),
    }


def reference(x_prompt, x_sample, cache_na_k_l0, cache_na_v_l0, state_ssd_fwd_l0, state_ssd_bwd_l0,
              state_gla_fwd_l1, state_gla_bwd_l1, state_hgrn_fwd_l1, state_hgrn_bwd_l1, c,
              c_ctx, w_ada, b_ada, norm_w, ffn_w_up, ffn_conv_w, ffn_conv_b, ffn_w_down,
              w_in_l0, w_out_l0, ssd_conv_w_l0, ssd_conv_b_l0, ssd_dt_bias_l0, ssd_a_log_l0, ssd_d_l0,
              ssd_norm_w_l0, na_q_norm_l0, na_k_norm_l0, na_rpb_l0,
              w_in_l1, w_out_l1, gla_wa2_l1, gla_ba2_l1, gla_norm_w_l1, hgrn_lb_logits, hgrn_norm_w_l1):
    p_lb = jax.nn.softmax(hgrn_lb_logits.astype(jnp.float32), axis=1)
    lb_all = jnp.cumsum(p_lb, axis=1) - p_lb[:, :1]
    mixer_weights = (
        (w_in_l0, w_out_l0, ssd_conv_w_l0, ssd_conv_b_l0, ssd_dt_bias_l0, ssd_a_log_l0, ssd_d_l0,
         ssd_norm_w_l0, na_q_norm_l0, na_k_norm_l0, na_rpb_l0),
        (w_in_l1, w_out_l1, gla_wa2_l1, gla_ba2_l1, gla_norm_w_l1, lb_all[:, 1], hgrn_norm_w_l1),
    )
    caches = (
        (cache_na_k_l0, cache_na_v_l0, state_ssd_fwd_l0, state_ssd_bwd_l0),
        (state_gla_fwd_l1, state_gla_bwd_l1, state_hgrn_fwd_l1, state_hgrn_bwd_l1),
    )

    def run_layer(x, cvec, l, cache):
        mods = jax.nn.silu(cvec) @ w_ada[l] + b_ada[l]
        sh1, sc1, g1, sh2, sc2, g2 = jnp.split(mods[:, None, :], 6, axis=-1)
        h = _rms(x, norm_w[l, 0]) * (1.0 + sc1) + sh1
        mixer = _mixer_ab if l % 2 == 0 else _mixer_cd
        y, new = mixer(h, cache, *mixer_weights[l])
        x = x + g1 * y
        h = _rms(x, norm_w[l, 1]) * (1.0 + sc2) + sh2
        x = x + g2 * _conv_ffn(h, ffn_w_up[l], ffn_conv_w[l], ffn_conv_b[l], ffn_w_down[l])
        return x, new

    c_ctx_b = c_ctx[None, :]
    y_p, y_s = x_prompt, x_sample
    new_state = []
    for l in range(DEPTH):
        y_p, st = run_layer(y_p, c_ctx_b, l, None)
        new_state.append(st)
        y_s, _ = run_layer(y_s, c, l, caches[l])
    na_k0, na_v0, ssd_f0, ssd_b0 = new_state[0]
    gla_f1, gla_b1, hgrn_f1, hgrn_b1 = new_state[1]
    return (y_p, y_s, na_k0, na_v0, ssd_f0, ssd_b0, gla_f1, gla_b1, hgrn_f1, hgrn_b1)
```

```python
import functools

import jax
import jax.numpy as jnp
from jax import lax
from jax.experimental import pallas as pl
from jax.experimental.pallas import tpu as pltpu

F32 = jnp.float32
BF16 = jnp.bfloat16
HIGHEST = lax.Precision.HIGHEST

D_MODEL = 1024
BATCH = 32
SEQ = 256
DEC_BATCH = 4
DEC_SEQ = 4096
PAST_LEN = 256
GRID_W = 64
NORM_EPS = 1e-6
N_PROMPT = BATCH * SEQ
N_SAMPLE = DEC_BATCH * DEC_SEQ
N_TOK = N_PROMPT + N_SAMPLE

SSD_HEADS = 8
SSD_HEAD_DIM = 64
SSD_STATE = 64
SSD_WIDTH = 512
SSD_BC = 128
SSD_XBC = 768
NA_HEADS = 8
NA_HEAD_DIM = 64
NA_WIDTH = 512
NA_WIN_ROWS = 8
NA_WIN_COLS = 16
NA_SCALE = NA_HEAD_DIM ** -0.5
GLA_HEADS = 4
GLA_DK = 64
GLA_DV = 128
GLA_RANK = 16
GLA_GATE_NORM = 16.0
HGRN_HEADS = 4
FFN_DIM = 2816

V7X_VMEM_BYTES = 64 * 1024 * 1024
VMEM_LIMIT = 56 * 1024 * 1024
SUBLANES = 8

TM = 512
N_TILES = N_TOK // TM
PROMPT_TILES = N_PROMPT // TM
TILES_PER_SAMPLE = DEC_SEQ // TM
CTX_MOD_ROW = DEC_BATCH

RB = 256
N_BLK = N_TOK // RB
PROMPT_BLKS = N_PROMPT // RB
BLKS_PER_SAMPLE = DEC_SEQ // RB
N_SEQ = BATCH + DEC_BATCH
SSD_L = 128
LIN_L = 64
NA_R = 8
FFN_CH = 256
W0_COLS = 2944
W1_COLS = 3456


def _cparams(n_axes=1):
    return pltpu.CompilerParams(dimension_semantics=("arbitrary",) * n_axes,
                                vmem_limit_bytes=VMEM_LIMIT)


def _const_spec(shape):
    nd = len(shape)
    return pl.BlockSpec(shape, lambda *_: (0,) * nd)


def _sigmoid(x):
    return 1.0 / (1.0 + jnp.exp(-x))


def _silu(x):
    return x * _sigmoid(x)


def _softplus(x):
    return jnp.maximum(x, 0.0) + jnp.log1p(jnp.exp(-jnp.abs(x)))


def _mod_row(i):
    return jnp.where(i < PROMPT_TILES, CTX_MOD_ROW, (i - PROMPT_TILES) // TILES_PER_SAMPLE)


def _dot(a, b, **kw):
    return jnp.dot(a, b, preferred_element_type=F32, **kw)


def _dot_nt(a, b):
    return lax.dot_general(a, b, (((1,), (1,)), ((), ())), preferred_element_type=F32)


def _dot_tn(a, b):
    return lax.dot_general(a, b, (((0,), (0,)), ((), ())), preferred_element_type=F32)


MODS_NB = 1536


def _mods_kernel(c_ref, w_ref, b_ref, o_ref):
    s = _silu(c_ref[...])
    o_ref[0] = _dot(s, w_ref[0], precision=HIGHEST) + b_ref[0]


def _mods(cvec8, w_ada, b_ada):
    depth = w_ada.shape[0]
    nb = 6 * D_MODEL // MODS_NB
    return pl.pallas_call(
        _mods_kernel,
        grid=(depth, nb),
        in_specs=[
            _const_spec((SUBLANES, D_MODEL)),
            pl.BlockSpec((1, D_MODEL, MODS_NB), lambda l, j: (l, 0, j)),
            pl.BlockSpec((1, 1, MODS_NB), lambda l, j: (l, 0, j)),
        ],
        out_specs=pl.BlockSpec((1, SUBLANES, MODS_NB), lambda l, j: (l, 0, j)),
        out_shape=jax.ShapeDtypeStruct((depth, SUBLANES, 6 * D_MODEL), F32),
        compiler_params=_cparams(2),
        name="mods",
    )(cvec8, w_ada, b_ada.reshape(depth, 1, 6 * D_MODEL))


def _norm_mod(x, nw, sh, sc):
    ms = jnp.mean(x * x, axis=-1, keepdims=True)
    y = x * lax.rsqrt(ms + NORM_EPS) * nw
    return y * (1.0 + sc) + sh


def _head_rms(x, seg, w):
    ms = _dot((x * x).astype(BF16), seg)
    return x * lax.rsqrt(ms + NORM_EPS) * w


def _inproj0_kernel(x_ref, m_ref, nw_ref, w_ref, seg_ref, qw_ref, kw_ref,
                    z_ref, xbc_ref, dt_ref, qn_ref, kn_ref, knb_ref, v_ref, vb_ref):
    h = _norm_mod(x_ref[...], nw_ref[...], m_ref[0, 0:1, :], m_ref[0, 1:2, :]).astype(BF16)
    z_ref[...] = _dot(h, w_ref[:, 0:512])
    xbc_ref[...] = _dot(h, w_ref[:, 512:1280])
    dt_ref[...] = _dot(h, w_ref[:, 1280:1408])
    q = _dot(h, w_ref[:, 1408:1920])
    qn_ref[...] = (_head_rms(q, seg_ref[...], qw_ref[...]) * NA_SCALE).astype(BF16)
    k = _dot(h, w_ref[:, 1920:2432])
    kn = _head_rms(k, seg_ref[...], kw_ref[...])
    kn_ref[...] = kn
    knb_ref[...] = kn.astype(BF16)
    v = _dot(h, w_ref[:, 2432:2944])
    v_ref[...] = v
    vb_ref[...] = v.astype(BF16)


def _row_spec(width):
    return pl.BlockSpec((TM, width), lambda i: (i, 0))


def _mod_spec():
    return pl.BlockSpec((1, SUBLANES, D_MODEL), lambda i: (_mod_row(i), 0, 0))


def _inproj0(x, mrows, nw, w0, seg, qw, kw):
    widths = (512, 768, 128, 512, 512, 512, 512, 512)
    dtypes = (F32, F32, F32, BF16, F32, BF16, F32, BF16)
    return pl.pallas_call(
        _inproj0_kernel,
        grid=(N_TILES,),
        in_specs=[
            _row_spec(D_MODEL), _mod_spec(), _const_spec((1, D_MODEL)),
            _const_spec((D_MODEL, W0_COLS)), _const_spec((512, 512)),
            _const_spec((1, 512)), _const_spec((1, 512)),
        ],
        out_specs=[_row_spec(w) for w in widths],
        out_shape=[jax.ShapeDtypeStruct((N_TOK, w), d) for w, d in zip(widths, dtypes)],
        compiler_params=_cparams(),
        name="inproj0",
    )(x, mrows, nw, w0, seg, qw, kw)


def _inproj1_kernel(x_ref, m_ref, nw_ref, w_ref, gqk_ref, gv_ref, gg_ref, hqf_ref, hi_ref, hg_ref):
    h = _norm_mod(x_ref[...], nw_ref[...], m_ref[0, 0:1, :], m_ref[0, 1:2, :]).astype(BF16)
    gqk_ref[...] = _dot(h, w_ref[:, 0:640])
    gv_ref[...] = _dot(h, w_ref[:, 640:1152])
    gg_ref[...] = _dot(h, w_ref[:, 1152:1664])
    hqf_ref[...] = _dot(h, w_ref[:, 1664:2432])
    hi_ref[...] = _dot(h, w_ref[:, 2432:2944])
    hg_ref[...] = _dot(h, w_ref[:, 2944:3456])


def _inproj1(x, mrows, nw, w1):
    widths = (640, 512, 512, 768, 512, 512)
    return pl.pallas_call(
        _inproj1_kernel,
        grid=(N_TILES,),
        in_specs=[_row_spec(D_MODEL), _mod_spec(), _const_spec((1, D_MODEL)),
                  _const_spec((D_MODEL, W1_COLS))],
        out_specs=[_row_spec(w) for w in widths],
        out_shape=[jax.ShapeDtypeStruct((N_TOK, w), F32) for w in widths],
        compiler_params=_cparams(),
        name="inproj1",
    )(x, mrows, nw, w1)


def _seq_of(j):
    jj = j - PROMPT_BLKS
    is_p = j < PROMPT_BLKS
    seq = jnp.where(is_p, j, BATCH + jj // BLKS_PER_SAMPLE)
    blk = jnp.where(is_p, 0, jj % BLKS_PER_SAMPLE)
    nblk = jnp.where(is_p, 1, BLKS_PER_SAMPLE)
    return seq, blk, nblk


def _bwd_blk(j):
    _, blk, nblk = _seq_of(j)
    return j - blk + (nblk - 1 - blk)


def _state_idx(j):
    seq, _, _ = _seq_of(j)
    return jnp.maximum(seq - BATCH, 0)


def _fwd_spec(width):
    return pl.BlockSpec((RB, width), lambda j: (j, 0))


def _bwd_spec(width):
    return pl.BlockSpec((RB, width), lambda j: (_bwd_blk(j), 0))


def _tri(n, upper):
    r = lax.broadcasted_iota(jnp.int32, (n, n), 0)
    c = lax.broadcasted_iota(jnp.int32, (n, n), 1)
    return (c >= r) if upper else (c <= r)


HALO_PER_BLK = RB // SUBLANES


def _ssd_kernel(xf_ref, xfp_ref, xfn_ref, dtf_ref, xb_ref, xbp_ref, xbn_ref, dtb_ref,
                s0f_ref, s0b_ref, cw_ref, cb_ref, dtbias_ref, alog_ref, dsk_ref,
                of_ref, ob_ref, sf_ref, sb_ref, st_ref, xc_ref):
    j = pl.program_id(0)
    _, blk, nblk = _seq_of(j)
    is_prompt = j < PROMPT_BLKS

    @pl.when(blk == 0)
    def _():
        st_ref[0] = jnp.where(is_prompt, 0.0, s0f_ref[0])
        st_ref[1] = jnp.where(is_prompt, 0.0, s0b_ref[0])

    rows = lax.broadcasted_iota(jnp.int32, (RB, 1), 0)
    cw = cw_ref[...]

    def conv(x_ref, p_ref, n_ref, b):
        x = x_ref[...]
        prev_row = jnp.where(b > 0, p_ref[SUBLANES - 1:SUBLANES, :], 0.0)
        next_row = jnp.where(b < nblk - 1, n_ref[0:1, :], 0.0)
        xm = jnp.where(rows == 0, prev_row, pltpu.roll(x, 1, axis=0))
        xp = jnp.where(rows == RB - 1, next_row, pltpu.roll(x, RB - 1, axis=0))
        u = cb_ref[...] + cw[0:1] * xm + cw[1:2] * x + cw[2:3] * xp
        return _silu(u)

    xc_ref[0] = conv(xf_ref, xfp_ref, xfn_ref, blk)
    xc_ref[1] = conv(xb_ref, xbp_ref, xbn_ref, nblk - 1 - blk)

    a_neg = -jnp.exp(alog_ref[...])
    L = SSD_L
    tri_lo = _tri(L, False)
    tri_up = _tri(L, True)

    def chunk(d, r0, dt_ref, o_ref):
        xc = xc_ref[d, r0:r0 + L, :]
        dt = _softplus(dt_ref[r0:r0 + L, :] + dtbias_ref[...])
        g = dt * a_neg
        mask = tri_up if d else tri_lo
        cum = _dot(mask.astype(F32), g, precision=HIGHEST)
        cum_t = cum.T
        last = cum[0:1, :] if d else cum[L - 1:L, :]
        e_in = jnp.exp(cum)
        e_end = jnp.exp(last - cum)
        e_last = jnp.exp(last)
        for gi in range(2):
            bg = xc[:, 512 + 64 * gi:576 + 64 * gi]
            cg = xc[:, 640 + 64 * gi:704 + 64 * gi]
            cbm = _dot_nt(cg.astype(BF16), bg.astype(BF16))
            for hh in range(4):
                h = 4 * gi + hh
                col = 8 * d + h
                pc = cum[:, col:col + 1]
                pr = cum_t[col:col + 1, :]
                dec = jnp.exp(jnp.where(mask, pc - pr, -jnp.inf))
                w = (cbm * dec).astype(BF16)
                xh = xc[:, 64 * h:64 * h + 64]
                v = (xh * dt[:, col:col + 1]).astype(BF16)
                s = st_ref[d, h]
                o = _dot(w, v) + _dot((cg * e_in[:, col:col + 1]).astype(BF16), s.astype(BF16))
                if d == 0:
                    o = o + dsk_ref[:, 64 * h:64 * h + 64] * xh
                o_ref[r0:r0 + L, 64 * h:64 * h + 64] = o
                kend = (bg * e_end[:, col:col + 1]).astype(BF16)
                st_ref[d, h] = e_last[:, col:col + 1] * s + _dot_tn(kend, v)

    nch = RB // L
    for c in range(nch):
        chunk(0, c * L, dtf_ref, of_ref)
        chunk(1, (nch - 1 - c) * L, dtb_ref, ob_ref)

    sf_ref[0] = st_ref[0]
    sb_ref[0] = st_ref[1]


def _halo_prev_spec(width, bwd):
    def idx(j):
        b = _bwd_blk(j) if bwd else j
        return (jnp.maximum(b * HALO_PER_BLK - 1, 0), 0)
    return pl.BlockSpec((SUBLANES, width), idx)


def _halo_next_spec(width, bwd):
    def idx(j):
        b = _bwd_blk(j) if bwd else j
        return (jnp.minimum((b + 1) * HALO_PER_BLK, N_TOK // SUBLANES - 1), 0)
    return pl.BlockSpec((SUBLANES, width), idx)


def _ssd(xbc, dt, s0f, s0b, conv_w, conv_b, dtbias, alog, dsk):
    st_spec = pl.BlockSpec((1, SSD_HEADS, SSD_STATE, SSD_HEAD_DIM), lambda j: (_state_idx(j), 0, 0, 0))
    so_spec = pl.BlockSpec((1, SSD_HEADS, SSD_STATE, SSD_HEAD_DIM), lambda j: (_seq_of(j)[0], 0, 0, 0))
    so_shape = jax.ShapeDtypeStruct((N_SEQ, SSD_HEADS, SSD_STATE, SSD_HEAD_DIM), F32)
    return pl.pallas_call(
        _ssd_kernel,
        grid=(N_BLK,),
        in_specs=[
            _fwd_spec(SSD_XBC), _halo_prev_spec(SSD_XBC, False), _halo_next_spec(SSD_XBC, False), _fwd_spec(128),
            _bwd_spec(SSD_XBC), _halo_prev_spec(SSD_XBC, True), _halo_next_spec(SSD_XBC, True), _bwd_spec(128),
            st_spec, st_spec,
            _const_spec((3, SSD_XBC)), _const_spec((1, SSD_XBC)), _const_spec((1, 128)),
            _const_spec((1, 128)), _const_spec((1, SSD_WIDTH)),
        ],
        out_specs=[_fwd_spec(SSD_WIDTH), _bwd_spec(SSD_WIDTH), so_spec, so_spec],
        out_shape=[jax.ShapeDtypeStruct((N_TOK, SSD_WIDTH), F32)] * 2 + [so_shape, so_shape],
        scratch_shapes=[pltpu.VMEM((2, SSD_HEADS, SSD_STATE, SSD_HEAD_DIM), F32),
                        pltpu.VMEM((2, RB, SSD_XBC), F32)],
        compiler_params=_cparams(),
        name="ssd",
    )(xbc, xbc, xbc, dt, xbc, xbc, xbc, dt, s0f, s0b, conv_w, conv_b, dtbias, alog, dsk)


def _na_ctx_kernel(q_ref, k_ref, v_ref, y_ref, ko_ref, vo_ref):
    for h in range(NA_HEADS):
        sl = slice(NA_HEAD_DIM * h, NA_HEAD_DIM * (h + 1))
        q = q_ref[:, sl]
        k = k_ref[:, sl]
        v = v_ref[:, sl]
        ko_ref[0, h] = k
        vo_ref[0, h] = v
        s = _dot_nt(q, k.astype(BF16))
        m = jnp.max(s, axis=-1, keepdims=True)
        p = jnp.exp(s - m)
        l = jnp.sum(p, axis=-1, keepdims=True)
        y_ref[:, sl] = _dot(p.astype(BF16), v.astype(BF16)) / l


def _na_ctx(qn, kn, v):
    blk = lambda w: pl.BlockSpec((SEQ, w), lambda b: (b, 0))
    hm = pl.BlockSpec((1, NA_HEADS, SEQ, NA_HEAD_DIM), lambda b: (b, 0, 0, 0))
    hm_shape = jax.ShapeDtypeStruct((BATCH, NA_HEADS, SEQ, NA_HEAD_DIM), F32)
    return pl.pallas_call(
        _na_ctx_kernel,
        grid=(BATCH,),
        in_specs=[blk(NA_WIDTH)] * 3,
        out_specs=[blk(NA_WIDTH), hm, hm],
        out_shape=[jax.ShapeDtypeStruct((N_TOK, NA_WIDTH), F32), hm_shape, hm_shape],
        compiler_params=_cparams(),
        name="na_ctx",
    )(qn, kn, v)


GRID_ROWS = DEC_SEQ // GRID_W
NA_LOC = NA_WIN_ROWS * GRID_W
NA_MASKED = -1e30


def _na_lat_kernel(q_ref, k_ref, v_ref, kc_ref, vc_ref, bt_ref, yin_ref, y_ref):
    del yin_ref
    rb = pl.program_id(1)

    def row(r, carry):
        rr = rb * NA_R + r
        rs = jnp.clip(rr - NA_WIN_ROWS // 2, 0, GRID_ROWS - NA_WIN_ROWS)
        dr0 = rs - rr + (NA_WIN_ROWS - 1)
        q0 = pl.multiple_of(r * GRID_W, GRID_W)
        k0 = pl.multiple_of(rs * GRID_W, GRID_W)
        for h in range(NA_HEADS):
            sl = slice(NA_HEAD_DIM * h, NA_HEAD_DIM * (h + 1))
            q = q_ref[pl.ds(q0, GRID_W), sl]
            kl = k_ref[pl.ds(k0, NA_LOC), sl]
            vl = v_ref[pl.ds(k0, NA_LOC), sl]
            s_loc = _dot_nt(q, kl) + bt_ref[dr0, h]
            s_ctx = _dot_nt(q, kc_ref[0, h].astype(BF16))
            m = jnp.maximum(jnp.max(s_loc, axis=-1, keepdims=True), jnp.max(s_ctx, axis=-1, keepdims=True))
            p_loc = jnp.exp(s_loc - m)
            p_ctx = jnp.exp(s_ctx - m)
            l = jnp.sum(p_loc, axis=-1, keepdims=True) + jnp.sum(p_ctx, axis=-1, keepdims=True)
            o = _dot(p_loc.astype(BF16), vl) + _dot(p_ctx.astype(BF16), vc_ref[0, h].astype(BF16))
            y_ref[pl.ds(q0, GRID_W), sl] = o / l
        return carry

    lax.fori_loop(0, NA_R, row, 0)


def _na_lat(qn, knb, vb, kc, vc, btab, y_in):
    rows_per_step = NA_R * GRID_W
    steps = GRID_ROWS // NA_R
    off_q = N_PROMPT // rows_per_step
    off_s = N_PROMPT // DEC_SEQ
    qspec = pl.BlockSpec((rows_per_step, NA_WIDTH), lambda b, r: (off_q + b * steps + r, 0))
    kvspec = pl.BlockSpec((DEC_SEQ, NA_WIDTH), lambda b, r: (off_s + b, 0))
    cspec = pl.BlockSpec((1, NA_HEADS, PAST_LEN, NA_HEAD_DIM), lambda b, r: (b, 0, 0, 0))
    return pl.pallas_call(
        _na_lat_kernel,
        grid=(DEC_BATCH, steps),
        in_specs=[qspec, kvspec, kvspec, cspec, cspec,
                  _const_spec((NA_WIN_ROWS, NA_HEADS, GRID_W, NA_LOC)),
                  pl.BlockSpec(memory_space=pl.ANY)],
        out_specs=qspec,
        out_shape=jax.ShapeDtypeStruct((N_TOK, NA_WIDTH), F32),
        input_output_aliases={6: 0},
        compiler_params=_cparams(2),
        name="na_lat",
    )(qn, knb, vb, kc, vc, btab, y_in)


def _na_bias_table(rpb):
    col = jnp.arange(GRID_W)
    col_start = jnp.clip(col - NA_WIN_COLS // 2, 0, GRID_W - NA_WIN_COLS)
    ok = (col[None, :] >= col_start[:, None]) & (col[None, :] < col_start[:, None] + NA_WIN_COLS)
    d_col = jnp.clip(col[None, :] - col[:, None], -(NA_WIN_COLS - 1), NA_WIN_COLS - 1) + (NA_WIN_COLS - 1)
    d_row = jnp.arange(NA_WIN_ROWS)[:, None] + jnp.arange(NA_WIN_ROWS)[None, :]
    b = rpb[:, d_row][:, :, :, d_col]
    b = jnp.where(ok[None, None, None], b, NA_MASKED)
    b = b.transpose(1, 0, 3, 2, 4)
    return b.reshape(NA_WIN_ROWS, NA_HEADS, GRID_W, NA_LOC)


LIN_HEADS = 4
LIN_DK = 64
LIN_DV = 128
LIN_QK = LIN_HEADS * LIN_DK
LIN_V = LIN_HEADS * LIN_DV


def _lin_chunk(q, k, g, v, d, st_ref, o_ref, r0, eye):
    L = LIN_L
    mask = _tri(L, bool(d))
    cum = _dot(mask.astype(F32), g, precision=HIGHEST)
    last = cum[0:1, :] if d else cum[L - 1:L, :]
    q_in = q * jnp.exp(cum)
    k_out = k * jnp.exp(-cum)
    k_end = k * jnp.exp(last - cum)
    e_last = jnp.exp(last)
    for h in range(LIN_HEADS):
        ks = slice(LIN_DK * h, LIN_DK * (h + 1))
        vs = slice(LIN_DV * h, LIN_DV * (h + 1))
        qh = q_in[:, ks].astype(BF16)
        a = _dot_nt(qh, k_out[:, ks].astype(BF16))
        a = jnp.where(mask, a, 0.0).astype(BF16)
        vh = v[:, vs].astype(BF16)
        s = st_ref[d, h]
        o_ref[r0:r0 + L, vs] = _dot(a, vh) + _dot(qh, s.astype(BF16))
        dcol = jnp.sum(jnp.where(eye, e_last[:, ks], 0.0), axis=1, keepdims=True)
        st_ref[d, h] = dcol * s + _dot_tn(k_end[:, ks].astype(BF16), vh)


def _log_sigmoid(x):
    return jnp.minimum(x, 0.0) - jnp.log1p(jnp.exp(-jnp.abs(x)))


def _lin_init(j, s0f_ref, s0b_ref, st_ref):
    _, blk, _ = _seq_of(j)
    is_prompt = j < PROMPT_BLKS

    @pl.when(blk == 0)
    def _():
        st_ref[0] = jnp.where(is_prompt, 0.0, s0f_ref[0])
        st_ref[1] = jnp.where(is_prompt, 0.0, s0b_ref[0])


def _gla_kernel(qkf_ref, vf_ref, qkb_ref, vb_ref, s0f_ref, s0b_ref, wa_ref, ba_ref,
                of_ref, ob_ref, sf_ref, sb_ref, st_ref):
    j = pl.program_id(0)
    _lin_init(j, s0f_ref, s0b_ref, st_ref)
    L = LIN_L
    eye = _tri(LIN_DK, False) & _tri(LIN_DK, True)
    nch = RB // L

    def one(d, r0, qk_ref, v_ref, o_ref):
        qk = qk_ref[r0:r0 + L, :]
        q = qk[:, 0:256] * (GLA_DK ** -0.5)
        k = qk[:, 256:512]
        ga = qk[:, 512 + GLA_RANK * d:512 + GLA_RANK * (d + 1)]
        g = _log_sigmoid(_dot(ga, wa_ref[d], precision=HIGHEST) + ba_ref[d]) / GLA_GATE_NORM
        _lin_chunk(q, k, g, v_ref[r0:r0 + L, :], d, st_ref, o_ref, r0, eye)

    for c in range(nch):
        one(0, c * L, qkf_ref, vf_ref, of_ref)
        one(1, (nch - 1 - c) * L, qkb_ref, vb_ref, ob_ref)
    sf_ref[0] = st_ref[0]
    sb_ref[0] = st_ref[1]


def _hgrn_kernel(qff_ref, vf_ref, qfb_ref, vb_ref, s0f_ref, s0b_ref, lbl_ref,
                 of_ref, ob_ref, sf_ref, sb_ref, st_ref):
    j = pl.program_id(0)
    _lin_init(j, s0f_ref, s0b_ref, st_ref)
    L = LIN_L
    eye = _tri(LIN_DK, False) & _tri(LIN_DK, True)
    nch = RB // L

    def lower_bound(d):
        l0 = lbl_ref[d, 0:1, :]
        l1 = lbl_ref[d, 1:2, :]
        m = jnp.maximum(l0, l1)
        e0 = jnp.exp(l0 - m)
        e1 = jnp.exp(l1 - m)
        p0 = e0 / (e0 + e1)
        p1 = e1 / (e0 + e1)
        return (p0 + p1) - p0

    lbs = (lower_bound(0), lower_bound(1))

    def one(d, r0, qf_ref, v_ref, o_ref):
        qf = qf_ref[r0:r0 + L, :]
        q = qf[:, 0:256]
        x = qf[:, 256 * (d + 1):256 * (d + 2)]
        f = lbs[d] + (1.0 - lbs[d]) * _sigmoid(x)
        _lin_chunk(q, 1.0 - f, jnp.log(f), v_ref[r0:r0 + L, :], d, st_ref, o_ref, r0, eye)

    for c in range(nch):
        one(0, c * L, qff_ref, vf_ref, of_ref)
        one(1, (nch - 1 - c) * L, qfb_ref, vb_ref, ob_ref)
    sf_ref[0] = st_ref[0]
    sb_ref[0] = st_ref[1]


def _lin_call(kernel, name, qk, v, s0f, s0b, params, qk_width):
    st_spec = pl.BlockSpec((1, LIN_HEADS, LIN_DK, LIN_DV), lambda j: (_state_idx(j), 0, 0, 0))
    so_spec = pl.BlockSpec((1, LIN_HEADS, LIN_DK, LIN_DV), lambda j: (_seq_of(j)[0], 0, 0, 0))
    so_shape = jax.ShapeDtypeStruct((N_SEQ, LIN_HEADS, LIN_DK, LIN_DV), F32)
    return pl.pallas_call(
        kernel,
        grid=(N_BLK,),
        in_specs=[_fwd_spec(qk_width), _fwd_spec(LIN_V), _bwd_spec(qk_width), _bwd_spec(LIN_V),
                  st_spec, st_spec] + [_const_spec(p.shape) for p in params],
        out_specs=[_fwd_spec(LIN_V), _bwd_spec(LIN_V), so_spec, so_spec],
        out_shape=[jax.ShapeDtypeStruct((N_TOK, LIN_V), F32)] * 2 + [so_shape, so_shape],
        scratch_shapes=[pltpu.VMEM((2, LIN_HEADS, LIN_DK, LIN_DV), F32)],
        compiler_params=_cparams(),
        name=name,
    )(qk, v, qk, v, s0f, s0b, *params)


def _outproj0_kernel(x_ref, m_ref, of_ref, ob_ref, z_ref, yb_ref, nw_ref, w_ref, o_ref):
    ya = (of_ref[...] + ob_ref[...]) * _silu(z_ref[...])
    ms = jnp.mean(ya * ya, axis=-1, keepdims=True)
    ya = ya * lax.rsqrt(ms + NORM_EPS) * nw_ref[...]
    y = _dot(ya.astype(BF16), w_ref[0:512, :]) + _dot(yb_ref[...].astype(BF16), w_ref[512:1024, :])
    o_ref[...] = x_ref[...] + m_ref[0, 2:3, :] * y


def _outproj0(x, mrows, of, ob, z, yb, nw, w):
    return pl.pallas_call(
        _outproj0_kernel,
        grid=(N_TILES,),
        in_specs=[_row_spec(D_MODEL), _mod_spec(), _row_spec(512), _row_spec(512), _row_spec(512),
                  _row_spec(512), _const_spec((1, 512)), _const_spec((D_MODEL, D_MODEL))],
        out_specs=_row_spec(D_MODEL),
        out_shape=jax.ShapeDtypeStruct((N_TOK, D_MODEL), F32),
        compiler_params=_cparams(),
        name="outproj0",
    )(x, mrows, of, ob, z, yb, nw, w)


def _head_rms128(o, w):
    parts = []
    for h in range(LIN_HEADS):
        oh = o[:, LIN_DV * h:LIN_DV * (h + 1)]
        ms = jnp.mean(oh * oh, axis=-1, keepdims=True)
        parts.append(oh * lax.rsqrt(ms + NORM_EPS) * w)
    return jnp.concatenate(parts, axis=-1)


def _outproj1_kernel(x_ref, m_ref, gf_ref, gb_ref, gg_ref, hf_ref, hb_ref, hg_ref,
                     gw_ref, hw_ref, w_ref, o_ref):
    yc = _head_rms128(gf_ref[...] + gb_ref[...], gw_ref[...]) * _silu(gg_ref[...])
    yd = _head_rms128(hf_ref[...] + hb_ref[...], hw_ref[...]) * _silu(hg_ref[...])
    y = _dot(yc.astype(BF16), w_ref[0:512, :]) + _dot(yd.astype(BF16), w_ref[512:1024, :])
    o_ref[...] = x_ref[...] + m_ref[0, 2:3, :] * y


def _outproj1(x, mrows, gf, gb, gg, hf, hb, hg, gw, hw, w):
    return pl.pallas_call(
        _outproj1_kernel,
        grid=(N_TILES,),
        in_specs=[_row_spec(D_MODEL), _mod_spec()] + [_row_spec(512)] * 6
                 + [_const_spec((1, LIN_DV)), _const_spec((1, LIN_DV)), _const_spec((D_MODEL, D_MODEL))],
        out_specs=_row_spec(D_MODEL),
        out_shape=jax.ShapeDtypeStruct((N_TOK, D_MODEL), F32),
        compiler_params=_cparams(),
        name="outproj1",
    )(x, mrows, gf, gb, gg, hf, hb, hg, gw, hw, w)


HALO_PER_TILE = TM // SUBLANES


def _ffn_kernel(x_ref, xp_ref, xn_ref, m_ref, nw_ref, wu_ref, cw_ref, cb_ref, wd_ref, o_ref, hs_ref, acc_ref):
    i = pl.program_id(0)
    nw = nw_ref[...]
    sh = m_ref[0, 3:4, :]
    sc = m_ref[0, 4:5, :]
    x = x_ref[...]
    hs_ref[0:TM, :] = _norm_mod(x, nw, sh, sc).astype(BF16)
    halo = jnp.concatenate([xp_ref[...], xn_ref[...]], axis=0)
    hs_ref[TM:TM + 2 * SUBLANES, :] = _norm_mod(halo, nw, sh, sc).astype(BF16)

    seq = jnp.where(i < PROMPT_TILES, SEQ, DEC_SEQ)
    pos = (i * TM + lax.broadcasted_iota(jnp.int32, (TM, 1), 0)) & (seq - 1)
    rows = lax.broadcasted_iota(jnp.int32, (TM, 1), 0)
    first = pos == 0
    lastp = pos == seq - 1

    def dwconv(u_all, c0):
        u = u_all[0:TM]
        prev_row = u_all[TM + SUBLANES - 1:TM + SUBLANES]
        next_row = u_all[TM + SUBLANES:TM + SUBLANES + 1]
        um = jnp.where(rows == 0, prev_row, pltpu.roll(u, 1, axis=0))
        up = jnp.where(rows == TM - 1, next_row, pltpu.roll(u, TM - 1, axis=0))
        um = jnp.where(first, 0.0, um)
        up = jnp.where(lastp, 0.0, up)
        cs = slice(c0, c0 + FFN_CH)
        return cb_ref[:, cs] + cw_ref[0:1, cs] * um + cw_ref[1:2, cs] * u + cw_ref[2:3, cs] * up

    hs = hs_ref[...]
    for c in range(FFN_DIM // FFN_CH):
        a = dwconv(_dot(hs, wu_ref[:, c * FFN_CH:(c + 1) * FFN_CH]), c * FFN_CH)
        b = dwconv(_dot(hs, wu_ref[:, FFN_DIM + c * FFN_CH:FFN_DIM + (c + 1) * FFN_CH]), FFN_DIM + c * FFN_CH)
        act = (_silu(a) * b).astype(BF16)
        part = _dot(act, wd_ref[c * FFN_CH:(c + 1) * FFN_CH, :])
        if c == 0:
            acc_ref[...] = part
        else:
            acc_ref[...] += part
    o_ref[...] = x + m_ref[0, 5:6, :] * acc_ref[...]


def _ffn(x, mrows, nw, wu, cw, cb, wd):
    prev = pl.BlockSpec((SUBLANES, D_MODEL), lambda i: (jnp.maximum(i * HALO_PER_TILE - 1, 0), 0))
    nxt = pl.BlockSpec((SUBLANES, D_MODEL),
                       lambda i: (jnp.minimum((i + 1) * HALO_PER_TILE, N_TOK // SUBLANES - 1), 0))
    single = dict(pipeline_mode=pl.Buffered(1))
    return pl.pallas_call(
        _ffn_kernel,
        grid=(N_TILES,),
        in_specs=[_row_spec(D_MODEL), prev, nxt, _mod_spec(), _const_spec((1, D_MODEL)),
                  pl.BlockSpec((D_MODEL, 2 * FFN_DIM), lambda i: (0, 0), **single),
                  _const_spec((3, 2 * FFN_DIM)), _const_spec((1, 2 * FFN_DIM)),
                  pl.BlockSpec((FFN_DIM, D_MODEL), lambda i: (0, 0), **single)],
        out_specs=_row_spec(D_MODEL),
        out_shape=jax.ShapeDtypeStruct((N_TOK, D_MODEL), F32),
        scratch_shapes=[pltpu.VMEM((TM + 2 * SUBLANES, D_MODEL), BF16), pltpu.VMEM((TM, D_MODEL), F32)],
        compiler_params=_cparams(),
        name="ffn",
    )(x, x, x, mrows, nw, wu, cw, cb, wd)


def _pad_lanes(v, width=128):
    v = v.reshape(1, -1)
    return jnp.pad(v, ((0, 0), (0, width - v.shape[1])))


def kernel(x_prompt, x_sample, cache_na_k_l0, cache_na_v_l0, state_ssd_fwd_l0, state_ssd_bwd_l0,
           state_gla_fwd_l1, state_gla_bwd_l1, state_hgrn_fwd_l1, state_hgrn_bwd_l1, c,
           c_ctx, w_ada, b_ada, norm_w, ffn_w_up, ffn_conv_w, ffn_conv_b, ffn_w_down,
           w_in_l0, w_out_l0, ssd_conv_w_l0, ssd_conv_b_l0, ssd_dt_bias_l0, ssd_a_log_l0, ssd_d_l0,
           ssd_norm_w_l0, na_q_norm_l0, na_k_norm_l0, na_rpb_l0,
           w_in_l1, w_out_l1, gla_wa2_l1, gla_ba2_l1, gla_norm_w_l1, hgrn_lb_logits, hgrn_norm_w_l1):
    x = jnp.concatenate([x_prompt.reshape(N_PROMPT, D_MODEL), x_sample.reshape(N_SAMPLE, D_MODEL)], axis=0)

    cvec8 = jnp.zeros((SUBLANES, D_MODEL), F32).at[0:DEC_BATCH].set(c).at[CTX_MOD_ROW].set(c_ctx)
    mods = _mods(cvec8, w_ada, b_ada)
    mods = mods.reshape(2, SUBLANES, 6, D_MODEL)
    mods = jnp.pad(mods, ((0, 0), (0, 0), (0, SUBLANES - 6), (0, 0)))

    zpad = lambda n: jnp.zeros((D_MODEL, n), F32)
    w0 = jnp.concatenate([w_in_l0[:, :1296], zpad(112), w_in_l0[:, 1296:]], axis=1).astype(BF16)
    w1 = jnp.concatenate([w_in_l1[:, 0:512], w_in_l1[:, 1536:1568], zpad(96),
                          w_in_l1[:, 512:1536], w_in_l1[:, 1568:3360]], axis=1).astype(BF16)
    seg = jnp.kron(jnp.eye(NA_HEADS, dtype=F32), jnp.full((NA_HEAD_DIM, NA_HEAD_DIM), 1.0 / NA_HEAD_DIM, F32)).astype(BF16)

    m0 = mods[0]
    z, xbc, dt, qn, kn, knb, v, vb = _inproj0(
        x, m0, norm_w[0, 0].reshape(1, D_MODEL), w0, seg,
        jnp.tile(na_q_norm_l0, NA_HEADS).reshape(1, NA_WIDTH), jnp.tile(na_k_norm_l0, NA_HEADS).reshape(1, NA_WIDTH))
    of, ob, ssd_f, ssd_b = _ssd(
        xbc, dt, state_ssd_fwd_l0, state_ssd_bwd_l0, ssd_conv_w_l0, ssd_conv_b_l0.reshape(1, SSD_XBC),
        _pad_lanes(ssd_dt_bias_l0), _pad_lanes(ssd_a_log_l0),
        jnp.repeat(ssd_d_l0, SSD_HEAD_DIM).reshape(1, SSD_WIDTH))
    yb, na_k, na_v = _na_ctx(qn, kn, v)
    yb = _na_lat(qn, knb, vb, cache_na_k_l0, cache_na_v_l0, _na_bias_table(na_rpb_l0), yb)
    x = _outproj0(x, m0, of, ob, z, yb, ssd_norm_w_l0.reshape(1, SSD_WIDTH), w_out_l0.astype(BF16))
    x = _ffn(x, m0, norm_w[0, 1].reshape(1, D_MODEL), ffn_w_up[0].astype(BF16), ffn_conv_w[0],
             ffn_conv_b[0].reshape(1, 2 * FFN_DIM), ffn_w_down[0].astype(BF16))

    m1 = mods[1]
    gqk, gv, gg, hqf, hi, hg = _inproj1(x, m1, norm_w[1, 0].reshape(1, D_MODEL), w1)
    gf, gb, gla_f, gla_b = _lin_call(
        _gla_kernel, "gla", gqk, gv, state_gla_fwd_l1, state_gla_bwd_l1,
        (gla_wa2_l1, gla_ba2_l1.reshape(2, 1, LIN_QK)), 640)
    hf, hb, hgrn_f, hgrn_b = _lin_call(
        _hgrn_kernel, "hgrn", hqf, hi, state_hgrn_fwd_l1, state_hgrn_bwd_l1, (hgrn_lb_logits,), 768)
    x = _outproj1(x, m1, gf, gb, gg, hf, hb, hg, gla_norm_w_l1.reshape(1, LIN_DV),
                  hgrn_norm_w_l1.reshape(1, LIN_DV), w_out_l1.astype(BF16))
    x = _ffn(x, m1, norm_w[1, 1].reshape(1, D_MODEL), ffn_w_up[1].astype(BF16), ffn_conv_w[1],
             ffn_conv_b[1].reshape(1, 2 * FFN_DIM), ffn_w_down[1].astype(BF16))

    y_p = x[:N_PROMPT].reshape(BATCH, SEQ, D_MODEL)
    y_s = x[N_PROMPT:].reshape(DEC_BATCH, DEC_SEQ, D_MODEL)
    return (y_p, y_s, na_k, na_v, ssd_f[:BATCH], ssd_b[:BATCH],
            gla_f[:BATCH], gla_b[:BATCH], hgrn_f[:BATCH], hgrn_b[:BATCH])
```

```python
import functools

import jax
import jax.numpy as jnp
from jax import lax
from jax.experimental import pallas as pl
from jax.experimental.pallas import tpu as pltpu

F32 = jnp.float32
BF16 = jnp.bfloat16
HIGHEST = lax.Precision.HIGHEST

D_MODEL = 1024
BATCH = 32
SEQ = 256
DEC_BATCH = 4
DEC_SEQ = 4096
PAST_LEN = 256
GRID_W = 64
NORM_EPS = 1e-6
N_PROMPT = BATCH * SEQ
N_SAMPLE = DEC_BATCH * DEC_SEQ
N_TOK = N_PROMPT + N_SAMPLE

SSD_HEADS = 8
SSD_HEAD_DIM = 64
SSD_STATE = 64
SSD_WIDTH = 512
SSD_BC = 128
SSD_XBC = 768
NA_HEADS = 8
NA_HEAD_DIM = 64
NA_WIDTH = 512
NA_WIN_ROWS = 8
NA_WIN_COLS = 16
NA_SCALE = NA_HEAD_DIM ** -0.5
GLA_HEADS = 4
GLA_DK = 64
GLA_DV = 128
GLA_RANK = 16
GLA_GATE_NORM = 16.0
HGRN_HEADS = 4
FFN_DIM = 2816

V7X_VMEM_BYTES = 64 * 1024 * 1024
VMEM_LIMIT = 56 * 1024 * 1024
SUBLANES = 8

TM = 512
N_TILES = N_TOK // TM
PROMPT_TILES = N_PROMPT // TM
TILES_PER_SAMPLE = DEC_SEQ // TM
CTX_MOD_ROW = DEC_BATCH

RB = 256
N_BLK = N_TOK // RB
PROMPT_BLKS = N_PROMPT // RB
BLKS_PER_SAMPLE = DEC_SEQ // RB
N_SEQ = BATCH + DEC_BATCH
SSD_L = 128
LIN_L = 64
NA_R = 8
FFN_CH = 256
W0_COLS = 2944
W1_COLS = 3456


def _cparams(n_axes=1):
    return pltpu.CompilerParams(dimension_semantics=("arbitrary",) * n_axes,
                                vmem_limit_bytes=VMEM_LIMIT)


def _const_spec(shape):
    nd = len(shape)
    return pl.BlockSpec(shape, lambda *_: (0,) * nd)


def _sigmoid(x):
    return 1.0 / (1.0 + jnp.exp(-x))


def _silu(x):
    return x * _sigmoid(x)


def _softplus(x):
    return jnp.maximum(x, 0.0) + jnp.log1p(jnp.exp(-jnp.abs(x)))


def _mod_row(i):
    return jnp.where(i < PROMPT_TILES, CTX_MOD_ROW, (i - PROMPT_TILES) // TILES_PER_SAMPLE)


def _dot(a, b, **kw):
    return jnp.dot(a, b, preferred_element_type=F32, **kw)


def _dot_nt(a, b):
    return lax.dot_general(a, b, (((1,), (1,)), ((), ())), preferred_element_type=F32)


def _dot_tn(a, b):
    return lax.dot_general(a, b, (((0,), (0,)), ((), ())), preferred_element_type=F32)


MODS_NB = 1536


def _mods_kernel(c_ref, w_ref, b_ref, o_ref):
    s = _silu(c_ref[...])
    o_ref[0] = _dot(s, w_ref[0], precision=HIGHEST) + b_ref[0]


def _mods(cvec8, w_ada, b_ada):
    depth = w_ada.shape[0]
    nb = 6 * D_MODEL // MODS_NB
    return pl.pallas_call(
        _mods_kernel,
        grid=(depth, nb),
        in_specs=[
            _const_spec((SUBLANES, D_MODEL)),
            pl.BlockSpec((1, D_MODEL, MODS_NB), lambda l, j: (l, 0, j)),
            pl.BlockSpec((1, 1, MODS_NB), lambda l, j: (l, 0, j)),
        ],
        out_specs=pl.BlockSpec((1, SUBLANES, MODS_NB), lambda l, j: (l, 0, j)),
        out_shape=jax.ShapeDtypeStruct((depth, SUBLANES, 6 * D_MODEL), F32),
        compiler_params=_cparams(2),
        name="mods",
    )(cvec8, w_ada, b_ada.reshape(depth, 1, 6 * D_MODEL))


def _norm_mod(x, nw, sh, sc):
    ms = jnp.mean(x * x, axis=-1, keepdims=True)
    y = x * lax.rsqrt(ms + NORM_EPS) * nw
    return y * (1.0 + sc) + sh


def _head_rms(x, seg, w):
    ms = _dot((x * x).astype(BF16), seg)
    return x * lax.rsqrt(ms + NORM_EPS) * w


def _inproj0_kernel(x_ref, m_ref, nw_ref, w_ref, seg_ref, qw_ref, kw_ref,
                    z_ref, xbc_ref, dt_ref, qn_ref, kn_ref, knb_ref, v_ref, vb_ref):
    h = _norm_mod(x_ref[...], nw_ref[...], m_ref[0, 0:1, :], m_ref[0, 1:2, :]).astype(BF16)
    z_ref[...] = _dot(h, w_ref[:, 0:512])
    xbc_ref[...] = _dot(h, w_ref[:, 512:1280])
    dt_ref[...] = _dot(h, w_ref[:, 1280:1408])
    q = _dot(h, w_ref[:, 1408:1920])
    qn_ref[...] = (_head_rms(q, seg_ref[...], qw_ref[...]) * NA_SCALE).astype(BF16)
    k = _dot(h, w_ref[:, 1920:2432])
    kn = _head_rms(k, seg_ref[...], kw_ref[...])
    kn_ref[...] = kn
    knb_ref[...] = kn.astype(BF16)
    v = _dot(h, w_ref[:, 2432:2944])
    v_ref[...] = v
    vb_ref[...] = v.astype(BF16)


def _row_spec(width):
    return pl.BlockSpec((TM, width), lambda i: (i, 0))


def _mod_spec():
    return pl.BlockSpec((1, SUBLANES, D_MODEL), lambda i: (_mod_row(i), 0, 0))


def _inproj0(x, mrows, nw, w0, seg, qw, kw):
    widths = (512, 768, 128, 512, 512, 512, 512, 512)
    dtypes = (F32, F32, F32, BF16, F32, BF16, F32, BF16)
    return pl.pallas_call(
        _inproj0_kernel,
        grid=(N_TILES,),
        in_specs=[
            _row_spec(D_MODEL), _mod_spec(), _const_spec((1, D_MODEL)),
            _const_spec((D_MODEL, W0_COLS)), _const_spec((512, 512)),
            _const_spec((1, 512)), _const_spec((1, 512)),
        ],
        out_specs=[_row_spec(w) for w in widths],
        out_shape=[jax.ShapeDtypeStruct((N_TOK, w), d) for w, d in zip(widths, dtypes)],
        compiler_params=_cparams(),
        name="inproj0",
    )(x, mrows, nw, w0, seg, qw, kw)


def _inproj1_kernel(x_ref, m_ref, nw_ref, w_ref, gqk_ref, gv_ref, gg_ref, hqf_ref, hi_ref, hg_ref):
    h = _norm_mod(x_ref[...], nw_ref[...], m_ref[0, 0:1, :], m_ref[0, 1:2, :]).astype(BF16)
    gqk_ref[...] = _dot(h, w_ref[:, 0:640])
    gv_ref[...] = _dot(h, w_ref[:, 640:1152])
    gg_ref[...] = _dot(h, w_ref[:, 1152:1664])
    hqf_ref[...] = _dot(h, w_ref[:, 1664:2432])
    hi_ref[...] = _dot(h, w_ref[:, 2432:2944])
    hg_ref[...] = _dot(h, w_ref[:, 2944:3456])


def _inproj1(x, mrows, nw, w1):
    widths = (640, 512, 512, 768, 512, 512)
    return pl.pallas_call(
        _inproj1_kernel,
        grid=(N_TILES,),
        in_specs=[_row_spec(D_MODEL), _mod_spec(), _const_spec((1, D_MODEL)),
                  _const_spec((D_MODEL, W1_COLS))],
        out_specs=[_row_spec(w) for w in widths],
        out_shape=[jax.ShapeDtypeStruct((N_TOK, w), F32) for w in widths],
        compiler_params=_cparams(),
        name="inproj1",
    )(x, mrows, nw, w1)


def _seq_of(j):
    jj = j - PROMPT_BLKS
    is_p = j < PROMPT_BLKS
    seq = jnp.where(is_p, j, BATCH + jj // BLKS_PER_SAMPLE)
    blk = jnp.where(is_p, 0, jj % BLKS_PER_SAMPLE)
    nblk = jnp.where(is_p, 1, BLKS_PER_SAMPLE)
    return seq, blk, nblk


def _bwd_blk(j):
    _, blk, nblk = _seq_of(j)
    return j - blk + (nblk - 1 - blk)


def _state_idx(j):
    seq, _, _ = _seq_of(j)
    return jnp.maximum(seq - BATCH, 0)


def _fwd_spec(width):
    return pl.BlockSpec((RB, width), lambda j: (j, 0))


def _bwd_spec(width):
    return pl.BlockSpec((RB, width), lambda j: (_bwd_blk(j), 0))


def _tri(n, upper):
    r = lax.broadcasted_iota(jnp.int32, (n, n), 0)
    c = lax.broadcasted_iota(jnp.int32, (n, n), 1)
    return (c >= r) if upper else (c <= r)


HALO_PER_BLK = RB // SUBLANES


def _diag_blocks(x, rows):
    lh = lax.broadcasted_iota(jnp.int32, (1, 4 * SSD_HEAD_DIM), 1) >> 6
    out = x[3 * rows:4 * rows]
    for h in (2, 1, 0):
        out = jnp.where(lh == h, x[h * rows:(h + 1) * rows], out)
    return out


def _ssd_kernel(xf_ref, xfp_ref, xfn_ref, dtf_ref, xb_ref, xbp_ref, xbn_ref, dtb_ref,
                s0f_ref, s0b_ref, cw_ref, cb_ref, dtbias_ref, alog_ref, dsk_ref, ex_ref,
                of_ref, ob_ref, sf_ref, sb_ref, st_ref, xc_ref, v_ref):
    j = pl.program_id(0)
    _, blk, nblk = _seq_of(j)
    is_prompt = j < PROMPT_BLKS

    @pl.when(blk == 0)
    def _():
        st_ref[0] = jnp.where(is_prompt, 0.0, s0f_ref[0])
        st_ref[1] = jnp.where(is_prompt, 0.0, s0b_ref[0])

    rows = lax.broadcasted_iota(jnp.int32, (RB, 1), 0)
    cw = cw_ref[...]

    def conv(x_ref, p_ref, n_ref, b):
        x = x_ref[...]
        prev_row = jnp.where(b > 0, p_ref[SUBLANES - 1:SUBLANES, :], 0.0)
        next_row = jnp.where(b < nblk - 1, n_ref[0:1, :], 0.0)
        xm = jnp.where(rows == 0, prev_row, pltpu.roll(x, 1, axis=0))
        xp = jnp.where(rows == RB - 1, next_row, pltpu.roll(x, RB - 1, axis=0))
        u = cb_ref[...] + cw[0:1] * xm + cw[1:2] * x + cw[2:3] * xp
        return _silu(u)

    xc_ref[0] = conv(xf_ref, xfp_ref, xfn_ref, blk)
    xc_ref[1] = conv(xb_ref, xbp_ref, xbn_ref, nblk - 1 - blk)

    a_neg = -jnp.exp(alog_ref[...])
    L = SSD_L
    nch = RB // L
    dt_refs = (dtf_ref, dtb_ref)
    o_refs = (of_ref, ob_ref)
    gs = []
    for d in range(2):
        dt = _softplus(dt_refs[d][...] + dtbias_ref[...])
        dt_hi, dt_lo = _split2(dt)
        dtx = _dot(dt_hi, ex_ref[d]) + _dot(dt_lo, ex_ref[d])
        v_ref[d] = (xc_ref[d, :, 0:SSD_WIDTH] * dtx).astype(BF16)
        gs.append(dt * a_neg)

    units = [(d, c) for c in range(nch) for d in range(2)]
    row0 = {(d, c): ((nch - 1 - c) * L if d else c * L) for d, c in units}
    tri = [_tri(L, False), _tri(L, True)]
    tri_bf = [jnp.where(t, 1.0, 0.0).astype(BF16) for t in tri]
    lane = lax.broadcasted_iota(jnp.int32, (1, SSD_BC), 1)
    gmask = [lane < SSD_STATE, lane >= SSD_STATE]
    eye_bf = jnp.where(tri[0] & tri[1], 1.0, 0.0).astype(BF16)
    zero_blk = jnp.zeros((SSD_STATE, 4 * SSD_HEAD_DIM), F32)

    cum = {u: _dot_exact_lhs(tri_bf[u[0]], gs[u[0]][row0[u]:row0[u] + L]) for u in units}
    ct8, e_in_x, e_last_x, e_end_t, b_t, cb_g, cm_bf = {}, {}, {}, {}, {}, {}, {}
    for u in units:
        d, _ = u
        r0 = row0[u]
        cu = cum[u]
        ct8[u] = cu.T[8 * d:8 * d + 8]
        hi, lo = _split2(jnp.exp(cu))
        e_in_x[u] = _dot(hi, ex_ref[d]) + _dot(lo, ex_ref[d])
        parts = _split3(cu[0:SUBLANES] if d else cu[L - SUBLANES:L])
        last_x = _dot(parts[0], ex_ref[d]) + _dot(parts[1], ex_ref[d]) + _dot(parts[2], ex_ref[d])
        e_last_x[u] = jnp.exp(last_x[0:1] if d else last_x[SUBLANES - 1:SUBLANES])
        last_col = ct8[u][:, 0:1] if d else ct8[u][:, L - 1:L]
        e_end_t[u] = jnp.exp(last_col - ct8[u])
        bm = xc_ref[d, r0:r0 + L, 512:640].astype(BF16)
        cm = xc_ref[d, r0:r0 + L, 640:768]
        b_t[u] = _dot_nt(eye_bf, bm)
        cb_g[u] = [_dot_nt(jnp.where(gmask[g], cm, 0.0).astype(BF16), bm) for g in range(2)]
        cm_bf[u] = cm.astype(BF16)

    w, k_t = {}, {}
    for u in units:
        d, _ = u
        for g in range(2):
            ws, ks = [], []
            for hh in range(4):
                h = 4 * g + hh
                pc = cum[u][:, 8 * d + h:8 * d + h + 1]
                pr = ct8[u][h:h + 1, :]
                dec = jnp.exp(jnp.where(tri[d], pc - pr, -jnp.inf))
                ws.append((cb_g[u][g] * dec).astype(BF16))
                ks.append((b_t[u][SSD_STATE * g:SSD_STATE * (g + 1)] * e_end_t[u][h:h + 1, :]).astype(BF16))
            w[(u, g)] = jnp.concatenate(ws, axis=0)
            k_t[(u, g)] = jnp.concatenate(ks, axis=0)

    o_intra, upd = {}, {}
    for u in units:
        d, _ = u
        r0 = row0[u]
        oi, up = [], []
        for g in range(2):
            vg = v_ref[d, r0:r0 + L, 256 * g:256 * (g + 1)]
            oi.append(_diag_blocks(_dot(w[(u, g)], vg), L))
            up.append(_diag_blocks(_dot(k_t[(u, g)], vg), SSD_STATE))
        o_intra[u] = jnp.concatenate(oi, axis=1)
        upd[u] = jnp.concatenate([jnp.concatenate([up[0], zero_blk], axis=1),
                                  jnp.concatenate([zero_blk, up[1]], axis=1)], axis=0)

    s = [st_ref[0], st_ref[1]]
    for u in units:
        d, _ = u
        r0 = row0[u]
        o = o_intra[u] + e_in_x[u] * _dot(cm_bf[u], s[d].astype(BF16))
        if d == 0:
            o = o + dsk_ref[...] * xc_ref[0, r0:r0 + L, 0:SSD_WIDTH]
        o_refs[d][r0:r0 + L, :] = o
        s[d] = e_last_x[u] * s[d] + upd[u]
    st_ref[0] = s[0]
    st_ref[1] = s[1]
    sf_ref[0] = s[0]
    sb_ref[0] = s[1]


def _halo_prev_spec(width, bwd):
    def idx(j):
        b = _bwd_blk(j) if bwd else j
        return (jnp.maximum(b * HALO_PER_BLK - 1, 0), 0)
    return pl.BlockSpec((SUBLANES, width), idx)


def _halo_next_spec(width, bwd):
    def idx(j):
        b = _bwd_blk(j) if bwd else j
        return (jnp.minimum((b + 1) * HALO_PER_BLK, N_TOK // SUBLANES - 1), 0)
    return pl.BlockSpec((SUBLANES, width), idx)


def _ssd_pack_state(s):
    b = s.shape[0]
    g = s.reshape(b, 2, 4, SSD_STATE, SSD_HEAD_DIM).transpose(0, 1, 3, 2, 4).reshape(b, 2, SSD_STATE, 256)
    z = jnp.zeros((b, SSD_STATE, 256), F32)
    return jnp.concatenate([jnp.concatenate([g[:, 0], z], axis=2), jnp.concatenate([z, g[:, 1]], axis=2)], axis=1)


def _ssd_unpack_state(s):
    b = s.shape[0]
    g = jnp.stack([s[:, 0:SSD_STATE, 0:256], s[:, SSD_STATE:, 256:512]], axis=1)
    g = g.reshape(b, 2, SSD_STATE, 4, SSD_HEAD_DIM).transpose(0, 1, 3, 2, 4)
    return g.reshape(b, SSD_HEADS, SSD_STATE, SSD_HEAD_DIM)


def _ssd(xbc, dt, s0f, s0b, conv_w, conv_b, dtbias, alog, dsk):
    st_shape = (1, 2 * SSD_STATE, SSD_WIDTH)
    st_spec = pl.BlockSpec(st_shape, lambda j: (_state_idx(j), 0, 0))
    so_spec = pl.BlockSpec(st_shape, lambda j: (_seq_of(j)[0], 0, 0))
    so_shape = jax.ShapeDtypeStruct((N_SEQ, 2 * SSD_STATE, SSD_WIDTH), F32)
    col = jnp.arange(128)[:, None]
    lane_head = jnp.arange(SSD_WIDTH)[None, :] // SSD_HEAD_DIM
    expand = jnp.stack([col == lane_head, col == SSD_HEADS + lane_head]).astype(BF16)
    of, ob, sf, sb = pl.pallas_call(
        _ssd_kernel,
        grid=(N_BLK,),
        in_specs=[
            _fwd_spec(SSD_XBC), _halo_prev_spec(SSD_XBC, False), _halo_next_spec(SSD_XBC, False), _fwd_spec(128),
            _bwd_spec(SSD_XBC), _halo_prev_spec(SSD_XBC, True), _halo_next_spec(SSD_XBC, True), _bwd_spec(128),
            st_spec, st_spec,
            _const_spec((3, SSD_XBC)), _const_spec((1, SSD_XBC)), _const_spec((1, 128)),
            _const_spec((1, 128)), _const_spec((1, SSD_WIDTH)), _const_spec((2, 128, SSD_WIDTH)),
        ],
        out_specs=[_fwd_spec(SSD_WIDTH), _bwd_spec(SSD_WIDTH), so_spec, so_spec],
        out_shape=[jax.ShapeDtypeStruct((N_TOK, SSD_WIDTH), F32)] * 2 + [so_shape, so_shape],
        scratch_shapes=[pltpu.VMEM((2, 2 * SSD_STATE, SSD_WIDTH), F32),
                        pltpu.VMEM((2, RB, SSD_XBC), F32),
                        pltpu.VMEM((2, RB, SSD_WIDTH), BF16)],
        compiler_params=_cparams(),
        name="ssd",
    )(xbc, xbc, xbc, dt, xbc, xbc, xbc, dt, _ssd_pack_state(s0f), _ssd_pack_state(s0b),
      conv_w, conv_b, dtbias, alog, dsk, expand)
    return of, ob, _ssd_unpack_state(sf[:BATCH]), _ssd_unpack_state(sb[:BATCH])


NA_PAIRS = NA_HEADS // 2


def _stack_pair(qt):
    lower = lax.broadcasted_iota(jnp.int32, (1, 2 * NA_HEAD_DIM), 1) < NA_HEAD_DIM
    zero = jnp.zeros_like(qt)
    return jnp.concatenate([jnp.where(lower, qt, zero), jnp.where(lower, zero, qt)], axis=0)


def _unstack_pair(x, n):
    lower = lax.broadcasted_iota(jnp.int32, (1, 2 * NA_HEAD_DIM), 1) < NA_HEAD_DIM
    return jnp.where(lower, x[0:n], x[n:2 * n])


def _na_ctx_kernel(q_ref, k_ref, v_ref, y_ref, ko_ref, vo_ref):
    tiles = [slice(128 * i, 128 * (i + 1)) for i in range(NA_PAIRS)]
    s = [_dot_nt(_stack_pair(q_ref[:, ts]), k_ref[:, ts].astype(BF16)) for ts in tiles]
    p, l = [], []
    for i in range(NA_PAIRS):
        e = jnp.exp(s[i] - jnp.max(s[i], axis=-1, keepdims=True))
        l.append(jnp.sum(e, axis=-1, keepdims=True))
        p.append(e.astype(BF16))
    o = [_dot(p[i], v_ref[:, tiles[i]].astype(BF16)) for i in range(NA_PAIRS)]
    for i in range(NA_PAIRS):
        y_ref[:, tiles[i]] = _unstack_pair(o[i], SEQ) / _unstack_pair(jnp.broadcast_to(l[i], o[i].shape), SEQ)
    for h in range(NA_HEADS):
        sl = slice(NA_HEAD_DIM * h, NA_HEAD_DIM * (h + 1))
        ko_ref[0, h] = k_ref[:, sl]
        vo_ref[0, h] = v_ref[:, sl]


def _na_ctx(qn, kn, v):
    blk = lambda w: pl.BlockSpec((SEQ, w), lambda b: (b, 0))
    hm = pl.BlockSpec((1, NA_HEADS, SEQ, NA_HEAD_DIM), lambda b: (b, 0, 0, 0))
    hm_shape = jax.ShapeDtypeStruct((BATCH, NA_HEADS, SEQ, NA_HEAD_DIM), F32)
    return pl.pallas_call(
        _na_ctx_kernel,
        grid=(BATCH,),
        in_specs=[blk(NA_WIDTH)] * 3,
        out_specs=[blk(NA_WIDTH), hm, hm],
        out_shape=[jax.ShapeDtypeStruct((N_TOK, NA_WIDTH), F32), hm_shape, hm_shape],
        compiler_params=_cparams(),
        name="na_ctx",
    )(qn, kn, v)


GRID_ROWS = DEC_SEQ // GRID_W
NA_LOC = NA_WIN_ROWS * GRID_W
NA_MASKED = -1e30


def _na_lat_kernel(q_ref, k_ref, v_ref, kc_ref, vc_ref, bt_ref, yin_ref, y_ref):
    del yin_ref
    rb = pl.program_id(1)

    def row(r, carry):
        rr = rb * NA_R + r
        rs = jnp.clip(rr - NA_WIN_ROWS // 2, 0, GRID_ROWS - NA_WIN_ROWS)
        dr0 = rs - rr + (NA_WIN_ROWS - 1)
        q0 = pl.multiple_of(r * GRID_W, GRID_W)
        k0 = pl.multiple_of(rs * GRID_W, GRID_W)
        tiles = [slice(128 * i, 128 * (i + 1)) for i in range(NA_PAIRS)]
        qq = [_stack_pair(q_ref[pl.ds(q0, GRID_W), ts]) for ts in tiles]
        s_loc = [_dot_nt(qq[i], k_ref[pl.ds(k0, NA_LOC), tiles[i]])
                 + bt_ref[dr0, 2 * i:2 * i + 2].reshape(2 * GRID_W, NA_LOC) for i in range(NA_PAIRS)]
        s_ctx = [_dot_nt(qq[i], kc_ref[0, :, tiles[i]]) for i in range(NA_PAIRS)]
        p_loc, p_ctx, l = [], [], []
        for i in range(NA_PAIRS):
            m = jnp.maximum(jnp.max(s_loc[i], axis=-1, keepdims=True), jnp.max(s_ctx[i], axis=-1, keepdims=True))
            e_loc = jnp.exp(s_loc[i] - m)
            e_ctx = jnp.exp(s_ctx[i] - m)
            l.append(jnp.sum(e_loc, axis=-1, keepdims=True) + jnp.sum(e_ctx, axis=-1, keepdims=True))
            p_loc.append(e_loc.astype(BF16))
            p_ctx.append(e_ctx.astype(BF16))
        o = [_dot(p_loc[i], v_ref[pl.ds(k0, NA_LOC), tiles[i]]) + _dot(p_ctx[i], vc_ref[0, :, tiles[i]])
             for i in range(NA_PAIRS)]
        for i in range(NA_PAIRS):
            y_ref[pl.ds(q0, GRID_W), tiles[i]] = (
                _unstack_pair(o[i], GRID_W) / _unstack_pair(jnp.broadcast_to(l[i], o[i].shape), GRID_W))
        return carry

    lax.fori_loop(0, NA_R, row, 0)


def _na_lat(qn, knb, vb, kc, vc, btab, y_in):
    rows_per_step = NA_R * GRID_W
    steps = GRID_ROWS // NA_R
    off_q = N_PROMPT // rows_per_step
    off_s = N_PROMPT // DEC_SEQ
    qspec = pl.BlockSpec((rows_per_step, NA_WIDTH), lambda b, r: (off_q + b * steps + r, 0))
    kvspec = pl.BlockSpec((DEC_SEQ, NA_WIDTH), lambda b, r: (off_s + b, 0))
    cspec = pl.BlockSpec((1, PAST_LEN, NA_WIDTH), lambda b, r: (b, 0, 0))
    token_major = lambda a: a.transpose(0, 2, 1, 3).reshape(DEC_BATCH, PAST_LEN, NA_WIDTH).astype(BF16)
    kc, vc = token_major(kc), token_major(vc)
    return pl.pallas_call(
        _na_lat_kernel,
        grid=(DEC_BATCH, steps),
        in_specs=[qspec, kvspec, kvspec, cspec, cspec,
                  _const_spec((NA_WIN_ROWS, NA_HEADS, GRID_W, NA_LOC)),
                  pl.BlockSpec(memory_space=pl.ANY)],
        out_specs=qspec,
        out_shape=jax.ShapeDtypeStruct((N_TOK, NA_WIDTH), F32),
        input_output_aliases={6: 0},
        compiler_params=_cparams(2),
        name="na_lat",
    )(qn, knb, vb, kc, vc, btab, y_in)


def _na_bias_table(rpb):
    col = jnp.arange(GRID_W)
    col_start = jnp.clip(col - NA_WIN_COLS // 2, 0, GRID_W - NA_WIN_COLS)
    ok = (col[None, :] >= col_start[:, None]) & (col[None, :] < col_start[:, None] + NA_WIN_COLS)
    d_col = jnp.clip(col[None, :] - col[:, None], -(NA_WIN_COLS - 1), NA_WIN_COLS - 1) + (NA_WIN_COLS - 1)
    d_row = jnp.arange(NA_WIN_ROWS)[:, None] + jnp.arange(NA_WIN_ROWS)[None, :]
    b = rpb[:, d_row][:, :, :, d_col]
    b = jnp.where(ok[None, None, None], b, NA_MASKED)
    b = b.transpose(1, 0, 3, 2, 4)
    return b.reshape(NA_WIN_ROWS, NA_HEADS, GRID_W, NA_LOC)


LIN_HEADS = 4
LIN_DK = 64
LIN_DV = 128
LIN_QK = LIN_HEADS * LIN_DK
LIN_V = LIN_HEADS * LIN_DV


def _split2(x):
    hi = x.astype(BF16)
    lo = (x - hi.astype(F32)).astype(BF16)
    return hi, lo


def _split3(x):
    hi = x.astype(BF16)
    r = x - hi.astype(F32)
    mid = r.astype(BF16)
    lo = (r - mid.astype(F32)).astype(BF16)
    return hi, mid, lo


def _dot_exact_lhs(a_bf, x):
    hi, mid, lo = _split3(x)
    return _dot(a_bf, hi) + _dot(a_bf, mid) + _dot(a_bf, lo)


def _log_sigmoid(x):
    return jnp.minimum(x, 0.0) - jnp.log1p(jnp.exp(-jnp.abs(x)))


def _lin_scan(chunk_inputs, v_refs, o_refs, st_ref):
    L = LIN_L
    nch = RB // L
    lane = lax.broadcasted_iota(jnp.int32, (1, LIN_QK), 1)
    head_mask = [(lane >> 6) == h for h in range(LIN_HEADS)]
    r4 = lax.broadcasted_iota(jnp.int32, (LIN_HEADS * L, L), 0) & (L - 1)
    c4 = lax.broadcasted_iota(jnp.int32, (LIN_HEADS * L, L), 1)
    er = lax.broadcasted_iota(jnp.int32, (LIN_QK, LIN_QK), 0)
    ec = lax.broadcasted_iota(jnp.int32, (LIN_QK, LIN_QK), 1)
    eye = er == ec

    units = [(d, c) for c in range(nch) for d in range(2)]
    row0 = {(d, c): ((nch - 1 - c) * L if d else c * L) for d, c in units}
    tri_bf = [jnp.where(_tri(L, bool(d)), 1.0, 0.0).astype(BF16) for d in range(2)]
    tri4 = [c4 <= r4, c4 >= r4]

    qkg = {u: chunk_inputs(u[0], row0[u]) for u in units}
    cum = {u: _dot_exact_lhs(tri_bf[u[0]], qkg[u][2]) for u in units}
    qs, k_out, k_end, dcol, v = {}, {}, {}, {}, {}
    for u in units:
        d, _ = u
        q, k, _ = qkg[u]
        last = cum[u][0:1, :] if d else cum[u][L - 1:L, :]
        q_in = q * jnp.exp(cum[u])
        k_out[u] = (k * jnp.exp(-cum[u])).astype(BF16)
        k_end[u] = (k * jnp.exp(last - cum[u])).astype(BF16)
        qs[u] = jnp.concatenate([jnp.where(m, q_in, 0.0) for m in head_mask], axis=0).astype(BF16)
        dcol[u] = jnp.sum(jnp.where(eye, jnp.exp(last), 0.0), axis=1, keepdims=True)
        v[u] = v_refs[d][row0[u]:row0[u] + L, :].astype(BF16)
    a = {u: _dot_nt(qs[u], k_out[u]) for u in units}
    upd_full = {u: _dot_tn(k_end[u], v[u]) for u in units}
    a = {u: jnp.where(tri4[u[0]], a[u], 0.0).astype(BF16) for u in units}
    o_intra = {u: jnp.concatenate(
        [_dot(a[u][h * L:(h + 1) * L], v[u][:, LIN_DV * h:LIN_DV * (h + 1)]) for h in range(LIN_HEADS)], axis=0)
        for u in units}
    upd = {u: jnp.concatenate(
        [upd_full[u][LIN_DK * h:LIN_DK * (h + 1), LIN_DV * h:LIN_DV * (h + 1)] for h in range(LIN_HEADS)], axis=0)
        for u in units}

    s = [st_ref[0], st_ref[1]]
    for u in units:
        d, _ = u
        o = o_intra[u] + _dot(qs[u], s[d].astype(BF16))
        o_refs[d][row0[u]:row0[u] + L, :] = jnp.concatenate([o[h * L:(h + 1) * L] for h in range(LIN_HEADS)], axis=1)
        s[d] = dcol[u] * s[d] + upd[u]
    st_ref[0] = s[0]
    st_ref[1] = s[1]


def _lin_init(j, s0f_ref, s0b_ref, st_ref):
    _, blk, _ = _seq_of(j)
    is_prompt = j < PROMPT_BLKS

    @pl.when(blk == 0)
    def _():
        st_ref[0] = jnp.where(is_prompt, 0.0, s0f_ref[0])
        st_ref[1] = jnp.where(is_prompt, 0.0, s0b_ref[0])


def _gla_kernel(qkf_ref, vf_ref, qkb_ref, vb_ref, s0f_ref, s0b_ref, wa_ref, ba_ref,
                of_ref, ob_ref, sf_ref, sb_ref, st_ref):
    _lin_init(pl.program_id(0), s0f_ref, s0b_ref, st_ref)
    qk_refs = (qkf_ref, qkb_ref)

    def chunk_inputs(d, r0):
        qk = qk_refs[d][r0:r0 + LIN_L, :]
        q = qk[:, 0:256] * (GLA_DK ** -0.5)
        k = qk[:, 256:512]
        ga_hi, ga_lo = _split2(qk[:, 512:640])
        wa_hi, wa_lo = _split2(wa_ref[d])
        x = _dot(ga_hi, wa_hi) + _dot(ga_lo, wa_hi) + _dot(ga_hi, wa_lo) + ba_ref[d]
        return q, k, _log_sigmoid(x) / GLA_GATE_NORM

    _lin_scan(chunk_inputs, (vf_ref, vb_ref), (of_ref, ob_ref), st_ref)
    sf_ref[0] = st_ref[0]
    sb_ref[0] = st_ref[1]


def _hgrn_kernel(qff_ref, vf_ref, qfb_ref, vb_ref, s0f_ref, s0b_ref, lbl_ref,
                 of_ref, ob_ref, sf_ref, sb_ref, st_ref):
    _lin_init(pl.program_id(0), s0f_ref, s0b_ref, st_ref)
    qf_refs = (qff_ref, qfb_ref)

    def lower_bound(d):
        l0 = lbl_ref[d, 0:1, :]
        l1 = lbl_ref[d, 1:2, :]
        m = jnp.maximum(l0, l1)
        e0 = jnp.exp(l0 - m)
        e1 = jnp.exp(l1 - m)
        p0 = e0 / (e0 + e1)
        p1 = e1 / (e0 + e1)
        return (p0 + p1) - p0

    lbs = (lower_bound(0), lower_bound(1))

    def chunk_inputs(d, r0):
        q = qf_refs[d][r0:r0 + LIN_L, 0:256]
        x = qf_refs[d][r0:r0 + LIN_L, 256 * (d + 1):256 * (d + 2)]
        f = lbs[d] + (1.0 - lbs[d]) * _sigmoid(x)
        return q, 1.0 - f, jnp.log(f)

    _lin_scan(chunk_inputs, (vf_ref, vb_ref), (of_ref, ob_ref), st_ref)
    sf_ref[0] = st_ref[0]
    sb_ref[0] = st_ref[1]


def _lin_call(kernel, name, qk, v, s0f, s0b, params, qk_width):
    st_spec = pl.BlockSpec((1, LIN_QK, LIN_DV), lambda j: (_state_idx(j), 0, 0))
    so_spec = pl.BlockSpec((1, LIN_QK, LIN_DV), lambda j: (_seq_of(j)[0], 0, 0))
    so_shape = jax.ShapeDtypeStruct((N_SEQ, LIN_QK, LIN_DV), F32)
    of, ob, sf, sb = pl.pallas_call(
        kernel,
        grid=(N_BLK,),
        in_specs=[_fwd_spec(qk_width), _fwd_spec(LIN_V), _bwd_spec(qk_width), _bwd_spec(LIN_V),
                  st_spec, st_spec] + [_const_spec(p.shape) for p in params],
        out_specs=[_fwd_spec(LIN_V), _bwd_spec(LIN_V), so_spec, so_spec],
        out_shape=[jax.ShapeDtypeStruct((N_TOK, LIN_V), F32)] * 2 + [so_shape, so_shape],
        scratch_shapes=[pltpu.VMEM((2, LIN_QK, LIN_DV), F32)],
        compiler_params=_cparams(),
        name=name,
    )(qk, v, qk, v, s0f.reshape(DEC_BATCH, LIN_QK, LIN_DV), s0b.reshape(DEC_BATCH, LIN_QK, LIN_DV), *params)
    unpack = lambda s: s[:BATCH].reshape(BATCH, LIN_HEADS, LIN_DK, LIN_DV)
    return of, ob, unpack(sf), unpack(sb)


def _outproj0_kernel(x_ref, m_ref, of_ref, ob_ref, z_ref, yb_ref, nw_ref, w_ref, o_ref):
    ya = (of_ref[...] + ob_ref[...]) * _silu(z_ref[...])
    ms = jnp.mean(ya * ya, axis=-1, keepdims=True)
    ya = ya * lax.rsqrt(ms + NORM_EPS) * nw_ref[...]
    y = _dot(ya.astype(BF16), w_ref[0:512, :]) + _dot(yb_ref[...].astype(BF16), w_ref[512:1024, :])
    o_ref[...] = x_ref[...] + m_ref[0, 2:3, :] * y


def _outproj0(x, mrows, of, ob, z, yb, nw, w):
    return pl.pallas_call(
        _outproj0_kernel,
        grid=(N_TILES,),
        in_specs=[_row_spec(D_MODEL), _mod_spec(), _row_spec(512), _row_spec(512), _row_spec(512),
                  _row_spec(512), _const_spec((1, 512)), _const_spec((D_MODEL, D_MODEL))],
        out_specs=_row_spec(D_MODEL),
        out_shape=jax.ShapeDtypeStruct((N_TOK, D_MODEL), F32),
        compiler_params=_cparams(),
        name="outproj0",
    )(x, mrows, of, ob, z, yb, nw, w)


def _head_rms128(o, w):
    parts = []
    for h in range(LIN_HEADS):
        oh = o[:, LIN_DV * h:LIN_DV * (h + 1)]
        ms = jnp.mean(oh * oh, axis=-1, keepdims=True)
        parts.append(oh * lax.rsqrt(ms + NORM_EPS) * w)
    return jnp.concatenate(parts, axis=-1)


def _outproj1_kernel(x_ref, m_ref, gf_ref, gb_ref, gg_ref, hf_ref, hb_ref, hg_ref,
                     gw_ref, hw_ref, w_ref, o_ref):
    yc = _head_rms128(gf_ref[...] + gb_ref[...], gw_ref[...]) * _silu(gg_ref[...])
    yd = _head_rms128(hf_ref[...] + hb_ref[...], hw_ref[...]) * _silu(hg_ref[...])
    y = _dot(yc.astype(BF16), w_ref[0:512, :]) + _dot(yd.astype(BF16), w_ref[512:1024, :])
    o_ref[...] = x_ref[...] + m_ref[0, 2:3, :] * y


def _outproj1(x, mrows, gf, gb, gg, hf, hb, hg, gw, hw, w):
    return pl.pallas_call(
        _outproj1_kernel,
        grid=(N_TILES,),
        in_specs=[_row_spec(D_MODEL), _mod_spec()] + [_row_spec(512)] * 6
                 + [_const_spec((1, LIN_DV)), _const_spec((1, LIN_DV)), _const_spec((D_MODEL, D_MODEL))],
        out_specs=_row_spec(D_MODEL),
        out_shape=jax.ShapeDtypeStruct((N_TOK, D_MODEL), F32),
        compiler_params=_cparams(),
        name="outproj1",
    )(x, mrows, gf, gb, gg, hf, hb, hg, gw, hw, w)


HALO_PER_TILE = TM // SUBLANES


def _ffn_kernel(x_ref, xp_ref, xn_ref, m_ref, nw_ref, wu_ref, cw_ref, cb_ref, wd_ref, o_ref, hs_ref, acc_ref):
    i = pl.program_id(0)
    nw = nw_ref[...]
    sh = m_ref[0, 3:4, :]
    sc = m_ref[0, 4:5, :]
    x = x_ref[...]
    hs_ref[0:TM, :] = _norm_mod(x, nw, sh, sc).astype(BF16)
    halo = jnp.concatenate([xp_ref[...], xn_ref[...]], axis=0)
    hs_ref[TM:TM + 2 * SUBLANES, :] = _norm_mod(halo, nw, sh, sc).astype(BF16)

    seq = jnp.where(i < PROMPT_TILES, SEQ, DEC_SEQ)
    pos = (i * TM + lax.broadcasted_iota(jnp.int32, (TM, 1), 0)) & (seq - 1)
    rows = lax.broadcasted_iota(jnp.int32, (TM, 1), 0)
    first = pos == 0
    lastp = pos == seq - 1

    def dwconv(u_all, c0):
        u = u_all[0:TM]
        prev_row = u_all[TM + SUBLANES - 1:TM + SUBLANES]
        next_row = u_all[TM + SUBLANES:TM + SUBLANES + 1]
        um = jnp.where(rows == 0, prev_row, pltpu.roll(u, 1, axis=0))
        up = jnp.where(rows == TM - 1, next_row, pltpu.roll(u, TM - 1, axis=0))
        um = jnp.where(first, 0.0, um)
        up = jnp.where(lastp, 0.0, up)
        cs = slice(c0, c0 + FFN_CH)
        return cb_ref[:, cs] + cw_ref[0:1, cs] * um + cw_ref[1:2, cs] * u + cw_ref[2:3, cs] * up

    hs = hs_ref[...]
    for c in range(FFN_DIM // FFN_CH):
        a = dwconv(_dot(hs, wu_ref[:, c * FFN_CH:(c + 1) * FFN_CH]), c * FFN_CH)
        b = dwconv(_dot(hs, wu_ref[:, FFN_DIM + c * FFN_CH:FFN_DIM + (c + 1) * FFN_CH]), FFN_DIM + c * FFN_CH)
        act = (_silu(a) * b).astype(BF16)
        part = _dot(act, wd_ref[c * FFN_CH:(c + 1) * FFN_CH, :])
        if c == 0:
            acc_ref[...] = part
        else:
            acc_ref[...] += part
    o_ref[...] = x + m_ref[0, 5:6, :] * acc_ref[...]


def _ffn(x, mrows, nw, wu, cw, cb, wd):
    prev = pl.BlockSpec((SUBLANES, D_MODEL), lambda i: (jnp.maximum(i * HALO_PER_TILE - 1, 0), 0))
    nxt = pl.BlockSpec((SUBLANES, D_MODEL),
                       lambda i: (jnp.minimum((i + 1) * HALO_PER_TILE, N_TOK // SUBLANES - 1), 0))
    single = dict(pipeline_mode=pl.Buffered(1))
    return pl.pallas_call(
        _ffn_kernel,
        grid=(N_TILES,),
        in_specs=[_row_spec(D_MODEL), prev, nxt, _mod_spec(), _const_spec((1, D_MODEL)),
                  pl.BlockSpec((D_MODEL, 2 * FFN_DIM), lambda i: (0, 0), **single),
                  _const_spec((3, 2 * FFN_DIM)), _const_spec((1, 2 * FFN_DIM)),
                  pl.BlockSpec((FFN_DIM, D_MODEL), lambda i: (0, 0), **single)],
        out_specs=_row_spec(D_MODEL),
        out_shape=jax.ShapeDtypeStruct((N_TOK, D_MODEL), F32),
        scratch_shapes=[pltpu.VMEM((TM + 2 * SUBLANES, D_MODEL), BF16), pltpu.VMEM((TM, D_MODEL), F32)],
        compiler_params=_cparams(),
        name="ffn",
    )(x, x, x, mrows, nw, wu, cw, cb, wd)


def _pad_lanes(v, width=128):
    v = v.reshape(1, -1)
    return jnp.pad(v, ((0, 0), (0, width - v.shape[1])))


def kernel(x_prompt, x_sample, cache_na_k_l0, cache_na_v_l0, state_ssd_fwd_l0, state_ssd_bwd_l0,
           state_gla_fwd_l1, state_gla_bwd_l1, state_hgrn_fwd_l1, state_hgrn_bwd_l1, c,
           c_ctx, w_ada, b_ada, norm_w, ffn_w_up, ffn_conv_w, ffn_conv_b, ffn_w_down,
           w_in_l0, w_out_l0, ssd_conv_w_l0, ssd_conv_b_l0, ssd_dt_bias_l0, ssd_a_log_l0, ssd_d_l0,
           ssd_norm_w_l0, na_q_norm_l0, na_k_norm_l0, na_rpb_l0,
           w_in_l1, w_out_l1, gla_wa2_l1, gla_ba2_l1, gla_norm_w_l1, hgrn_lb_logits, hgrn_norm_w_l1):
    x = jnp.concatenate([x_prompt.reshape(N_PROMPT, D_MODEL), x_sample.reshape(N_SAMPLE, D_MODEL)], axis=0)

    cvec8 = jnp.zeros((SUBLANES, D_MODEL), F32).at[0:DEC_BATCH].set(c).at[CTX_MOD_ROW].set(c_ctx)
    mods = _mods(cvec8, w_ada, b_ada)
    mods = mods.reshape(2, SUBLANES, 6, D_MODEL)
    mods = jnp.pad(mods, ((0, 0), (0, 0), (0, SUBLANES - 6), (0, 0)))

    zpad = lambda n: jnp.zeros((D_MODEL, n), F32)
    w0 = jnp.concatenate([w_in_l0[:, :1296], zpad(112), w_in_l0[:, 1296:]], axis=1).astype(BF16)
    w1 = jnp.concatenate([w_in_l1[:, 0:512], w_in_l1[:, 1536:1568], zpad(96),
                          w_in_l1[:, 512:1536], w_in_l1[:, 1568:3360]], axis=1).astype(BF16)
    seg = jnp.kron(jnp.eye(NA_HEADS, dtype=F32), jnp.full((NA_HEAD_DIM, NA_HEAD_DIM), 1.0 / NA_HEAD_DIM, F32)).astype(BF16)

    m0 = mods[0]
    z, xbc, dt, qn, kn, knb, v, vb = _inproj0(
        x, m0, norm_w[0, 0].reshape(1, D_MODEL), w0, seg,
        jnp.tile(na_q_norm_l0, NA_HEADS).reshape(1, NA_WIDTH), jnp.tile(na_k_norm_l0, NA_HEADS).reshape(1, NA_WIDTH))
    of, ob, ssd_f, ssd_b = _ssd(
        xbc, dt, state_ssd_fwd_l0, state_ssd_bwd_l0, ssd_conv_w_l0, ssd_conv_b_l0.reshape(1, SSD_XBC),
        _pad_lanes(ssd_dt_bias_l0), _pad_lanes(ssd_a_log_l0),
        jnp.repeat(ssd_d_l0, SSD_HEAD_DIM).reshape(1, SSD_WIDTH))
    yb, na_k, na_v = _na_ctx(qn, kn, v)
    yb = _na_lat(qn, knb, vb, cache_na_k_l0, cache_na_v_l0, _na_bias_table(na_rpb_l0), yb)
    x = _outproj0(x, m0, of, ob, z, yb, ssd_norm_w_l0.reshape(1, SSD_WIDTH), w_out_l0.astype(BF16))
    x = _ffn(x, m0, norm_w[0, 1].reshape(1, D_MODEL), ffn_w_up[0].astype(BF16), ffn_conv_w[0],
             ffn_conv_b[0].reshape(1, 2 * FFN_DIM), ffn_w_down[0].astype(BF16))

    m1 = mods[1]
    gqk, gv, gg, hqf, hi, hg = _inproj1(x, m1, norm_w[1, 0].reshape(1, D_MODEL), w1)
    wa_pad = jnp.zeros((2, 128, LIN_QK), F32)
    wa_pad = wa_pad.at[0, 0:GLA_RANK].set(gla_wa2_l1[0]).at[1, GLA_RANK:2 * GLA_RANK].set(gla_wa2_l1[1])
    gf, gb, gla_f, gla_b = _lin_call(
        _gla_kernel, "gla", gqk, gv, state_gla_fwd_l1, state_gla_bwd_l1,
        (wa_pad, gla_ba2_l1.reshape(2, 1, LIN_QK)), 640)
    hf, hb, hgrn_f, hgrn_b = _lin_call(
        _hgrn_kernel, "hgrn", hqf, hi, state_hgrn_fwd_l1, state_hgrn_bwd_l1, (hgrn_lb_logits,), 768)
    x = _outproj1(x, m1, gf, gb, gg, hf, hb, hg, gla_norm_w_l1.reshape(1, LIN_DV),
                  hgrn_norm_w_l1.reshape(1, LIN_DV), w_out_l1.astype(BF16))
    x = _ffn(x, m1, norm_w[1, 1].reshape(1, D_MODEL), ffn_w_up[1].astype(BF16), ffn_conv_w[1],
             ffn_conv_b[1].reshape(1, 2 * FFN_DIM), ffn_w_down[1].astype(BF16))

    y_p = x[:N_PROMPT].reshape(BATCH, SEQ, D_MODEL)
    y_s = x[N_PROMPT:].reshape(DEC_BATCH, DEC_SEQ, D_MODEL)
    return (y_p, y_s, na_k, na_v, ssd_f, ssd_b, gla_f, gla_b, hgrn_f, hgrn_b)
```

```python
import functools

import jax
import jax.numpy as jnp
from jax import lax
from jax.experimental import pallas as pl
from jax.experimental.pallas import tpu as pltpu

F32 = jnp.float32
BF16 = jnp.bfloat16
HIGHEST = lax.Precision.HIGHEST

D_MODEL = 1024
BATCH = 32
SEQ = 256
DEC_BATCH = 4
DEC_SEQ = 4096
PAST_LEN = 256
GRID_W = 64
NORM_EPS = 1e-6
N_PROMPT = BATCH * SEQ
N_SAMPLE = DEC_BATCH * DEC_SEQ
N_TOK = N_PROMPT + N_SAMPLE

SSD_HEADS = 8
SSD_HEAD_DIM = 64
SSD_STATE = 64
SSD_WIDTH = 512
SSD_BC = 128
SSD_XBC = 768
NA_HEADS = 8
NA_HEAD_DIM = 64
NA_WIDTH = 512
NA_WIN_ROWS = 8
NA_WIN_COLS = 16
NA_SCALE = NA_HEAD_DIM ** -0.5
GLA_HEADS = 4
GLA_DK = 64
GLA_DV = 128
GLA_RANK = 16
GLA_GATE_NORM = 16.0
HGRN_HEADS = 4
FFN_DIM = 2816

V7X_VMEM_BYTES = 64 * 1024 * 1024
VMEM_LIMIT = 56 * 1024 * 1024
SUBLANES = 8

TM = 512
N_TILES = N_TOK // TM
PROMPT_TILES = N_PROMPT // TM
TILES_PER_SAMPLE = DEC_SEQ // TM
CTX_MOD_ROW = DEC_BATCH

RB = 256
N_BLK = N_TOK // RB
PROMPT_BLKS = N_PROMPT // RB
BLKS_PER_SAMPLE = DEC_SEQ // RB
N_SEQ = BATCH + DEC_BATCH
SSD_L = 128
LIN_L = 64
NA_R = 8
FFN_CH = 256
W0_COLS = 2944
W1_COLS = 3456


def _cparams(n_axes=1):
    return pltpu.CompilerParams(dimension_semantics=("arbitrary",) * n_axes,
                                vmem_limit_bytes=VMEM_LIMIT)


def _const_spec(shape):
    nd = len(shape)
    return pl.BlockSpec(shape, lambda *_: (0,) * nd)


def _sigmoid(x):
    return 1.0 / (1.0 + jnp.exp(-x))


def _silu(x):
    return x * _sigmoid(x)


def _softplus(x):
    return jnp.maximum(x, 0.0) + jnp.log1p(jnp.exp(-jnp.abs(x)))


def _mod_row(i):
    return jnp.where(i < PROMPT_TILES, CTX_MOD_ROW, (i - PROMPT_TILES) // TILES_PER_SAMPLE)


def _dot(a, b, **kw):
    return jnp.dot(a, b, preferred_element_type=F32, **kw)


def _dot_nt(a, b):
    return lax.dot_general(a, b, (((1,), (1,)), ((), ())), preferred_element_type=F32)


def _dot_tn(a, b):
    return lax.dot_general(a, b, (((0,), (0,)), ((), ())), preferred_element_type=F32)


MODS_NB = 1536


def _mods_kernel(c_ref, w_ref, b_ref, o_ref):
    s = _silu(c_ref[...])
    o_ref[0] = _dot(s, w_ref[0], precision=HIGHEST) + b_ref[0]


def _mods(cvec8, w_ada, b_ada):
    depth = w_ada.shape[0]
    nb = 6 * D_MODEL // MODS_NB
    return pl.pallas_call(
        _mods_kernel,
        grid=(depth, nb),
        in_specs=[
            _const_spec((SUBLANES, D_MODEL)),
            pl.BlockSpec((1, D_MODEL, MODS_NB), lambda l, j: (l, 0, j)),
            pl.BlockSpec((1, 1, MODS_NB), lambda l, j: (l, 0, j)),
        ],
        out_specs=pl.BlockSpec((1, SUBLANES, MODS_NB), lambda l, j: (l, 0, j)),
        out_shape=jax.ShapeDtypeStruct((depth, SUBLANES, 6 * D_MODEL), F32),
        compiler_params=_cparams(2),
        name="mods",
    )(cvec8, w_ada, b_ada.reshape(depth, 1, 6 * D_MODEL))


def _norm_mod(x, nw, sh, sc):
    ms = jnp.mean(x * x, axis=-1, keepdims=True)
    y = x * lax.rsqrt(ms + NORM_EPS) * nw
    return y * (1.0 + sc) + sh


def _head_rms(x, seg, w):
    ms = _dot((x * x).astype(BF16), seg)
    return x * lax.rsqrt(ms + NORM_EPS) * w


def _tile_of(xp_ref, xs_ref):
    return jnp.where(pl.program_id(0) < PROMPT_TILES, xp_ref[...], xs_ref[...])


HALO_ROWS = 2 * SUBLANES
TM_EXT = TM + HALO_ROWS


def _halo_keep(i):
    seq = jnp.where(i < PROMPT_TILES, SEQ, DEC_SEQ)
    keep_prev = jnp.where(((i * TM) & (seq - 1)) == 0, 0.0, 1.0)
    keep_next = jnp.where(((i * TM + TM) & (seq - 1)) == 0, 0.0, 1.0)
    hrow = lax.broadcasted_iota(jnp.int32, (HALO_ROWS, 1), 0)
    return jnp.where(hrow < SUBLANES, keep_next, keep_prev)


def _dwconv_tile(u_all, w, bias, i):
    um = pltpu.roll(u_all, 1, axis=0)[0:TM] * w[0:1]
    up = pltpu.roll(u_all, TM_EXT - 1, axis=0)[0:TM] * w[2:3]
    out = bias + um + u_all[0:TM] * w[1:2] + up
    edge_on = jnp.where(i < PROMPT_TILES, 1.0, 0.0)
    erow = lax.broadcasted_iota(jnp.int32, (HALO_ROWS, 1), 0)
    at_first = jnp.where(erow == SUBLANES, edge_on, 0.0)
    at_last = jnp.where(erow == SUBLANES - 1, edge_on, 0.0)
    pieces, r = [], 0
    for edge in range(SEQ, TM, SEQ):
        sl = slice(edge - SUBLANES, edge + SUBLANES)
        pieces += [out[r:edge - SUBLANES], out[sl] - at_first * um[sl] - at_last * up[sl]]
        r = edge + SUBLANES
    return jnp.concatenate(pieces + [out[r:TM]], axis=0)


def _inproj0_kernel(xp_ref, xs_ref, xn_ref, xv_ref, m_ref, nw_ref, w_ref, seg_ref, qw_ref, kw_ref, cw_ref, cb_ref,
                    z_ref, xc_ref, dt_ref, qn_ref, knb_ref, vb_ref, kn_ref, v_ref, hs_ref):
    i = pl.program_id(0)
    nw, sh, sc = nw_ref[...], m_ref[0, 0:1, :], m_ref[0, 1:2, :]
    hs_ref[0:TM, :] = _norm_mod(_tile_of(xp_ref, xs_ref), nw, sh, sc).astype(BF16)
    halo = _norm_mod(jnp.concatenate([xn_ref[...], xv_ref[...]], axis=0), nw, sh, sc)
    hs_ref[TM:TM_EXT, :] = (halo * _halo_keep(i)).astype(BF16)
    z_ref[...] = _dot(hs_ref[0:TM, :], w_ref[:, 0:512])
    xc_ref[...] = _silu(_dwconv_tile(_dot(hs_ref[...], w_ref[:, 512:1280]), cw_ref[...], cb_ref[...], i))
    dt_ref[...] = _dot(hs_ref[0:TM, :], w_ref[:, 1280:1408])
    q = _dot(hs_ref[0:TM, :], w_ref[:, 1408:1920])
    k = _dot(hs_ref[0:TM, :], w_ref[:, 1920:2432])
    v = _dot(hs_ref[0:TM, :], w_ref[:, 2432:2944])
    vb_ref[...] = v.astype(BF16)
    ms_q = _dot((q * q).astype(BF16), seg_ref[...])
    ms_k = _dot((k * k).astype(BF16), seg_ref[...])
    qn_ref[...] = (q * lax.rsqrt(ms_q + NORM_EPS) * qw_ref[...] * NA_SCALE).astype(BF16)
    kn = k * lax.rsqrt(ms_k + NORM_EPS) * kw_ref[...]
    knb_ref[...] = kn.astype(BF16)

    @pl.when(pl.program_id(0) < PROMPT_TILES)
    def _():
        kn_ref[...] = kn
        v_ref[...] = v


def _row_spec(width):
    return pl.BlockSpec((TM, width), lambda i: (i, 0))


def _mod_spec():
    return pl.BlockSpec((1, SUBLANES, D_MODEL), lambda i: (_mod_row(i), 0, 0))


def _prompt_tile_spec(width=D_MODEL):
    return pl.BlockSpec((TM, width), lambda i: (jnp.minimum(i, PROMPT_TILES - 1), 0))


def _sample_tile_spec(width=D_MODEL):
    return pl.BlockSpec((TM, width), lambda i: (jnp.maximum(i - PROMPT_TILES, 0), 0))


def _inproj0(xp, xs, mrows, nw, w0, seg, qw, kw, conv_w, conv_b):
    widths = (512, 768, 128, 512, 512, 512)
    dtypes = (F32, F32, F32, BF16, BF16, BF16)
    halo_per_tile = TM // SUBLANES
    last_halo = N_SAMPLE // SUBLANES - 1
    nxt = pl.BlockSpec((SUBLANES, D_MODEL),
                       lambda i: (jnp.clip((i - PROMPT_TILES + 1) * halo_per_tile, 0, last_halo), 0))
    prv = pl.BlockSpec((SUBLANES, D_MODEL),
                       lambda i: (jnp.clip((i - PROMPT_TILES) * halo_per_tile - 1, 0, last_halo), 0))
    return pl.pallas_call(
        _inproj0_kernel,
        grid=(N_TILES,),
        in_specs=[
            _prompt_tile_spec(), _sample_tile_spec(), nxt, prv, _mod_spec(), _const_spec((1, D_MODEL)),
            _const_spec((D_MODEL, W0_COLS)), _const_spec((512, 512)),
            _const_spec((1, 512)), _const_spec((1, 512)), _const_spec((3, SSD_XBC)), _const_spec((1, SSD_XBC)),
        ],
        out_specs=[_row_spec(w) for w in widths] + [_prompt_tile_spec(NA_WIDTH)] * 2,
        out_shape=[jax.ShapeDtypeStruct((N_TOK, w), d) for w, d in zip(widths, dtypes)]
                  + [jax.ShapeDtypeStruct((N_PROMPT, NA_WIDTH), F32)] * 2,
        scratch_shapes=[pltpu.VMEM((TM_EXT, D_MODEL), BF16)],
        compiler_params=_cparams(),
        name="inproj0",
    )(xp, xs, xs, xs, mrows, nw, w0, seg, qw, kw, conv_w, conv_b)


def _inproj1_kernel(x_ref, m_ref, nw_ref, w_ref, gqk_ref, gv_ref, gg_ref, hqf_ref, hi_ref, hg_ref, hs_ref):
    hs_ref[...] = _norm_mod(x_ref[...], nw_ref[...], m_ref[0, 0:1, :], m_ref[0, 1:2, :]).astype(BF16)
    gqk_ref[...] = _dot(hs_ref[...], w_ref[:, 0:640])
    gv_ref[...] = _dot(hs_ref[...], w_ref[:, 640:1152])
    gg_ref[...] = _dot(hs_ref[...], w_ref[:, 1152:1664])
    hqf_ref[...] = _dot(hs_ref[...], w_ref[:, 1664:2432])
    hi_ref[...] = _dot(hs_ref[...], w_ref[:, 2432:2944])
    hg_ref[...] = _dot(hs_ref[...], w_ref[:, 2944:3456])


def _inproj1(x, mrows, nw, w1):
    widths = (640, 512, 512, 768, 512, 512)
    return pl.pallas_call(
        _inproj1_kernel,
        grid=(N_TILES,),
        in_specs=[_row_spec(D_MODEL), _mod_spec(), _const_spec((1, D_MODEL)),
                  _const_spec((D_MODEL, W1_COLS))],
        out_specs=[_row_spec(w) for w in widths],
        out_shape=[jax.ShapeDtypeStruct((N_TOK, w), F32) for w in widths],
        scratch_shapes=[pltpu.VMEM((TM, D_MODEL), BF16)],
        compiler_params=_cparams(),
        name="inproj1",
    )(x, mrows, nw, w1)


def _seq_of(j):
    jj = j - PROMPT_BLKS
    is_p = j < PROMPT_BLKS
    seq = jnp.where(is_p, j, BATCH + jj // BLKS_PER_SAMPLE)
    blk = jnp.where(is_p, 0, jj % BLKS_PER_SAMPLE)
    nblk = jnp.where(is_p, 1, BLKS_PER_SAMPLE)
    return seq, blk, nblk


def _bwd_blk(j):
    _, blk, nblk = _seq_of(j)
    return j - blk + (nblk - 1 - blk)


def _state_idx(j):
    seq, _, _ = _seq_of(j)
    return jnp.maximum(seq - BATCH, 0)


def _fwd_spec(width):
    return pl.BlockSpec((RB, width), lambda j: (j, 0))


def _bwd_spec(width):
    return pl.BlockSpec((RB, width), lambda j: (_bwd_blk(j), 0))


def _tri(n, upper):
    r = lax.broadcasted_iota(jnp.int32, (n, n), 0)
    c = lax.broadcasted_iota(jnp.int32, (n, n), 1)
    return (c >= r) if upper else (c <= r)


def _diag_blocks(x, rows):
    lh = lax.broadcasted_iota(jnp.int32, (1, 4 * SSD_HEAD_DIM), 1) >> 6
    out = x[3 * rows:4 * rows]
    for h in (2, 1, 0):
        out = jnp.where(lh == h, x[h * rows:(h + 1) * rows], out)
    return out


def _ssd_kernel(xf_ref, dtf_ref, xb_ref, dtb_ref, s0f_ref, s0b_ref, dtbias_ref, alog_ref, dsk_ref, ex_ref,
                of_ref, ob_ref, sf_ref, sb_ref, st_ref, v_ref):
    j = pl.program_id(0)
    _, blk, _ = _seq_of(j)
    is_prompt = j < PROMPT_BLKS

    @pl.when(blk == 0)
    def _():
        st_ref[0] = jnp.where(is_prompt, 0.0, s0f_ref[0])
        st_ref[1] = jnp.where(is_prompt, 0.0, s0b_ref[0])

    a_neg = -jnp.exp(alog_ref[...])
    L = SSD_L
    nch = RB // L
    xc_refs = (xf_ref, xb_ref)
    dt_refs = (dtf_ref, dtb_ref)
    o_refs = (of_ref, ob_ref)
    gs = []
    for d in range(2):
        dt = _softplus(dt_refs[d][...] + dtbias_ref[...])
        dt_hi, dt_lo = _split2(dt)
        dtx = _dot(dt_hi, ex_ref[d]) + _dot(dt_lo, ex_ref[d])
        v_ref[d] = (xc_refs[d][:, 0:SSD_WIDTH] * dtx).astype(BF16)
        gs.append(dt * a_neg)

    units = [(d, c) for c in range(nch) for d in range(2)]
    row0 = {(d, c): ((nch - 1 - c) * L if d else c * L) for d, c in units}
    tri = [_tri(L, False), _tri(L, True)]
    tri_bf = [jnp.where(t, 1.0, 0.0).astype(BF16) for t in tri]
    lane = lax.broadcasted_iota(jnp.int32, (1, SSD_BC), 1)
    gmask = [lane < SSD_STATE, lane >= SSD_STATE]
    eye_bf = jnp.where(tri[0] & tri[1], 1.0, 0.0).astype(BF16)
    zero_blk = jnp.zeros((SSD_STATE, 4 * SSD_HEAD_DIM), F32)

    cum = {u: _dot_exact_lhs(tri_bf[u[0]], gs[u[0]][row0[u]:row0[u] + L]) for u in units}
    ct8, e_in_x, e_last_x, e_end_t, b_t, cb_g, cm_bf = {}, {}, {}, {}, {}, {}, {}
    for u in units:
        d, _ = u
        r0 = row0[u]
        cu = cum[u]
        ct8[u] = cu.T[8 * d:8 * d + 8]
        hi, lo = _split2(jnp.exp(cu))
        e_in_x[u] = _dot(hi, ex_ref[d]) + _dot(lo, ex_ref[d])
        parts = _split3(cu[0:SUBLANES] if d else cu[L - SUBLANES:L])
        last_x = _dot(parts[0], ex_ref[d]) + _dot(parts[1], ex_ref[d]) + _dot(parts[2], ex_ref[d])
        e_last_x[u] = jnp.exp(last_x[0:1] if d else last_x[SUBLANES - 1:SUBLANES])
        last_col = ct8[u][:, 0:1] if d else ct8[u][:, L - 1:L]
        e_end_t[u] = jnp.exp(last_col - ct8[u])
        bm = xc_refs[d][r0:r0 + L, 512:640].astype(BF16)
        cm = xc_refs[d][r0:r0 + L, 640:768]
        b_t[u] = _dot_nt(eye_bf, bm)
        cb_g[u] = [_dot_nt(jnp.where(gmask[g], cm, 0.0).astype(BF16), bm) for g in range(2)]
        cm_bf[u] = cm.astype(BF16)

    w, k_t = {}, {}
    for u in units:
        d, _ = u
        for g in range(2):
            ws, ks = [], []
            for hh in range(4):
                h = 4 * g + hh
                pc = cum[u][:, 8 * d + h:8 * d + h + 1]
                pr = ct8[u][h:h + 1, :]
                dec = jnp.exp(jnp.where(tri[d], pc - pr, -jnp.inf))
                ws.append((cb_g[u][g] * dec).astype(BF16))
                ks.append((b_t[u][SSD_STATE * g:SSD_STATE * (g + 1)] * e_end_t[u][h:h + 1, :]).astype(BF16))
            w[(u, g)] = jnp.concatenate(ws, axis=0)
            k_t[(u, g)] = jnp.concatenate(ks, axis=0)

    o_intra, upd = {}, {}
    for u in units:
        d, _ = u
        r0 = row0[u]
        oi, up = [], []
        for g in range(2):
            vg = v_ref[d, r0:r0 + L, 256 * g:256 * (g + 1)]
            oi.append(_diag_blocks(_dot(w[(u, g)], vg), L))
            up.append(_diag_blocks(_dot(k_t[(u, g)], vg), SSD_STATE))
        o_intra[u] = jnp.concatenate(oi, axis=1)
        upd[u] = jnp.concatenate([jnp.concatenate([up[0], zero_blk], axis=1),
                                  jnp.concatenate([zero_blk, up[1]], axis=1)], axis=0)

    s = [st_ref[0], st_ref[1]]
    for u in units:
        d, _ = u
        r0 = row0[u]
        o = o_intra[u] + e_in_x[u] * _dot(cm_bf[u], s[d].astype(BF16))
        if d == 0:
            o = o + dsk_ref[...] * xf_ref[r0:r0 + L, 0:SSD_WIDTH]
        o_refs[d][r0:r0 + L, :] = o
        s[d] = e_last_x[u] * s[d] + upd[u]
    st_ref[0] = s[0]
    st_ref[1] = s[1]
    sf_ref[0] = s[0]
    sb_ref[0] = s[1]


def _ssd_pack_state(s):
    b = s.shape[0]
    g = s.reshape(b, 2, 4, SSD_STATE, SSD_HEAD_DIM).transpose(0, 1, 3, 2, 4).reshape(b, 2, SSD_STATE, 256)
    z = jnp.zeros((b, SSD_STATE, 256), F32)
    return jnp.concatenate([jnp.concatenate([g[:, 0], z], axis=2), jnp.concatenate([z, g[:, 1]], axis=2)], axis=1)


def _ssd_unpack_state(s):
    b = s.shape[0]
    g = jnp.stack([s[:, 0:SSD_STATE, 0:256], s[:, SSD_STATE:, 256:512]], axis=1)
    g = g.reshape(b, 2, SSD_STATE, 4, SSD_HEAD_DIM).transpose(0, 1, 3, 2, 4)
    return g.reshape(b, SSD_HEADS, SSD_STATE, SSD_HEAD_DIM)


def _ssd(xc, dt, s0f, s0b, dtbias, alog, dsk):
    st_shape = (1, 2 * SSD_STATE, SSD_WIDTH)
    st_spec = pl.BlockSpec(st_shape, lambda j: (_state_idx(j), 0, 0))
    so_spec = pl.BlockSpec(st_shape, lambda j: (_seq_of(j)[0], 0, 0))
    so_shape = jax.ShapeDtypeStruct((N_SEQ, 2 * SSD_STATE, SSD_WIDTH), F32)
    col = jnp.arange(128)[:, None]
    lane_head = jnp.arange(SSD_WIDTH)[None, :] // SSD_HEAD_DIM
    expand = jnp.stack([col == lane_head, col == SSD_HEADS + lane_head]).astype(BF16)
    of, ob, sf, sb = pl.pallas_call(
        _ssd_kernel,
        grid=(N_BLK,),
        in_specs=[
            _fwd_spec(SSD_XBC), _fwd_spec(128), _bwd_spec(SSD_XBC), _bwd_spec(128), st_spec, st_spec,
            _const_spec((1, 128)), _const_spec((1, 128)), _const_spec((1, SSD_WIDTH)),
            _const_spec((2, 128, SSD_WIDTH)),
        ],
        out_specs=[_fwd_spec(SSD_WIDTH), _bwd_spec(SSD_WIDTH), so_spec, so_spec],
        out_shape=[jax.ShapeDtypeStruct((N_TOK, SSD_WIDTH), F32)] * 2 + [so_shape, so_shape],
        scratch_shapes=[pltpu.VMEM((2, 2 * SSD_STATE, SSD_WIDTH), F32),
                        pltpu.VMEM((2, RB, SSD_WIDTH), BF16)],
        compiler_params=_cparams(),
        name="ssd",
    )(xc, dt, xc, dt, _ssd_pack_state(s0f), _ssd_pack_state(s0b), dtbias, alog, dsk, expand)
    return of, ob, _ssd_unpack_state(sf[:BATCH]), _ssd_unpack_state(sb[:BATCH])


NA_PAIRS = NA_HEADS // 2


def _stack_pair(qt):
    lower = lax.broadcasted_iota(jnp.int32, (1, 2 * NA_HEAD_DIM), 1) < NA_HEAD_DIM
    zero = jnp.zeros_like(qt)
    return jnp.concatenate([jnp.where(lower, qt, zero), jnp.where(lower, zero, qt)], axis=0)


def _unstack_pair(x, n):
    lower = lax.broadcasted_iota(jnp.int32, (1, 2 * NA_HEAD_DIM), 1) < NA_HEAD_DIM
    return jnp.where(lower, x[0:n], x[n:2 * n])


def _na_ctx_kernel(q_ref, k_ref, v_ref, y_ref, ko_ref, vo_ref):
    tiles = [slice(128 * i, 128 * (i + 1)) for i in range(NA_PAIRS)]
    s = [_dot_nt(_stack_pair(q_ref[:, ts]), k_ref[:, ts].astype(BF16)) for ts in tiles]
    p, l = [], []
    for i in range(NA_PAIRS):
        e = jnp.exp(s[i] - jnp.max(s[i], axis=-1, keepdims=True))
        l.append(jnp.sum(e, axis=-1, keepdims=True))
        p.append(e.astype(BF16))
    o = [_dot(p[i], v_ref[:, tiles[i]].astype(BF16)) for i in range(NA_PAIRS)]
    for i in range(NA_PAIRS):
        y = _unstack_pair(o[i], SEQ) / _unstack_pair(jnp.broadcast_to(l[i], o[i].shape), SEQ)
        y_ref[:, tiles[i]] = y.astype(BF16)
    for h in range(NA_HEADS):
        sl = slice(NA_HEAD_DIM * h, NA_HEAD_DIM * (h + 1))
        ko_ref[0, h] = k_ref[:, sl]
        vo_ref[0, h] = v_ref[:, sl]


def _na_ctx(qn, kn, v):
    blk = lambda w: pl.BlockSpec((SEQ, w), lambda b: (b, 0))
    hm = pl.BlockSpec((1, NA_HEADS, SEQ, NA_HEAD_DIM), lambda b: (b, 0, 0, 0))
    hm_shape = jax.ShapeDtypeStruct((BATCH, NA_HEADS, SEQ, NA_HEAD_DIM), F32)
    return pl.pallas_call(
        _na_ctx_kernel,
        grid=(BATCH,),
        in_specs=[blk(NA_WIDTH)] * 3,
        out_specs=[blk(NA_WIDTH), hm, hm],
        out_shape=[jax.ShapeDtypeStruct((N_TOK, NA_WIDTH), BF16), hm_shape, hm_shape],
        compiler_params=_cparams(),
        name="na_ctx",
    )(qn, kn, v)


GRID_ROWS = DEC_SEQ // GRID_W
NA_LOC = NA_WIN_ROWS * GRID_W
NA_MASKED = -1e30


def _na_lat_kernel(q_ref, k_ref, v_ref, kc_ref, vc_ref, bt_ref, yin_ref, y_ref):
    del yin_ref
    rb = pl.program_id(1)

    def row(r, carry):
        rr = rb * NA_R + r
        rs = jnp.clip(rr - NA_WIN_ROWS // 2, 0, GRID_ROWS - NA_WIN_ROWS)
        dr0 = rs - rr + (NA_WIN_ROWS - 1)
        q0 = pl.multiple_of(r * GRID_W, GRID_W)
        k0 = pl.multiple_of(rs * GRID_W, GRID_W)
        tiles = [slice(128 * i, 128 * (i + 1)) for i in range(NA_PAIRS)]
        qq = [_stack_pair(q_ref[pl.ds(q0, GRID_W), ts]) for ts in tiles]
        s_loc = [_dot_nt(qq[i], k_ref[pl.ds(k0, NA_LOC), tiles[i]])
                 + bt_ref[dr0, 2 * i:2 * i + 2].reshape(2 * GRID_W, NA_LOC) for i in range(NA_PAIRS)]
        s_ctx = [_dot_nt(qq[i], kc_ref[0, :, tiles[i]]) for i in range(NA_PAIRS)]
        p_loc, p_ctx, l = [], [], []
        for i in range(NA_PAIRS):
            m = jnp.maximum(jnp.max(s_loc[i], axis=-1, keepdims=True), jnp.max(s_ctx[i], axis=-1, keepdims=True))
            e_loc = jnp.exp(s_loc[i] - m)
            e_ctx = jnp.exp(s_ctx[i] - m)
            l.append(jnp.sum(e_loc, axis=-1, keepdims=True) + jnp.sum(e_ctx, axis=-1, keepdims=True))
            p_loc.append(e_loc.astype(BF16))
            p_ctx.append(e_ctx.astype(BF16))
        o = [_dot(p_loc[i], v_ref[pl.ds(k0, NA_LOC), tiles[i]]) + _dot(p_ctx[i], vc_ref[0, :, tiles[i]])
             for i in range(NA_PAIRS)]
        for i in range(NA_PAIRS):
            y = _unstack_pair(o[i], GRID_W) / _unstack_pair(jnp.broadcast_to(l[i], o[i].shape), GRID_W)
            y_ref[pl.ds(q0, GRID_W), tiles[i]] = y.astype(BF16)
        return carry

    lax.fori_loop(0, NA_R, row, 0)


def _na_lat(qn, knb, vb, kc, vc, btab, y_in):
    rows_per_step = NA_R * GRID_W
    steps = GRID_ROWS // NA_R
    off_q = N_PROMPT // rows_per_step
    off_s = N_PROMPT // DEC_SEQ
    qspec = pl.BlockSpec((rows_per_step, NA_WIDTH), lambda b, r: (off_q + b * steps + r, 0))
    kvspec = pl.BlockSpec((DEC_SEQ, NA_WIDTH), lambda b, r: (off_s + b, 0))
    cspec = pl.BlockSpec((1, PAST_LEN, NA_WIDTH), lambda b, r: (b, 0, 0))
    token_major = lambda a: a.transpose(0, 2, 1, 3).reshape(DEC_BATCH, PAST_LEN, NA_WIDTH).astype(BF16)
    kc, vc = token_major(kc), token_major(vc)
    return pl.pallas_call(
        _na_lat_kernel,
        grid=(DEC_BATCH, steps),
        in_specs=[qspec, kvspec, kvspec, cspec, cspec,
                  _const_spec((NA_WIN_ROWS, NA_HEADS, GRID_W, NA_LOC)),
                  pl.BlockSpec(memory_space=pl.ANY)],
        out_specs=qspec,
        out_shape=jax.ShapeDtypeStruct((N_TOK, NA_WIDTH), BF16),
        input_output_aliases={6: 0},
        compiler_params=_cparams(2),
        name="na_lat",
    )(qn, knb, vb, kc, vc, btab, y_in)


def _na_bias_table(rpb):
    col = jnp.arange(GRID_W)
    col_start = jnp.clip(col - NA_WIN_COLS // 2, 0, GRID_W - NA_WIN_COLS)
    ok = (col[None, :] >= col_start[:, None]) & (col[None, :] < col_start[:, None] + NA_WIN_COLS)
    d_col = jnp.clip(col[None, :] - col[:, None], -(NA_WIN_COLS - 1), NA_WIN_COLS - 1) + (NA_WIN_COLS - 1)
    onehot = (d_col[:, :, None] == jnp.arange(2 * NA_WIN_COLS - 1)).astype(F32)
    t = jnp.einsum('hrd,cxd->hcrx', rpb, onehot, precision=HIGHEST)
    t = jnp.where(ok[None, :, None, :], t, NA_MASKED)
    b = jnp.stack([t[:, :, a:a + NA_WIN_ROWS, :] for a in range(NA_WIN_ROWS)])
    return b.reshape(NA_WIN_ROWS, NA_HEADS, GRID_W, NA_LOC)


LIN_HEADS = 4
LIN_DK = 64
LIN_DV = 128
LIN_QK = LIN_HEADS * LIN_DK
LIN_V = LIN_HEADS * LIN_DV


def _split2(x):
    hi = x.astype(BF16)
    lo = (x - hi.astype(F32)).astype(BF16)
    return hi, lo


def _split3(x):
    hi = x.astype(BF16)
    r = x - hi.astype(F32)
    mid = r.astype(BF16)
    lo = (r - mid.astype(F32)).astype(BF16)
    return hi, mid, lo


def _dot_exact_lhs(a_bf, x):
    hi, mid, lo = _split3(x)
    return _dot(a_bf, hi) + _dot(a_bf, mid) + _dot(a_bf, lo)


def _log_sigmoid(x):
    return jnp.minimum(x, 0.0) - jnp.log1p(jnp.exp(-jnp.abs(x)))


def _lin_scan(chunk_inputs, v_refs, o_refs, st_ref):
    L = LIN_L
    nch = RB // L
    lane = lax.broadcasted_iota(jnp.int32, (1, LIN_QK), 1)
    head_mask = [(lane >> 6) == h for h in range(LIN_HEADS)]
    r4 = lax.broadcasted_iota(jnp.int32, (LIN_HEADS * L, L), 0) & (L - 1)
    c4 = lax.broadcasted_iota(jnp.int32, (LIN_HEADS * L, L), 1)
    er = lax.broadcasted_iota(jnp.int32, (LIN_QK, LIN_QK), 0)
    ec = lax.broadcasted_iota(jnp.int32, (LIN_QK, LIN_QK), 1)
    eye = er == ec

    units = [(d, c) for c in range(nch) for d in range(2)]
    row0 = {(d, c): ((nch - 1 - c) * L if d else c * L) for d, c in units}
    tri_bf = [jnp.where(_tri(L, bool(d)), 1.0, 0.0).astype(BF16) for d in range(2)]
    tri4 = [c4 <= r4, c4 >= r4]

    qkg = {u: chunk_inputs(u[0], row0[u]) for u in units}
    cum = {u: _dot_exact_lhs(tri_bf[u[0]], qkg[u][2]) for u in units}
    qs, k_out, k_end, dcol, v = {}, {}, {}, {}, {}
    for u in units:
        d, _ = u
        q, k, _ = qkg[u]
        last = cum[u][0:1, :] if d else cum[u][L - 1:L, :]
        q_in = q * jnp.exp(cum[u])
        k_out[u] = (k * jnp.exp(-cum[u])).astype(BF16)
        k_end[u] = (k * jnp.exp(last - cum[u])).astype(BF16)
        qs[u] = jnp.concatenate([jnp.where(m, q_in, 0.0) for m in head_mask], axis=0).astype(BF16)
        dcol[u] = jnp.sum(jnp.where(eye, jnp.exp(last), 0.0), axis=1, keepdims=True)
        v[u] = v_refs[d][row0[u]:row0[u] + L, :].astype(BF16)
    a = {u: _dot_nt(qs[u], k_out[u]) for u in units}
    upd_full = {u: _dot_tn(k_end[u], v[u]) for u in units}
    a = {u: jnp.where(tri4[u[0]], a[u], 0.0).astype(BF16) for u in units}
    o_intra = {u: jnp.concatenate(
        [_dot(a[u][h * L:(h + 1) * L], v[u][:, LIN_DV * h:LIN_DV * (h + 1)]) for h in range(LIN_HEADS)], axis=0)
        for u in units}
    upd = {u: jnp.concatenate(
        [upd_full[u][LIN_DK * h:LIN_DK * (h + 1), LIN_DV * h:LIN_DV * (h + 1)] for h in range(LIN_HEADS)], axis=0)
        for u in units}

    s = [st_ref[0], st_ref[1]]
    for u in units:
        d, _ = u
        o = o_intra[u] + _dot(qs[u], s[d].astype(BF16))
        o_refs[d][row0[u]:row0[u] + L, :] = jnp.concatenate([o[h * L:(h + 1) * L] for h in range(LIN_HEADS)], axis=1)
        s[d] = dcol[u] * s[d] + upd[u]
    st_ref[0] = s[0]
    st_ref[1] = s[1]


def _lin_init(j, s0f_ref, s0b_ref, st_ref):
    _, blk, _ = _seq_of(j)
    is_prompt = j < PROMPT_BLKS

    @pl.when(blk == 0)
    def _():
        st_ref[0] = jnp.where(is_prompt, 0.0, s0f_ref[0])
        st_ref[1] = jnp.where(is_prompt, 0.0, s0b_ref[0])


def _gla_kernel(qkf_ref, vf_ref, qkb_ref, vb_ref, s0f_ref, s0b_ref, wa_ref, ba_ref,
                of_ref, ob_ref, sf_ref, sb_ref, st_ref):
    _lin_init(pl.program_id(0), s0f_ref, s0b_ref, st_ref)
    qk_refs = (qkf_ref, qkb_ref)

    def chunk_inputs(d, r0):
        qk = qk_refs[d][r0:r0 + LIN_L, :]
        q = qk[:, 0:256] * (GLA_DK ** -0.5)
        k = qk[:, 256:512]
        ga_hi, ga_lo = _split2(qk[:, 512:640])
        wa_hi, wa_lo = _split2(wa_ref[d])
        x = _dot(ga_hi, wa_hi) + _dot(ga_lo, wa_hi) + _dot(ga_hi, wa_lo) + ba_ref[d]
        return q, k, _log_sigmoid(x) / GLA_GATE_NORM

    _lin_scan(chunk_inputs, (vf_ref, vb_ref), (of_ref, ob_ref), st_ref)
    sf_ref[0] = st_ref[0]
    sb_ref[0] = st_ref[1]


def _hgrn_kernel(qff_ref, vf_ref, qfb_ref, vb_ref, s0f_ref, s0b_ref, lbl_ref,
                 of_ref, ob_ref, sf_ref, sb_ref, st_ref):
    _lin_init(pl.program_id(0), s0f_ref, s0b_ref, st_ref)
    qf_refs = (qff_ref, qfb_ref)

    def lower_bound(d):
        l0 = lbl_ref[d, 0:1, :]
        l1 = lbl_ref[d, 1:2, :]
        m = jnp.maximum(l0, l1)
        e0 = jnp.exp(l0 - m)
        e1 = jnp.exp(l1 - m)
        p0 = e0 / (e0 + e1)
        p1 = e1 / (e0 + e1)
        return (p0 + p1) - p0

    lbs = (lower_bound(0), lower_bound(1))

    def chunk_inputs(d, r0):
        q = qf_refs[d][r0:r0 + LIN_L, 0:256]
        x = qf_refs[d][r0:r0 + LIN_L, 256 * (d + 1):256 * (d + 2)]
        f = lbs[d] + (1.0 - lbs[d]) * _sigmoid(x)
        return q, 1.0 - f, jnp.log(f)

    _lin_scan(chunk_inputs, (vf_ref, vb_ref), (of_ref, ob_ref), st_ref)
    sf_ref[0] = st_ref[0]
    sb_ref[0] = st_ref[1]


def _lin_call(kernel, name, qk, v, s0f, s0b, params, qk_width):
    st_spec = pl.BlockSpec((1, LIN_QK, LIN_DV), lambda j: (_state_idx(j), 0, 0))
    so_spec = pl.BlockSpec((1, LIN_QK, LIN_DV), lambda j: (_seq_of(j)[0], 0, 0))
    so_shape = jax.ShapeDtypeStruct((N_SEQ, LIN_QK, LIN_DV), F32)
    of, ob, sf, sb = pl.pallas_call(
        kernel,
        grid=(N_BLK,),
        in_specs=[_fwd_spec(qk_width), _fwd_spec(LIN_V), _bwd_spec(qk_width), _bwd_spec(LIN_V),
                  st_spec, st_spec] + [_const_spec(p.shape) for p in params],
        out_specs=[_fwd_spec(LIN_V), _bwd_spec(LIN_V), so_spec, so_spec],
        out_shape=[jax.ShapeDtypeStruct((N_TOK, LIN_V), F32)] * 2 + [so_shape, so_shape],
        scratch_shapes=[pltpu.VMEM((2, LIN_QK, LIN_DV), F32)],
        compiler_params=_cparams(),
        name=name,
    )(qk, v, qk, v, s0f.reshape(DEC_BATCH, LIN_QK, LIN_DV), s0b.reshape(DEC_BATCH, LIN_QK, LIN_DV), *params)
    unpack = lambda s: s[:BATCH].reshape(BATCH, LIN_HEADS, LIN_DK, LIN_DV)
    return of, ob, unpack(sf), unpack(sb)


def _outproj0_kernel(xp_ref, xs_ref, m_ref, of_ref, ob_ref, z_ref, yb_ref, nw_ref, w_ref, o_ref):
    ya = (of_ref[...] + ob_ref[...]) * _silu(z_ref[...])
    ms = jnp.mean(ya * ya, axis=-1, keepdims=True)
    ya = ya * lax.rsqrt(ms + NORM_EPS) * nw_ref[...]
    y = _dot(ya.astype(BF16), w_ref[0:512, :]) + _dot(yb_ref[...], w_ref[512:1024, :])
    o_ref[...] = _tile_of(xp_ref, xs_ref) + m_ref[0, 2:3, :] * y


def _outproj0(xp, xs, mrows, of, ob, z, yb, nw, w):
    return pl.pallas_call(
        _outproj0_kernel,
        grid=(N_TILES,),
        in_specs=[_prompt_tile_spec(), _sample_tile_spec(), _mod_spec(), _row_spec(512), _row_spec(512),
                  _row_spec(512), _row_spec(512), _const_spec((1, 512)), _const_spec((D_MODEL, D_MODEL))],
        out_specs=_row_spec(D_MODEL),
        out_shape=jax.ShapeDtypeStruct((N_TOK, D_MODEL), F32),
        compiler_params=_cparams(),
        name="outproj0",
    )(xp, xs, mrows, of, ob, z, yb, nw, w)


def _head_rms128(o, w):
    parts = []
    for h in range(LIN_HEADS):
        oh = o[:, LIN_DV * h:LIN_DV * (h + 1)]
        ms = jnp.mean(oh * oh, axis=-1, keepdims=True)
        parts.append(oh * lax.rsqrt(ms + NORM_EPS) * w)
    return jnp.concatenate(parts, axis=-1)


def _outproj1_kernel(x_ref, m_ref, gf_ref, gb_ref, gg_ref, hf_ref, hb_ref, hg_ref,
                     gw_ref, hw_ref, w_ref, o_ref):
    yc = _head_rms128(gf_ref[...] + gb_ref[...], gw_ref[...]) * _silu(gg_ref[...])
    yd = _head_rms128(hf_ref[...] + hb_ref[...], hw_ref[...]) * _silu(hg_ref[...])
    y = _dot(yc.astype(BF16), w_ref[0:512, :]) + _dot(yd.astype(BF16), w_ref[512:1024, :])
    o_ref[...] = x_ref[...] + m_ref[0, 2:3, :] * y


def _outproj1(x, mrows, gf, gb, gg, hf, hb, hg, gw, hw, w):
    return pl.pallas_call(
        _outproj1_kernel,
        grid=(N_TILES,),
        in_specs=[_row_spec(D_MODEL), _mod_spec()] + [_row_spec(512)] * 6
                 + [_const_spec((1, LIN_DV)), _const_spec((1, LIN_DV)), _const_spec((D_MODEL, D_MODEL))],
        out_specs=_row_spec(D_MODEL),
        out_shape=jax.ShapeDtypeStruct((N_TOK, D_MODEL), F32),
        compiler_params=_cparams(),
        name="outproj1",
    )(x, mrows, gf, gb, gg, hf, hb, hg, gw, hw, w)


HALO_PER_TILE = TM // SUBLANES


def _ffn_kernel(x_ref, xp_ref, xn_ref, m_ref, nw_ref, wu_ref, cw_ref, cb_ref, wd_ref, *rest, n_out):
    o_refs, (hs_ref, act_ref) = rest[:n_out], rest[n_out:]
    i = pl.program_id(0)
    nw = nw_ref[...]
    sh = m_ref[0, 3:4, :]
    sc = m_ref[0, 4:5, :]
    x = x_ref[...]
    hs_ref[0:TM, :] = _norm_mod(x, nw, sh, sc).astype(BF16)
    halo = _norm_mod(jnp.concatenate([xn_ref[...], xp_ref[...]], axis=0), nw, sh, sc)
    hs_ref[TM:TM_EXT, :] = (halo * _halo_keep(i)).astype(BF16)
    is_prompt = i < PROMPT_TILES

    def conv_cols(c0):
        cs = slice(c0, c0 + FFN_CH)
        return _dwconv_tile(_dot(hs_ref[...], wu_ref[:, cs]), cw_ref[:, cs], cb_ref[:, cs], i)

    for c in range(FFN_DIM // FFN_CH):
        a = conv_cols(c * FFN_CH)
        b = conv_cols(FFN_DIM + c * FFN_CH)
        act_ref[:, c * FFN_CH:(c + 1) * FFN_CH] = (_silu(a) * b).astype(BF16)
    out = x + m_ref[0, 5:6, :] * _dot(act_ref[...], wd_ref[...])
    if len(o_refs) == 1:
        o_refs[0][...] = out
    else:
        @pl.when(is_prompt)
        def _():
            o_refs[0][...] = out

        @pl.when(jnp.logical_not(is_prompt))
        def _():
            o_refs[1][...] = out


def _ffn(x, mrows, nw, wu, cw, cb, wd, split_out):
    prev = pl.BlockSpec((SUBLANES, D_MODEL), lambda i: (jnp.maximum(i * HALO_PER_TILE - 1, 0), 0))
    nxt = pl.BlockSpec((SUBLANES, D_MODEL),
                       lambda i: (jnp.minimum((i + 1) * HALO_PER_TILE, N_TOK // SUBLANES - 1), 0))
    single = dict(pipeline_mode=pl.Buffered(1))
    if split_out:
        out_specs = [_prompt_tile_spec(), _sample_tile_spec()]
        out_shape = [jax.ShapeDtypeStruct((N_PROMPT, D_MODEL), F32), jax.ShapeDtypeStruct((N_SAMPLE, D_MODEL), F32)]
    else:
        out_specs = [_row_spec(D_MODEL)]
        out_shape = [jax.ShapeDtypeStruct((N_TOK, D_MODEL), F32)]
    return pl.pallas_call(
        functools.partial(_ffn_kernel, n_out=len(out_specs)),
        grid=(N_TILES,),
        in_specs=[_row_spec(D_MODEL), prev, nxt, _mod_spec(), _const_spec((1, D_MODEL)),
                  pl.BlockSpec((D_MODEL, 2 * FFN_DIM), lambda i: (0, 0), **single),
                  _const_spec((3, 2 * FFN_DIM)), _const_spec((1, 2 * FFN_DIM)),
                  pl.BlockSpec((FFN_DIM, D_MODEL), lambda i: (0, 0), **single)],
        out_specs=out_specs,
        out_shape=out_shape,
        scratch_shapes=[pltpu.VMEM((TM_EXT, D_MODEL), BF16), pltpu.VMEM((TM, FFN_DIM), BF16)],
        compiler_params=_cparams(),
        name="ffn",
    )(x, x, x, mrows, nw, wu, cw, cb, wd)


def _pad_lanes(v, width=128):
    v = v.reshape(1, -1)
    return jnp.pad(v, ((0, 0), (0, width - v.shape[1])))


def kernel(x_prompt, x_sample, cache_na_k_l0, cache_na_v_l0, state_ssd_fwd_l0, state_ssd_bwd_l0,
           state_gla_fwd_l1, state_gla_bwd_l1, state_hgrn_fwd_l1, state_hgrn_bwd_l1, c,
           c_ctx, w_ada, b_ada, norm_w, ffn_w_up, ffn_conv_w, ffn_conv_b, ffn_w_down,
           w_in_l0, w_out_l0, ssd_conv_w_l0, ssd_conv_b_l0, ssd_dt_bias_l0, ssd_a_log_l0, ssd_d_l0,
           ssd_norm_w_l0, na_q_norm_l0, na_k_norm_l0, na_rpb_l0,
           w_in_l1, w_out_l1, gla_wa2_l1, gla_ba2_l1, gla_norm_w_l1, hgrn_lb_logits, hgrn_norm_w_l1):
    xp = x_prompt.reshape(N_PROMPT, D_MODEL)
    xs = x_sample.reshape(N_SAMPLE, D_MODEL)

    cvec8 = jnp.zeros((SUBLANES, D_MODEL), F32).at[0:DEC_BATCH].set(c).at[CTX_MOD_ROW].set(c_ctx)
    mods = _mods(cvec8, w_ada, b_ada)
    mods = mods.reshape(2, SUBLANES, 6, D_MODEL)
    mods = jnp.pad(mods, ((0, 0), (0, 0), (0, SUBLANES - 6), (0, 0)))

    zpad = lambda n: jnp.zeros((D_MODEL, n), F32)
    w0 = jnp.concatenate([w_in_l0[:, :1296], zpad(112), w_in_l0[:, 1296:]], axis=1).astype(BF16)
    w1 = jnp.concatenate([w_in_l1[:, 0:512], w_in_l1[:, 1536:1568], zpad(96),
                          w_in_l1[:, 512:1536], w_in_l1[:, 1568:3360]], axis=1).astype(BF16)
    seg = jnp.kron(jnp.eye(NA_HEADS, dtype=F32), jnp.full((NA_HEAD_DIM, NA_HEAD_DIM), 1.0 / NA_HEAD_DIM, F32)).astype(BF16)

    m0 = mods[0]
    z, xc, dt, qn, knb, vb, kn, v = _inproj0(
        xp, xs, m0, norm_w[0, 0].reshape(1, D_MODEL), w0, seg,
        jnp.tile(na_q_norm_l0, NA_HEADS).reshape(1, NA_WIDTH), jnp.tile(na_k_norm_l0, NA_HEADS).reshape(1, NA_WIDTH),
        ssd_conv_w_l0, ssd_conv_b_l0.reshape(1, SSD_XBC))
    of, ob, ssd_f, ssd_b = _ssd(
        xc, dt, state_ssd_fwd_l0, state_ssd_bwd_l0, _pad_lanes(ssd_dt_bias_l0), _pad_lanes(ssd_a_log_l0),
        jnp.repeat(ssd_d_l0, SSD_HEAD_DIM).reshape(1, SSD_WIDTH))
    yb, na_k, na_v = _na_ctx(qn, kn, v)
    yb = _na_lat(qn, knb, vb, cache_na_k_l0, cache_na_v_l0, _na_bias_table(na_rpb_l0), yb)
    x = _outproj0(xp, xs, m0, of, ob, z, yb, ssd_norm_w_l0.reshape(1, SSD_WIDTH), w_out_l0.astype(BF16))
    x, = _ffn(x, m0, norm_w[0, 1].reshape(1, D_MODEL), ffn_w_up[0].astype(BF16), ffn_conv_w[0],
              ffn_conv_b[0].reshape(1, 2 * FFN_DIM), ffn_w_down[0].astype(BF16), split_out=False)

    m1 = mods[1]
    gqk, gv, gg, hqf, hi, hg = _inproj1(x, m1, norm_w[1, 0].reshape(1, D_MODEL), w1)
    wa_pad = jnp.zeros((2, 128, LIN_QK), F32)
    wa_pad = wa_pad.at[0, 0:GLA_RANK].set(gla_wa2_l1[0]).at[1, GLA_RANK:2 * GLA_RANK].set(gla_wa2_l1[1])
    gf, gb, gla_f, gla_b = _lin_call(
        _gla_kernel, "gla", gqk, gv, state_gla_fwd_l1, state_gla_bwd_l1,
        (wa_pad, gla_ba2_l1.reshape(2, 1, LIN_QK)), 640)
    hf, hb, hgrn_f, hgrn_b = _lin_call(
        _hgrn_kernel, "hgrn", hqf, hi, state_hgrn_fwd_l1, state_hgrn_bwd_l1, (hgrn_lb_logits,), 768)
    x = _outproj1(x, m1, gf, gb, gg, hf, hb, hg, gla_norm_w_l1.reshape(1, LIN_DV),
                  hgrn_norm_w_l1.reshape(1, LIN_DV), w_out_l1.astype(BF16))
    y_p, y_s = _ffn(x, m1, norm_w[1, 1].reshape(1, D_MODEL), ffn_w_up[1].astype(BF16), ffn_conv_w[1],
                    ffn_conv_b[1].reshape(1, 2 * FFN_DIM), ffn_w_down[1].astype(BF16), split_out=True)
    return (y_p.reshape(BATCH, SEQ, D_MODEL), y_s.reshape(DEC_BATCH, DEC_SEQ, D_MODEL),
            na_k, na_v, ssd_f, ssd_b, gla_f, gla_b, hgrn_f, hgrn_b)
```

```python
import functools

import jax
import jax.numpy as jnp
from jax import lax
from jax.experimental import pallas as pl
from jax.experimental.pallas import tpu as pltpu

F32 = jnp.float32
BF16 = jnp.bfloat16
HIGHEST = lax.Precision.HIGHEST

D_MODEL = 1024
BATCH = 32
SEQ = 256
DEC_BATCH = 4
DEC_SEQ = 4096
PAST_LEN = 256
GRID_W = 64
NORM_EPS = 1e-6
N_PROMPT = BATCH * SEQ
N_SAMPLE = DEC_BATCH * DEC_SEQ
N_TOK = N_PROMPT + N_SAMPLE

SSD_HEADS = 8
SSD_HEAD_DIM = 64
SSD_STATE = 64
SSD_WIDTH = 512
SSD_BC = 128
SSD_XBC = 768
NA_HEADS = 8
NA_HEAD_DIM = 64
NA_WIDTH = 512
NA_WIN_ROWS = 8
NA_WIN_COLS = 16
NA_SCALE = NA_HEAD_DIM ** -0.5
GLA_HEADS = 4
GLA_DK = 64
GLA_DV = 128
GLA_RANK = 16
GLA_GATE_NORM = 16.0
HGRN_HEADS = 4
FFN_DIM = 2816

V7X_VMEM_BYTES = 64 * 1024 * 1024
VMEM_LIMIT = 56 * 1024 * 1024
SUBLANES = 8

TM = 512
N_TILES = N_TOK // TM
PROMPT_TILES = N_PROMPT // TM
TILES_PER_SAMPLE = DEC_SEQ // TM
CTX_MOD_ROW = DEC_BATCH

RB = 256
N_BLK = N_TOK // RB
PROMPT_BLKS = N_PROMPT // RB
BLKS_PER_SAMPLE = DEC_SEQ // RB
N_SEQ = BATCH + DEC_BATCH
SSD_L = 128
LIN_L = 64
NA_R = 8
NA_ROWS_PER_ITER = 4
FFN_CH = 256
W0_COLS = 2944
W1_COLS = 3456


def _cparams(n_axes=1):
    return pltpu.CompilerParams(dimension_semantics=("arbitrary",) * n_axes,
                                vmem_limit_bytes=VMEM_LIMIT)


def _const_spec(shape):
    nd = len(shape)
    return pl.BlockSpec(shape, lambda *_: (0,) * nd)


def _sigmoid(x):
    return 1.0 / (1.0 + jnp.exp(-x))


def _silu(x):
    return x * _sigmoid(x)


def _softplus(x):
    return jnp.maximum(x, 0.0) + jnp.log(1.0 + jnp.exp(-jnp.abs(x)))


def _mod_row(i):
    return jnp.where(i < PROMPT_TILES, CTX_MOD_ROW, (i - PROMPT_TILES) // TILES_PER_SAMPLE)


def _dot(a, b, **kw):
    return jnp.dot(a, b, preferred_element_type=F32, **kw)


def _dot_nt(a, b):
    return lax.dot_general(a, b, (((1,), (1,)), ((), ())), preferred_element_type=F32)


def _skewed(units, stages):
    results = {}
    for step in range(len(units) + len(stages) - 1):
        for k, stage in enumerate(stages):
            i = step - k
            if 0 <= i < len(units):
                results[(k, i)] = stage(units[i], results.pop((k - 1, i), None))


def _staged(units, stages):
    results = {i: None for i in range(len(units))}
    for stage in stages:
        for i, u in enumerate(units):
            results[i] = stage(u, results[i])


def _dot_tn(a, b):
    return lax.dot_general(a, b, (((0,), (0,)), ((), ())), preferred_element_type=F32)


MODS_NB = 1536


def _mods_kernel(c_ref, w_ref, b_ref, o_ref):
    s = _silu(c_ref[...])
    o_ref[0] = _dot(s, w_ref[0], precision=HIGHEST) + b_ref[0]


def _mods(cvec8, w_ada, b_ada):
    depth = w_ada.shape[0]
    nb = 6 * D_MODEL // MODS_NB
    return pl.pallas_call(
        _mods_kernel,
        grid=(depth, nb),
        in_specs=[
            _const_spec((SUBLANES, D_MODEL)),
            pl.BlockSpec((1, D_MODEL, MODS_NB), lambda l, j: (l, 0, j)),
            pl.BlockSpec((1, 1, MODS_NB), lambda l, j: (l, 0, j)),
        ],
        out_specs=pl.BlockSpec((1, SUBLANES, MODS_NB), lambda l, j: (l, 0, j)),
        out_shape=jax.ShapeDtypeStruct((depth, SUBLANES, 6 * D_MODEL), F32),
        compiler_params=_cparams(2),
        name="mods",
    )(cvec8, w_ada, b_ada.reshape(depth, 1, 6 * D_MODEL))


def _norm_mod(x, nw, sh, sc):
    ms = jnp.mean(x * x, axis=-1, keepdims=True)
    y = x * lax.rsqrt(ms + NORM_EPS) * nw
    return y * (1.0 + sc) + sh


def _head_rms(x, seg, w):
    ms = _dot((x * x).astype(BF16), seg)
    return x * lax.rsqrt(ms + NORM_EPS) * w


def _tile_of(xp_ref, xs_ref):
    return jnp.where(pl.program_id(0) < PROMPT_TILES, xp_ref[...], xs_ref[...])


HALO_ROWS = 2 * SUBLANES
TM_EXT = TM + HALO_ROWS


def _halo_keep(i):
    seq = jnp.where(i < PROMPT_TILES, SEQ, DEC_SEQ)
    keep_prev = jnp.where(((i * TM) & (seq - 1)) == 0, 0.0, 1.0)
    keep_next = jnp.where(((i * TM + TM) & (seq - 1)) == 0, 0.0, 1.0)
    hrow = lax.broadcasted_iota(jnp.int32, (HALO_ROWS, 1), 0)
    return jnp.where(hrow < SUBLANES, keep_next, keep_prev)


def _dwconv_tile(u_all, w, bias, i):
    um = pltpu.roll(u_all, 1, axis=0)[0:TM] * w[0:1]
    up = pltpu.roll(u_all, TM_EXT - 1, axis=0)[0:TM] * w[2:3]
    out = bias + um + u_all[0:TM] * w[1:2] + up
    edge_on = jnp.where(i < PROMPT_TILES, 1.0, 0.0)
    erow = lax.broadcasted_iota(jnp.int32, (HALO_ROWS, 1), 0)
    at_first = jnp.where(erow == SUBLANES, edge_on, 0.0)
    at_last = jnp.where(erow == SUBLANES - 1, edge_on, 0.0)
    pieces, r = [], 0
    for edge in range(SEQ, TM, SEQ):
        sl = slice(edge - SUBLANES, edge + SUBLANES)
        pieces += [out[r:edge - SUBLANES], out[sl] - at_first * um[sl] - at_last * up[sl]]
        r = edge + SUBLANES
    return jnp.concatenate(pieces + [out[r:TM]], axis=0)


def _inproj0_kernel(xp_ref, xs_ref, xn_ref, xv_ref, m_ref, nw_ref, w_ref, seg_ref, qw_ref, kw_ref, cw_ref, cb_ref,
                    z_ref, xc_ref, dt_ref, qn_ref, knb_ref, vb_ref, kn_ref, v_ref, hs_ref):
    i = pl.program_id(0)
    nw, sh, sc = nw_ref[...], m_ref[0, 0:1, :], m_ref[0, 1:2, :]
    hs_ref[0:TM, :] = _norm_mod(_tile_of(xp_ref, xs_ref), nw, sh, sc).astype(BF16)
    halo = _norm_mod(jnp.concatenate([xn_ref[...], xv_ref[...]], axis=0), nw, sh, sc)
    hs_ref[TM:TM_EXT, :] = (halo * _halo_keep(i)).astype(BF16)
    z_ref[...] = _dot(hs_ref[0:TM, :], w_ref[:, 0:512]).astype(BF16)
    xc_ref[...] = _silu(_dwconv_tile(_dot(hs_ref[...], w_ref[:, 512:1280]), cw_ref[...], cb_ref[...], i))
    dt_ref[...] = _dot(hs_ref[0:TM, :], w_ref[:, 1280:1408])
    q = _dot(hs_ref[0:TM, :], w_ref[:, 1408:1920])
    k = _dot(hs_ref[0:TM, :], w_ref[:, 1920:2432])
    v = _dot(hs_ref[0:TM, :], w_ref[:, 2432:2944])
    vb_ref[...] = v.astype(BF16)
    ms_q = _dot((q * q).astype(BF16), seg_ref[...])
    ms_k = _dot((k * k).astype(BF16), seg_ref[...])
    qn_ref[...] = (q * lax.rsqrt(ms_q + NORM_EPS) * qw_ref[...] * NA_SCALE).astype(BF16)
    kn = k * lax.rsqrt(ms_k + NORM_EPS) * kw_ref[...]
    knb_ref[...] = kn.astype(BF16)

    @pl.when(pl.program_id(0) < PROMPT_TILES)
    def _():
        kn_ref[...] = kn
        v_ref[...] = v


def _row_spec(width):
    return pl.BlockSpec((TM, width), lambda i: (i, 0))


def _mod_spec():
    return pl.BlockSpec((1, SUBLANES, D_MODEL), lambda i: (_mod_row(i), 0, 0))


def _prompt_tile_spec(width=D_MODEL):
    return pl.BlockSpec((TM, width), lambda i: (jnp.minimum(i, PROMPT_TILES - 1), 0))


def _sample_tile_spec(width=D_MODEL):
    return pl.BlockSpec((TM, width), lambda i: (jnp.maximum(i - PROMPT_TILES, 0), 0))


def _inproj0(xp, xs, mrows, nw, w0, seg, qw, kw, conv_w, conv_b):
    widths = (512, 768, 128, 512, 512, 512)
    dtypes = (BF16, F32, F32, BF16, BF16, BF16)
    halo_per_tile = TM // SUBLANES
    last_halo = N_SAMPLE // SUBLANES - 1
    nxt = pl.BlockSpec((SUBLANES, D_MODEL),
                       lambda i: (jnp.clip((i - PROMPT_TILES + 1) * halo_per_tile, 0, last_halo), 0))
    prv = pl.BlockSpec((SUBLANES, D_MODEL),
                       lambda i: (jnp.clip((i - PROMPT_TILES) * halo_per_tile - 1, 0, last_halo), 0))
    return pl.pallas_call(
        _inproj0_kernel,
        grid=(N_TILES,),
        in_specs=[
            _prompt_tile_spec(), _sample_tile_spec(), nxt, prv, _mod_spec(), _const_spec((1, D_MODEL)),
            _const_spec((D_MODEL, W0_COLS)), _const_spec((512, 512)),
            _const_spec((1, 512)), _const_spec((1, 512)), _const_spec((3, SSD_XBC)), _const_spec((1, SSD_XBC)),
        ],
        out_specs=[_row_spec(w) for w in widths] + [_prompt_tile_spec(NA_WIDTH)] * 2,
        out_shape=[jax.ShapeDtypeStruct((N_TOK, w), d) for w, d in zip(widths, dtypes)]
                  + [jax.ShapeDtypeStruct((N_PROMPT, NA_WIDTH), F32)] * 2,
        scratch_shapes=[pltpu.VMEM((TM_EXT, D_MODEL), BF16)],
        compiler_params=_cparams(),
        name="inproj0",
    )(xp, xs, xs, xs, mrows, nw, w0, seg, qw, kw, conv_w, conv_b)


def _inproj1_kernel(x_ref, m_ref, nw_ref, w_ref, gqk_ref, gv_ref, gg_ref, hqf_ref, hi_ref, hg_ref, hs_ref):
    hs_ref[...] = _norm_mod(x_ref[...], nw_ref[...], m_ref[0, 0:1, :], m_ref[0, 1:2, :]).astype(BF16)
    gqk_ref[...] = _dot(hs_ref[...], w_ref[:, 0:640])
    gv_ref[...] = _dot(hs_ref[...], w_ref[:, 640:1152]).astype(BF16)
    gg_ref[...] = _dot(hs_ref[...], w_ref[:, 1152:1664]).astype(BF16)
    hqf_ref[...] = _dot(hs_ref[...], w_ref[:, 1664:2432])
    hi_ref[...] = _dot(hs_ref[...], w_ref[:, 2432:2944]).astype(BF16)
    hg_ref[...] = _dot(hs_ref[...], w_ref[:, 2944:3456]).astype(BF16)


def _inproj1(x, mrows, nw, w1):
    widths = (640, 512, 512, 768, 512, 512)
    dtypes = (F32, BF16, BF16, F32, BF16, BF16)
    return pl.pallas_call(
        _inproj1_kernel,
        grid=(N_TILES,),
        in_specs=[_row_spec(D_MODEL), _mod_spec(), _const_spec((1, D_MODEL)),
                  _const_spec((D_MODEL, W1_COLS))],
        out_specs=[_row_spec(w) for w in widths],
        out_shape=[jax.ShapeDtypeStruct((N_TOK, w), d) for w, d in zip(widths, dtypes)],
        scratch_shapes=[pltpu.VMEM((TM, D_MODEL), BF16)],
        compiler_params=_cparams(),
        name="inproj1",
    )(x, mrows, nw, w1)


def _seq_of(j):
    jj = j - PROMPT_BLKS
    is_p = j < PROMPT_BLKS
    seq = jnp.where(is_p, j, BATCH + jj // BLKS_PER_SAMPLE)
    blk = jnp.where(is_p, 0, jj % BLKS_PER_SAMPLE)
    nblk = jnp.where(is_p, 1, BLKS_PER_SAMPLE)
    return seq, blk, nblk


def _bwd_blk(j):
    _, blk, nblk = _seq_of(j)
    return j - blk + (nblk - 1 - blk)


def _state_idx(j):
    seq, _, _ = _seq_of(j)
    return jnp.maximum(seq - BATCH, 0)


def _fwd_spec(width):
    return pl.BlockSpec((RB, width), lambda j: (j, 0))


def _bwd_spec(width):
    return pl.BlockSpec((RB, width), lambda j: (_bwd_blk(j), 0))


def _tri(n, upper):
    r = lax.broadcasted_iota(jnp.int32, (n, n), 0)
    c = lax.broadcasted_iota(jnp.int32, (n, n), 1)
    return (c >= r) if upper else (c <= r)


def _diag_blocks(x, rows):
    lh = lax.broadcasted_iota(jnp.int32, (1, 4 * SSD_HEAD_DIM), 1) >> 6
    out = x[3 * rows:4 * rows]
    for h in (2, 1, 0):
        out = jnp.where(lh == h, x[h * rows:(h + 1) * rows], out)
    return out


def _ssd_kernel(xf_ref, dtf_ref, xb_ref, dtb_ref, s0f_ref, s0b_ref, dtbias_ref, alog_ref, dsk_ref, ex_ref,
                of_ref, ob_ref, sf_ref, sb_ref, st_ref):
    j = pl.program_id(0)
    _, blk, _ = _seq_of(j)
    is_prompt = j < PROMPT_BLKS

    @pl.when(blk == 0)
    def _():
        st_ref[0] = jnp.where(is_prompt, 0.0, s0f_ref[0])
        st_ref[1] = jnp.where(is_prompt, 0.0, s0b_ref[0])

    a_neg = -jnp.exp(alog_ref[...])
    a_col = jnp.broadcast_to(a_neg, (SUBLANES, 128)).T
    L = SSD_L
    nch = RB // L
    xc_refs = (xf_ref, xb_ref)
    dt_refs = (dtf_ref, dtb_ref)
    o_refs = (of_ref, ob_ref)

    units = [(d, c) for c in range(nch) for d in range(2)]
    row0 = {(d, c): ((nch - 1 - c) * L if d else c * L) for d, c in units}
    tri = [_tri(L, False), _tri(L, True)]
    tri_bf = [jnp.where(t, 1.0, 0.0).astype(BF16) for t in tri]
    lane = lax.broadcasted_iota(jnp.int32, (1, SSD_BC), 1)
    gmask = [lane < SSD_STATE, lane >= SSD_STATE]
    eye_bf = jnp.where(tri[0] & tri[1], 1.0, 0.0).astype(BF16)
    zero_blk = jnp.zeros((SSD_STATE, 4 * SSD_HEAD_DIM), F32)

    def gates(u, _):
        d, r0 = u[0], row0[u]
        dt = _softplus(dt_refs[d][r0:r0 + L, :] + dtbias_ref[...])
        cum = _dot_exact_lhs(tri_bf[d], dt * a_neg)
        dt8 = dt.T[8 * d:8 * d + 8]
        ct8 = _dot_exact_rhs(dt8 * a_col[8 * d:8 * d + 8, 0:1], tri_bf[1 - d])
        parts = _split3(cum[0:SUBLANES] if d else cum[L - SUBLANES:L])
        last_x = _dot(parts[0], ex_ref[d]) + _dot(parts[1], ex_ref[d]) + _dot(parts[2], ex_ref[d])
        e_last_x = jnp.exp(last_x[0:1] if d else last_x[SUBLANES - 1:SUBLANES])
        last_col = ct8[:, 0:1] if d else ct8[:, L - 1:L]
        f_end = jnp.exp(last_col - ct8) * dt8
        bm = xc_refs[d][r0:r0 + L, 512:640].astype(BF16)
        cm = xc_refs[d][r0:r0 + L, 640:768]
        b_t = _dot_nt(eye_bf, bm)
        cb_g = [_dot_nt(jnp.where(gmask[g], cm, 0.0).astype(BF16), bm) for g in range(2)]
        return cum, ct8, dt8, e_last_x, f_end, b_t, cb_g, cm

    def operands(u, p):
        d = u[0]
        cum, ct8, dt8, e_last_x, f_end, b_t, cb_g, cm = p
        lhs, k_t = [], []
        for g in range(2):
            ls, ks = [], []
            for hh in range(4):
                h = 4 * g + hh
                pc = jnp.broadcast_to(cum[:, 8 * d + h:8 * d + h + 1], (L, L))
                dec = jnp.exp(jnp.where(tri[d], pc - ct8[h:h + 1, :], -jnp.inf))
                w = cb_g[g] * dec * dt8[h:h + 1, :]
                ls.append(jnp.concatenate([w.astype(BF16), (cm * jnp.exp(pc)).astype(BF16)], axis=1))
                ks.append((b_t[SSD_STATE * g:SSD_STATE * (g + 1)] * f_end[h:h + 1, :]).astype(BF16))
            lhs.append(jnp.concatenate(ls, axis=0))
            k_t.append(jnp.concatenate(ks, axis=0))
        return lhs, k_t, e_last_x

    def state_update(u, p):
        d, r0 = u[0], row0[u]
        lhs, k_t, e_last_x = p
        x_bf = [xc_refs[d][r0:r0 + L, 256 * g:256 * (g + 1)].astype(BF16) for g in range(2)]
        up = [_diag_blocks(_dot(k_t[g], x_bf[g]), SSD_STATE) for g in range(2)]
        upd = jnp.concatenate([jnp.concatenate([up[0], zero_blk], axis=1),
                               jnp.concatenate([zero_blk, up[1]], axis=1)], axis=0)
        return lhs, x_bf, e_last_x, upd

    s = [st_ref[0], st_ref[1]]

    def outputs(u, p):
        d, r0 = u[0], row0[u]
        lhs, x_bf, e_last_x, upd = p
        s_bf = s[d].astype(BF16)
        o = jnp.concatenate(
            [_diag_blocks(_dot(lhs[g], jnp.concatenate([x_bf[g], s_bf[:, 256 * g:256 * (g + 1)]], axis=0)), L)
             for g in range(2)], axis=1)
        if d == 0:
            o = o + dsk_ref[...] * xf_ref[r0:r0 + L, 0:SSD_WIDTH]
        o_refs[d][r0:r0 + L, :] = o.astype(BF16)
        s[d] = e_last_x * s[d] + upd

    _skewed(units, (gates, operands, state_update, outputs))
    st_ref[0] = s[0]
    st_ref[1] = s[1]
    sf_ref[0] = s[0]
    sb_ref[0] = s[1]


def _ssd_pack_state(s):
    b = s.shape[0]
    g = s.reshape(b, 2, 4, SSD_STATE, SSD_HEAD_DIM).transpose(0, 1, 3, 2, 4).reshape(b, 2, SSD_STATE, 256)
    z = jnp.zeros((b, SSD_STATE, 256), F32)
    return jnp.concatenate([jnp.concatenate([g[:, 0], z], axis=2), jnp.concatenate([z, g[:, 1]], axis=2)], axis=1)


def _ssd_unpack_state(s):
    b = s.shape[0]
    g = jnp.stack([s[:, 0:SSD_STATE, 0:256], s[:, SSD_STATE:, 256:512]], axis=1)
    g = g.reshape(b, 2, SSD_STATE, 4, SSD_HEAD_DIM).transpose(0, 1, 3, 2, 4)
    return g.reshape(b, SSD_HEADS, SSD_STATE, SSD_HEAD_DIM)


def _ssd(xc, dt, s0f, s0b, dtbias, alog, dsk):
    st_shape = (1, 2 * SSD_STATE, SSD_WIDTH)
    st_spec = pl.BlockSpec(st_shape, lambda j: (_state_idx(j), 0, 0))
    so_spec = pl.BlockSpec(st_shape, lambda j: (_seq_of(j)[0], 0, 0))
    so_shape = jax.ShapeDtypeStruct((N_SEQ, 2 * SSD_STATE, SSD_WIDTH), F32)
    col = jnp.arange(128)[:, None]
    lane_head = jnp.arange(SSD_WIDTH)[None, :] // SSD_HEAD_DIM
    expand = jnp.stack([col == lane_head, col == SSD_HEADS + lane_head]).astype(BF16)
    of, ob, sf, sb = pl.pallas_call(
        _ssd_kernel,
        grid=(N_BLK,),
        in_specs=[
            _fwd_spec(SSD_XBC), _fwd_spec(128), _bwd_spec(SSD_XBC), _bwd_spec(128), st_spec, st_spec,
            _const_spec((1, 128)), _const_spec((1, 128)), _const_spec((1, SSD_WIDTH)),
            _const_spec((2, 128, SSD_WIDTH)),
        ],
        out_specs=[_fwd_spec(SSD_WIDTH), _bwd_spec(SSD_WIDTH), so_spec, so_spec],
        out_shape=[jax.ShapeDtypeStruct((N_TOK, SSD_WIDTH), BF16)] * 2 + [so_shape, so_shape],
        scratch_shapes=[pltpu.VMEM((2, 2 * SSD_STATE, SSD_WIDTH), F32)],
        compiler_params=_cparams(),
        name="ssd",
    )(xc, dt, xc, dt, _ssd_pack_state(s0f), _ssd_pack_state(s0b), dtbias, alog, dsk, expand)
    return of, ob, _ssd_unpack_state(sf[:BATCH]), _ssd_unpack_state(sb[:BATCH])


NA_PAIRS = NA_HEADS // 2


def _stack_pair(qt):
    lower = lax.broadcasted_iota(jnp.int32, (1, 2 * NA_HEAD_DIM), 1) < NA_HEAD_DIM
    zero = jnp.zeros_like(qt)
    return jnp.concatenate([jnp.where(lower, qt, zero), jnp.where(lower, zero, qt)], axis=0)


def _unstack_pair(x, n):
    lower = lax.broadcasted_iota(jnp.int32, (1, 2 * NA_HEAD_DIM), 1) < NA_HEAD_DIM
    return jnp.where(lower, x[0:n], x[n:2 * n])


def _na_ctx_kernel(q_ref, k_ref, v_ref, y_ref, ko_ref, vo_ref):
    tiles = [slice(128 * i, 128 * (i + 1)) for i in range(NA_PAIRS)]
    s = [_dot_nt(_stack_pair(q_ref[:, ts]), k_ref[:, ts].astype(BF16)) for ts in tiles]
    p, l = [], []
    for i in range(NA_PAIRS):
        e = jnp.exp(s[i] - jnp.max(s[i], axis=-1, keepdims=True))
        l.append(jnp.sum(e, axis=-1, keepdims=True))
        p.append(e.astype(BF16))
    o = [_dot(p[i], v_ref[:, tiles[i]].astype(BF16)) for i in range(NA_PAIRS)]
    for i in range(NA_PAIRS):
        y = _unstack_pair(o[i], SEQ) / _unstack_pair(jnp.broadcast_to(l[i], o[i].shape), SEQ)
        y_ref[:, tiles[i]] = y.astype(BF16)
    for h in range(NA_HEADS):
        sl = slice(NA_HEAD_DIM * h, NA_HEAD_DIM * (h + 1))
        ko_ref[0, h] = k_ref[:, sl]
        vo_ref[0, h] = v_ref[:, sl]


def _na_ctx(qn, kn, v):
    blk = lambda w: pl.BlockSpec((SEQ, w), lambda b: (b, 0))
    hm = pl.BlockSpec((1, NA_HEADS, SEQ, NA_HEAD_DIM), lambda b: (b, 0, 0, 0))
    hm_shape = jax.ShapeDtypeStruct((BATCH, NA_HEADS, SEQ, NA_HEAD_DIM), F32)
    return pl.pallas_call(
        _na_ctx_kernel,
        grid=(BATCH,),
        in_specs=[blk(NA_WIDTH)] * 3,
        out_specs=[blk(NA_WIDTH), hm, hm],
        out_shape=[jax.ShapeDtypeStruct((N_TOK, NA_WIDTH), BF16), hm_shape, hm_shape],
        compiler_params=_cparams(),
        name="na_ctx",
    )(qn, kn, v)


GRID_ROWS = DEC_SEQ // GRID_W
NA_LOC = NA_WIN_ROWS * GRID_W
NA_MASKED = -1e30


def _na_lat_kernel(q_ref, k_ref, v_ref, kc_ref, vc_ref, bt_ref, yin_ref, y_ref):
    del yin_ref
    rb = pl.program_id(1)

    tiles = [slice(128 * i, 128 * (i + 1)) for i in range(NA_PAIRS)]

    def rows(it, carry):
        units, q0, k0, dr0 = [], {}, {}, {}
        for rr_ in range(NA_ROWS_PER_ITER):
            r = it * NA_ROWS_PER_ITER + rr_
            rr = rb * NA_R + r
            rs = jnp.clip(rr - NA_WIN_ROWS // 2, 0, GRID_ROWS - NA_WIN_ROWS)
            dr0[rr_] = rs - rr + (NA_WIN_ROWS - 1)
            q0[rr_] = pl.multiple_of(r * GRID_W, GRID_W)
            k0[rr_] = pl.multiple_of(rs * GRID_W, GRID_W)
            units += [(rr_, i) for i in range(NA_PAIRS)]
        def scores(u, _):
            r, i = u
            qq = _stack_pair(q_ref[pl.ds(q0[r], GRID_W), tiles[i]])
            s_loc = (_dot_nt(qq, k_ref[pl.ds(k0[r], NA_LOC), tiles[i]])
                     + bt_ref[dr0[r], 2 * i:2 * i + 2].reshape(2 * GRID_W, NA_LOC))
            return s_loc, _dot_nt(qq, kc_ref[0, :, tiles[i]])

        def softmax(u, s):
            s_loc, s_ctx = s
            m = jnp.maximum(jnp.max(s_loc, axis=-1, keepdims=True), jnp.max(s_ctx, axis=-1, keepdims=True))
            e_loc = jnp.exp(s_loc - m)
            e_ctx = jnp.exp(s_ctx - m)
            l = jnp.sum(e_loc, axis=-1, keepdims=True) + jnp.sum(e_ctx, axis=-1, keepdims=True)
            return e_loc.astype(BF16), e_ctx.astype(BF16), l

        def weighted(u, p):
            r, i = u
            p_loc, p_ctx, l = p
            o = _dot(p_loc, v_ref[pl.ds(k0[r], NA_LOC), tiles[i]]) + _dot(p_ctx, vc_ref[0, :, tiles[i]])
            y = _unstack_pair(o, GRID_W) / _unstack_pair(jnp.broadcast_to(l, o.shape), GRID_W)
            y_ref[pl.ds(q0[r], GRID_W), tiles[i]] = y.astype(BF16)

        _skewed(units, (scores, softmax, weighted))
        return carry

    lax.fori_loop(0, NA_R // NA_ROWS_PER_ITER, rows, 0)


def _na_lat(qn, knb, vb, kc, vc, btab, y_in):
    rows_per_step = NA_R * GRID_W
    steps = GRID_ROWS // NA_R
    off_q = N_PROMPT // rows_per_step
    off_s = N_PROMPT // DEC_SEQ
    qspec = pl.BlockSpec((rows_per_step, NA_WIDTH), lambda b, r: (off_q + b * steps + r, 0))
    kvspec = pl.BlockSpec((DEC_SEQ, NA_WIDTH), lambda b, r: (off_s + b, 0))
    cspec = pl.BlockSpec((1, PAST_LEN, NA_WIDTH), lambda b, r: (b, 0, 0))
    token_major = lambda a: a.transpose(0, 2, 1, 3).reshape(DEC_BATCH, PAST_LEN, NA_WIDTH).astype(BF16)
    kc, vc = token_major(kc), token_major(vc)
    return pl.pallas_call(
        _na_lat_kernel,
        grid=(DEC_BATCH, steps),
        in_specs=[qspec, kvspec, kvspec, cspec, cspec,
                  _const_spec((NA_WIN_ROWS, NA_HEADS, GRID_W, NA_LOC)),
                  pl.BlockSpec(memory_space=pl.ANY)],
        out_specs=qspec,
        out_shape=jax.ShapeDtypeStruct((N_TOK, NA_WIDTH), BF16),
        input_output_aliases={6: 0},
        compiler_params=_cparams(2),
        name="na_lat",
    )(qn, knb, vb, kc, vc, btab, y_in)


def _na_bias_table(rpb):
    col = jnp.arange(GRID_W)
    col_start = jnp.clip(col - NA_WIN_COLS // 2, 0, GRID_W - NA_WIN_COLS)
    ok = (col[None, :] >= col_start[:, None]) & (col[None, :] < col_start[:, None] + NA_WIN_COLS)
    d_col = jnp.clip(col[None, :] - col[:, None], -(NA_WIN_COLS - 1), NA_WIN_COLS - 1) + (NA_WIN_COLS - 1)
    onehot = (d_col[:, :, None] == jnp.arange(2 * NA_WIN_COLS - 1)).astype(F32)
    t = jnp.einsum('hrd,cxd->hcrx', rpb, onehot, precision=HIGHEST)
    t = jnp.where(ok[None, :, None, :], t, NA_MASKED)
    b = jnp.stack([t[:, :, a:a + NA_WIN_ROWS, :] for a in range(NA_WIN_ROWS)])
    return b.reshape(NA_WIN_ROWS, NA_HEADS, GRID_W, NA_LOC)


LIN_HEADS = 4
LIN_DK = 64
LIN_DV = 128
LIN_QK = LIN_HEADS * LIN_DK
LIN_V = LIN_HEADS * LIN_DV


def _split2(x):
    hi = x.astype(BF16)
    lo = (x - hi.astype(F32)).astype(BF16)
    return hi, lo


def _split3(x):
    hi = x.astype(BF16)
    r = x - hi.astype(F32)
    mid = r.astype(BF16)
    lo = (r - mid.astype(F32)).astype(BF16)
    return hi, mid, lo


def _dot_exact_lhs(a_bf, x):
    hi, lo = _split2(x)
    return _dot(a_bf, hi) + _dot(a_bf, lo)


def _dot_exact_rhs(x, b_bf):
    hi, lo = _split2(x)
    return _dot(hi, b_bf) + _dot(lo, b_bf)


def _log_sigmoid(x):
    return jnp.minimum(x, 0.0) - jnp.log(1.0 + jnp.exp(-jnp.abs(x)))


def _lin_kernel(gqf_ref, gvf_ref, gqb_ref, gvb_ref, hqf_ref, hvf_ref, hqb_ref, hvb_ref,
                sgf_ref, sgb_ref, shf_ref, shb_ref, wa_ref, ba_ref, lbl_ref,
                ogf_ref, ogb_ref, ohf_ref, ohb_ref, ngf_ref, ngb_ref, nhf_ref, nhb_ref, st_ref):
    j = pl.program_id(0)
    _, blk, _ = _seq_of(j)
    is_prompt = j < PROMPT_BLKS
    s0_refs = (sgf_ref, sgb_ref, shf_ref, shb_ref)

    @pl.when(blk == 0)
    def _():
        for i in range(4):
            st_ref[i] = jnp.where(is_prompt, 0.0, s0_refs[i][0])

    L = LIN_L
    nch = RB // L
    qk_refs = ((gqf_ref, gqb_ref), (hqf_ref, hqb_ref))
    v_refs = ((gvf_ref, gvb_ref), (hvf_ref, hvb_ref))
    o_refs = ((ogf_ref, ogb_ref), (ohf_ref, ohb_ref))
    lane = lax.broadcasted_iota(jnp.int32, (1, LIN_QK), 1)
    head_mask = [(lane >> 6) == h for h in range(LIN_HEADS)]
    r4 = lax.broadcasted_iota(jnp.int32, (LIN_HEADS * L, L), 0) & (L - 1)
    c4 = lax.broadcasted_iota(jnp.int32, (LIN_HEADS * L, L), 1)
    tri4 = [c4 <= r4, c4 >= r4]
    eye = (lax.broadcasted_iota(jnp.int32, (LIN_QK, LIN_QK), 0)
           == lax.broadcasted_iota(jnp.int32, (LIN_QK, LIN_QK), 1))
    tri_bf = [jnp.where(_tri(L, bool(d)), 1.0, 0.0).astype(BF16) for d in range(2)]

    def lower_bound(d):
        l0 = lbl_ref[d, 0:1, :]
        l1 = lbl_ref[d, 1:2, :]
        mx = jnp.maximum(l0, l1)
        e0 = jnp.exp(l0 - mx)
        e1 = jnp.exp(l1 - mx)
        p0 = e0 / (e0 + e1)
        p1 = e1 / (e0 + e1)
        return (p0 + p1) - p0

    lbs = (lower_bound(0), lower_bound(1))
    units = [(m, d, c) for c in range(nch) for d in range(2) for m in range(2)]
    row0 = {u: ((nch - 1 - u[2]) * L if u[1] else u[2] * L) for u in units}

    def gates(u, _):
        m, d, _ = u
        r0 = row0[u]
        x_ref = qk_refs[m][d]
        if m == 0:
            q = x_ref[r0:r0 + L, 0:256] * (GLA_DK ** -0.5)
            k = x_ref[r0:r0 + L, 256:512]
            ga_hi, ga_lo = _split2(x_ref[r0:r0 + L, 512:640])
            wa_hi, wa_lo = _split2(wa_ref[d])
            x = _dot(ga_hi, wa_hi) + _dot(ga_lo, wa_hi) + _dot(ga_hi, wa_lo) + ba_ref[d]
            g = _log_sigmoid(x) / GLA_GATE_NORM
        else:
            q = x_ref[r0:r0 + L, 0:256]
            f = lbs[d] + (1.0 - lbs[d]) * _sigmoid(x_ref[r0:r0 + L, 256 * (d + 1):256 * (d + 2)])
            k = 1.0 - f
            g = jnp.log(f)
        return q, k, _dot_exact_lhs(tri_bf[d], g)

    def scale(u, p):
        m, d, _ = u
        q, k, cum = p
        last = cum[0:1, :] if d else cum[L - 1:L, :]
        q_in = q * jnp.exp(cum)
        k_out = (k * jnp.exp(-cum)).astype(BF16)
        k_end = (k * jnp.exp(last - cum)).astype(BF16)
        qs = jnp.concatenate([jnp.where(hm, q_in, 0.0) for hm in head_mask], axis=0).astype(BF16)
        dcol = jnp.sum(jnp.where(eye, jnp.exp(last), 0.0), axis=1, keepdims=True)
        v = v_refs[m][d][row0[u]:row0[u] + L, :].astype(BF16)
        return qs, k_out, k_end, dcol, v

    def products(u, p):
        qs, k_out, k_end, dcol, v = p
        return qs, _dot_nt(qs, k_out), _dot_tn(k_end, v), dcol, v

    def mask(u, p):
        qs, a, upd_full, dcol, v = p
        a = jnp.where(tri4[u[1]], a, 0.0).astype(BF16)
        upd = jnp.concatenate(
            [upd_full[LIN_DK * h:LIN_DK * (h + 1), LIN_DV * h:LIN_DV * (h + 1)] for h in range(LIN_HEADS)], axis=0)
        return qs, a, upd, dcol, v

    def intra(u, p):
        qs, a, upd, dcol, v = p
        o_intra = jnp.concatenate(
            [_dot(a[h * L:(h + 1) * L], v[:, LIN_DV * h:LIN_DV * (h + 1)]) for h in range(LIN_HEADS)], axis=0)
        return qs, o_intra, upd, dcol

    s = [st_ref[i] for i in range(4)]

    def outputs(u, p):
        m, d, _ = u
        qs, o_intra, upd, dcol = p
        i = 2 * m + d
        o = o_intra + _dot(qs, s[i].astype(BF16))
        o_refs[m][d][row0[u]:row0[u] + L, :] = jnp.concatenate(
            [o[h * L:(h + 1) * L] for h in range(LIN_HEADS)], axis=1).astype(BF16)
        s[i] = dcol * s[i] + upd

    _staged(units, (gates, scale, products, mask, intra, outputs))
    for i, n_ref in enumerate((ngf_ref, ngb_ref, nhf_ref, nhb_ref)):
        st_ref[i] = s[i]
        n_ref[0] = s[i]


def _lin(gqk, gv, hqf, hi, states, wa, ba, lbl):
    st_spec = pl.BlockSpec((1, LIN_QK, LIN_DV), lambda j: (_state_idx(j), 0, 0))
    so_spec = pl.BlockSpec((1, LIN_QK, LIN_DV), lambda j: (_seq_of(j)[0], 0, 0))
    so_shape = jax.ShapeDtypeStruct((N_SEQ, LIN_QK, LIN_DV), F32)
    o_shape = jax.ShapeDtypeStruct((N_TOK, LIN_V), BF16)
    outs = pl.pallas_call(
        _lin_kernel,
        grid=(N_BLK,),
        in_specs=[_fwd_spec(640), _fwd_spec(LIN_V), _bwd_spec(640), _bwd_spec(LIN_V),
                  _fwd_spec(768), _fwd_spec(LIN_V), _bwd_spec(768), _bwd_spec(LIN_V)]
                 + [st_spec] * 4 + [_const_spec(wa.shape), _const_spec(ba.shape), _const_spec(lbl.shape)],
        out_specs=[_fwd_spec(LIN_V), _bwd_spec(LIN_V), _fwd_spec(LIN_V), _bwd_spec(LIN_V)] + [so_spec] * 4,
        out_shape=[o_shape] * 4 + [so_shape] * 4,
        scratch_shapes=[pltpu.VMEM((4, LIN_QK, LIN_DV), F32)],
        compiler_params=_cparams(),
        name="lin",
    )(gqk, gv, gqk, gv, hqf, hi, hqf, hi, *[s.reshape(DEC_BATCH, LIN_QK, LIN_DV) for s in states], wa, ba, lbl)
    unpack = lambda s: s[:BATCH].reshape(BATCH, LIN_HEADS, LIN_DK, LIN_DV)
    return outs[:4], [unpack(s) for s in outs[4:]]


def _f32(ref):
    return ref[...].astype(F32)


def _outproj0_kernel(xp_ref, xs_ref, m_ref, of_ref, ob_ref, z_ref, yb_ref, nw_ref, w_ref, o_ref):
    ya = (_f32(of_ref) + _f32(ob_ref)) * _silu(_f32(z_ref))
    ms = jnp.mean(ya * ya, axis=-1, keepdims=True)
    ya = ya * lax.rsqrt(ms + NORM_EPS) * nw_ref[...]
    y = _dot(ya.astype(BF16), w_ref[0:512, :]) + _dot(yb_ref[...], w_ref[512:1024, :])
    o_ref[...] = _tile_of(xp_ref, xs_ref) + m_ref[0, 2:3, :] * y


def _outproj0(xp, xs, mrows, of, ob, z, yb, nw, w):
    return pl.pallas_call(
        _outproj0_kernel,
        grid=(N_TILES,),
        in_specs=[_prompt_tile_spec(), _sample_tile_spec(), _mod_spec(), _row_spec(512), _row_spec(512),
                  _row_spec(512), _row_spec(512), _const_spec((1, 512)), _const_spec((D_MODEL, D_MODEL))],
        out_specs=_row_spec(D_MODEL),
        out_shape=jax.ShapeDtypeStruct((N_TOK, D_MODEL), F32),
        compiler_params=_cparams(),
        name="outproj0",
    )(xp, xs, mrows, of, ob, z, yb, nw, w)


def _head_rms128(o, w):
    parts = []
    for h in range(LIN_HEADS):
        oh = o[:, LIN_DV * h:LIN_DV * (h + 1)]
        ms = jnp.mean(oh * oh, axis=-1, keepdims=True)
        parts.append(oh * lax.rsqrt(ms + NORM_EPS) * w)
    return jnp.concatenate(parts, axis=-1)


def _outproj1_kernel(x_ref, m_ref, gf_ref, gb_ref, gg_ref, hf_ref, hb_ref, hg_ref,
                     gw_ref, hw_ref, w_ref, o_ref):
    yc = _head_rms128(_f32(gf_ref) + _f32(gb_ref), gw_ref[...]) * _silu(_f32(gg_ref))
    yd = _head_rms128(_f32(hf_ref) + _f32(hb_ref), hw_ref[...]) * _silu(_f32(hg_ref))
    y = _dot(yc.astype(BF16), w_ref[0:512, :]) + _dot(yd.astype(BF16), w_ref[512:1024, :])
    o_ref[...] = x_ref[...] + m_ref[0, 2:3, :] * y


def _outproj1(x, mrows, gf, gb, gg, hf, hb, hg, gw, hw, w):
    return pl.pallas_call(
        _outproj1_kernel,
        grid=(N_TILES,),
        in_specs=[_row_spec(D_MODEL), _mod_spec()] + [_row_spec(512)] * 6
                 + [_const_spec((1, LIN_DV)), _const_spec((1, LIN_DV)), _const_spec((D_MODEL, D_MODEL))],
        out_specs=_row_spec(D_MODEL),
        out_shape=jax.ShapeDtypeStruct((N_TOK, D_MODEL), F32),
        compiler_params=_cparams(),
        name="outproj1",
    )(x, mrows, gf, gb, gg, hf, hb, hg, gw, hw, w)


HALO_PER_TILE = TM // SUBLANES


def _ffn_kernel(x_ref, xp_ref, xn_ref, m_ref, nw_ref, wu_ref, cw_ref, cb_ref, wd_ref, *rest, n_out):
    o_refs, (hs_ref, act_ref) = rest[:n_out], rest[n_out:]
    i = pl.program_id(0)
    nw = nw_ref[...]
    sh = m_ref[0, 3:4, :]
    sc = m_ref[0, 4:5, :]
    x = x_ref[...]
    hs_ref[0:TM, :] = _norm_mod(x, nw, sh, sc).astype(BF16)
    halo = _norm_mod(jnp.concatenate([xn_ref[...], xp_ref[...]], axis=0), nw, sh, sc)
    hs_ref[TM:TM_EXT, :] = (halo * _halo_keep(i)).astype(BF16)
    is_prompt = i < PROMPT_TILES

    def conv_cols(c0):
        cs = slice(c0, c0 + FFN_CH)
        return _dwconv_tile(_dot(hs_ref[...], wu_ref[:, cs]), cw_ref[:, cs], cb_ref[:, cs], i)

    for c in range(FFN_DIM // FFN_CH):
        a = conv_cols(c * FFN_CH)
        b = conv_cols(FFN_DIM + c * FFN_CH)
        act_ref[:, c * FFN_CH:(c + 1) * FFN_CH] = (_silu(a) * b).astype(BF16)
    out = x + m_ref[0, 5:6, :] * _dot(act_ref[...], wd_ref[...])
    if len(o_refs) == 1:
        o_refs[0][...] = out
    else:
        @pl.when(is_prompt)
        def _():
            o_refs[0][...] = out

        @pl.when(jnp.logical_not(is_prompt))
        def _():
            o_refs[1][...] = out


def _ffn(x, mrows, nw, wu, cw, cb, wd, split_out):
    prev = pl.BlockSpec((SUBLANES, D_MODEL), lambda i: (jnp.maximum(i * HALO_PER_TILE - 1, 0), 0))
    nxt = pl.BlockSpec((SUBLANES, D_MODEL),
                       lambda i: (jnp.minimum((i + 1) * HALO_PER_TILE, N_TOK // SUBLANES - 1), 0))
    single = dict(pipeline_mode=pl.Buffered(1))
    if split_out:
        out_specs = [_prompt_tile_spec(), _sample_tile_spec()]
        out_shape = [jax.ShapeDtypeStruct((N_PROMPT, D_MODEL), F32), jax.ShapeDtypeStruct((N_SAMPLE, D_MODEL), F32)]
    else:
        out_specs = [_row_spec(D_MODEL)]
        out_shape = [jax.ShapeDtypeStruct((N_TOK, D_MODEL), F32)]
    return pl.pallas_call(
        functools.partial(_ffn_kernel, n_out=len(out_specs)),
        grid=(N_TILES,),
        in_specs=[_row_spec(D_MODEL), prev, nxt, _mod_spec(), _const_spec((1, D_MODEL)),
                  pl.BlockSpec((D_MODEL, 2 * FFN_DIM), lambda i: (0, 0), **single),
                  _const_spec((3, 2 * FFN_DIM)), _const_spec((1, 2 * FFN_DIM)),
                  pl.BlockSpec((FFN_DIM, D_MODEL), lambda i: (0, 0), **single)],
        out_specs=out_specs,
        out_shape=out_shape,
        scratch_shapes=[pltpu.VMEM((TM_EXT, D_MODEL), BF16), pltpu.VMEM((TM, FFN_DIM), BF16)],
        compiler_params=_cparams(),
        name="ffn",
    )(x, x, x, mrows, nw, wu, cw, cb, wd)


def _pad_lanes(v, width=128):
    v = v.reshape(1, -1)
    return jnp.pad(v, ((0, 0), (0, width - v.shape[1])))


def kernel(x_prompt, x_sample, cache_na_k_l0, cache_na_v_l0, state_ssd_fwd_l0, state_ssd_bwd_l0,
           state_gla_fwd_l1, state_gla_bwd_l1, state_hgrn_fwd_l1, state_hgrn_bwd_l1, c,
           c_ctx, w_ada, b_ada, norm_w, ffn_w_up, ffn_conv_w, ffn_conv_b, ffn_w_down,
           w_in_l0, w_out_l0, ssd_conv_w_l0, ssd_conv_b_l0, ssd_dt_bias_l0, ssd_a_log_l0, ssd_d_l0,
           ssd_norm_w_l0, na_q_norm_l0, na_k_norm_l0, na_rpb_l0,
           w_in_l1, w_out_l1, gla_wa2_l1, gla_ba2_l1, gla_norm_w_l1, hgrn_lb_logits, hgrn_norm_w_l1):
    xp = x_prompt.reshape(N_PROMPT, D_MODEL)
    xs = x_sample.reshape(N_SAMPLE, D_MODEL)

    cvec8 = jnp.zeros((SUBLANES, D_MODEL), F32).at[0:DEC_BATCH].set(c).at[CTX_MOD_ROW].set(c_ctx)
    mods = _mods(cvec8, w_ada, b_ada)
    mods = mods.reshape(2, SUBLANES, 6, D_MODEL)
    mods = jnp.pad(mods, ((0, 0), (0, 0), (0, SUBLANES - 6), (0, 0)))

    zpad = lambda n: jnp.zeros((D_MODEL, n), F32)
    w0 = jnp.concatenate([w_in_l0[:, :1296], zpad(112), w_in_l0[:, 1296:]], axis=1).astype(BF16)
    w1 = jnp.concatenate([w_in_l1[:, 0:512], w_in_l1[:, 1536:1568], zpad(96),
                          w_in_l1[:, 512:1536], w_in_l1[:, 1568:3360]], axis=1).astype(BF16)
    seg = jnp.kron(jnp.eye(NA_HEADS, dtype=F32), jnp.full((NA_HEAD_DIM, NA_HEAD_DIM), 1.0 / NA_HEAD_DIM, F32)).astype(BF16)

    m0 = mods[0]
    z, xc, dt, qn, knb, vb, kn, v = _inproj0(
        xp, xs, m0, norm_w[0, 0].reshape(1, D_MODEL), w0, seg,
        jnp.tile(na_q_norm_l0, NA_HEADS).reshape(1, NA_WIDTH), jnp.tile(na_k_norm_l0, NA_HEADS).reshape(1, NA_WIDTH),
        ssd_conv_w_l0, ssd_conv_b_l0.reshape(1, SSD_XBC))
    of, ob, ssd_f, ssd_b = _ssd(
        xc, dt, state_ssd_fwd_l0, state_ssd_bwd_l0, _pad_lanes(ssd_dt_bias_l0), _pad_lanes(ssd_a_log_l0),
        jnp.repeat(ssd_d_l0, SSD_HEAD_DIM).reshape(1, SSD_WIDTH))
    yb, na_k, na_v = _na_ctx(qn, kn, v)
    yb = _na_lat(qn, knb, vb, cache_na_k_l0, cache_na_v_l0, _na_bias_table(na_rpb_l0), yb)
    x = _outproj0(xp, xs, m0, of, ob, z, yb, ssd_norm_w_l0.reshape(1, SSD_WIDTH), w_out_l0.astype(BF16))
    x, = _ffn(x, m0, norm_w[0, 1].reshape(1, D_MODEL), ffn_w_up[0].astype(BF16), ffn_conv_w[0],
              ffn_conv_b[0].reshape(1, 2 * FFN_DIM), ffn_w_down[0].astype(BF16), split_out=False)

    m1 = mods[1]
    gqk, gv, gg, hqf, hi, hg = _inproj1(x, m1, norm_w[1, 0].reshape(1, D_MODEL), w1)
    wa_pad = jnp.zeros((2, 128, LIN_QK), F32)
    wa_pad = wa_pad.at[0, 0:GLA_RANK].set(gla_wa2_l1[0]).at[1, GLA_RANK:2 * GLA_RANK].set(gla_wa2_l1[1])
    (gf, gb, hf, hb), (gla_f, gla_b, hgrn_f, hgrn_b) = _lin(
        gqk, gv, hqf, hi, (state_gla_fwd_l1, state_gla_bwd_l1, state_hgrn_fwd_l1, state_hgrn_bwd_l1),
        wa_pad, gla_ba2_l1.reshape(2, 1, LIN_QK), hgrn_lb_logits)
    x = _outproj1(x, m1, gf, gb, gg, hf, hb, hg, gla_norm_w_l1.reshape(1, LIN_DV),
                  hgrn_norm_w_l1.reshape(1, LIN_DV), w_out_l1.astype(BF16))
    y_p, y_s = _ffn(x, m1, norm_w[1, 1].reshape(1, D_MODEL), ffn_w_up[1].astype(BF16), ffn_conv_w[1],
                    ffn_conv_b[1].reshape(1, 2 * FFN_DIM), ffn_w_down[1].astype(BF16), split_out=True)
    return (y_p.reshape(BATCH, SEQ, D_MODEL), y_s.reshape(DEC_BATCH, DEC_SEQ, D_MODEL),
            na_k, na_v, ssd_f, ssd_b, gla_f, gla_b, hgrn_f, hgrn_b)
```

```python
import functools

import jax
import jax.numpy as jnp
from jax import lax
from jax.experimental import pallas as pl
from jax.experimental.pallas import tpu as pltpu

F32 = jnp.float32
BF16 = jnp.bfloat16
HIGHEST = lax.Precision.HIGHEST

D_MODEL = 1024
BATCH = 32
SEQ = 256
DEC_BATCH = 4
DEC_SEQ = 4096
PAST_LEN = 256
GRID_W = 64
NORM_EPS = 1e-6
N_PROMPT = BATCH * SEQ
N_SAMPLE = DEC_BATCH * DEC_SEQ
N_TOK = N_PROMPT + N_SAMPLE

SSD_HEADS = 8
SSD_HEAD_DIM = 64
SSD_STATE = 64
SSD_WIDTH = 512
SSD_BC = 128
SSD_XBC = 768
NA_HEADS = 8
NA_HEAD_DIM = 64
NA_WIDTH = 512
NA_WIN_ROWS = 8
NA_WIN_COLS = 16
NA_SCALE = NA_HEAD_DIM ** -0.5
GLA_HEADS = 4
GLA_DK = 64
GLA_DV = 128
GLA_RANK = 16
GLA_GATE_NORM = 16.0
HGRN_HEADS = 4
FFN_DIM = 2816

V7X_VMEM_BYTES = 64 * 1024 * 1024
VMEM_LIMIT = 56 * 1024 * 1024
SUBLANES = 8

TM = 512
N_TILES = N_TOK // TM
PROMPT_TILES = N_PROMPT // TM
TILES_PER_SAMPLE = DEC_SEQ // TM
CTX_MOD_ROW = DEC_BATCH

RB = 256
N_BLK = N_TOK // RB
PROMPT_BLKS = N_PROMPT // RB
BLKS_PER_SAMPLE = DEC_SEQ // RB
N_SEQ = BATCH + DEC_BATCH
SSD_L = 128
LIN_L = 64
LIN_SAFE_LOG_DECAY = 60.0
NA_R = 8
NA_ROWS_PER_ITER = 4
FFN_CH = 256
W0_COLS = 2944
W1_COLS = 3456


def _cparams(n_axes=1):
    return pltpu.CompilerParams(dimension_semantics=("arbitrary",) * n_axes,
                                vmem_limit_bytes=VMEM_LIMIT)


def _const_spec(shape):
    nd = len(shape)
    return pl.BlockSpec(shape, lambda *_: (0,) * nd)


def _sigmoid(x):
    return 1.0 / (1.0 + jnp.exp(-x))


def _silu(x):
    return x * _sigmoid(x)


def _softplus(x):
    return jnp.maximum(x, 0.0) + jnp.log(1.0 + jnp.exp(-jnp.abs(x)))


def _mod_row(i):
    return jnp.where(i < PROMPT_TILES, CTX_MOD_ROW, (i - PROMPT_TILES) // TILES_PER_SAMPLE)


def _dot(a, b, **kw):
    return jnp.dot(a, b, preferred_element_type=F32, **kw)


def _dot_nt(a, b):
    return lax.dot_general(a, b, (((1,), (1,)), ((), ())), preferred_element_type=F32)


def _split2(x):
    hi = x.astype(BF16)
    lo = (x - hi.astype(F32)).astype(BF16)
    return hi, lo


def _split3(x):
    hi = x.astype(BF16)
    r = x - hi.astype(F32)
    mid = r.astype(BF16)
    lo = (r - mid.astype(F32)).astype(BF16)
    return hi, mid, lo


def _dot_exact_lhs(a_bf, x):
    hi, lo = _split2(x)
    return _dot(a_bf, hi) + _dot(a_bf, lo)


def _dot_exact_rhs(x, b_bf):
    hi, lo = _split2(x)
    return _dot(hi, b_bf) + _dot(lo, b_bf)


def _skewed(units, stages):
    results = {}
    for step in range(len(units) + len(stages) - 1):
        for k, stage in enumerate(stages):
            i = step - k
            if 0 <= i < len(units):
                results[(k, i)] = stage(units[i], results.pop((k - 1, i), None))


def _staged(units, stages):
    results = {i: None for i in range(len(units))}
    for stage in stages:
        for i, u in enumerate(units):
            results[i] = stage(u, results[i])


def _dot_tn(a, b):
    return lax.dot_general(a, b, (((0,), (0,)), ((), ())), preferred_element_type=F32)


MODS_NB = 1536


def _mods_kernel(c_ref, w_ref, b_ref, o_ref):
    s_hi, s_lo = _split2(_silu(c_ref[...]))
    w_hi, w_lo = _split2(w_ref[0])
    o_ref[0] = _dot(s_hi, w_hi) + _dot(s_lo, w_hi) + _dot(s_hi, w_lo) + b_ref[0]


def _mods(cvec8, w_ada, b_ada):
    depth = w_ada.shape[0]
    nb = 6 * D_MODEL // MODS_NB
    return pl.pallas_call(
        _mods_kernel,
        grid=(depth, nb),
        in_specs=[
            _const_spec((SUBLANES, D_MODEL)),
            pl.BlockSpec((1, D_MODEL, MODS_NB), lambda l, j: (l, 0, j)),
            pl.BlockSpec((1, 1, MODS_NB), lambda l, j: (l, 0, j)),
        ],
        out_specs=pl.BlockSpec((1, SUBLANES, MODS_NB), lambda l, j: (l, 0, j)),
        out_shape=jax.ShapeDtypeStruct((depth, SUBLANES, 6 * D_MODEL), F32),
        compiler_params=_cparams(2),
        name="mods",
    )(cvec8, w_ada, b_ada.reshape(depth, 1, 6 * D_MODEL))


def _norm_mod(x, nw, sh, sc):
    ms = jnp.mean(x * x, axis=-1, keepdims=True)
    y = x * lax.rsqrt(ms + NORM_EPS) * nw
    return y * (1.0 + sc) + sh


def _head_rms(x, seg, w):
    ms = _dot((x * x).astype(BF16), seg)
    return x * lax.rsqrt(ms + NORM_EPS) * w


def _tile_of(xp_ref, xs_ref):
    return jnp.where(pl.program_id(0) < PROMPT_TILES, xp_ref[...], xs_ref[...])


HALO_ROWS = 2 * SUBLANES
TM_EXT = TM + HALO_ROWS


def _halo_keep(i):
    seq = jnp.where(i < PROMPT_TILES, SEQ, DEC_SEQ)
    keep_prev = jnp.where(((i * TM) & (seq - 1)) == 0, 0.0, 1.0)
    keep_next = jnp.where(((i * TM + TM) & (seq - 1)) == 0, 0.0, 1.0)
    hrow = lax.broadcasted_iota(jnp.int32, (HALO_ROWS, 1), 0)
    return jnp.where(hrow < SUBLANES, keep_next, keep_prev)


def _dwconv_tile(u_all, w, bias, i):
    um = pltpu.roll(u_all, 1, axis=0)[0:TM] * w[0:1]
    up = pltpu.roll(u_all, TM_EXT - 1, axis=0)[0:TM] * w[2:3]
    out = bias + um + u_all[0:TM] * w[1:2] + up
    edge_on = jnp.where(i < PROMPT_TILES, 1.0, 0.0)
    erow = lax.broadcasted_iota(jnp.int32, (HALO_ROWS, 1), 0)
    at_first = jnp.where(erow == SUBLANES, edge_on, 0.0)
    at_last = jnp.where(erow == SUBLANES - 1, edge_on, 0.0)
    pieces, r = [], 0
    for edge in range(SEQ, TM, SEQ):
        sl = slice(edge - SUBLANES, edge + SUBLANES)
        pieces += [out[r:edge - SUBLANES], out[sl] - at_first * um[sl] - at_last * up[sl]]
        r = edge + SUBLANES
    return jnp.concatenate(pieces + [out[r:TM]], axis=0)


def _inproj0_kernel(xp_ref, xs_ref, xn_ref, xv_ref, m_ref, nw_ref, w_ref, seg_ref, qw_ref, kw_ref, cw_ref, cb_ref,
                    z_ref, xc_ref, dt_ref, qn_ref, knb_ref, vb_ref, kn_ref, v_ref, hs_ref):
    i = pl.program_id(0)
    nw, sh, sc = nw_ref[...], m_ref[0, 0:1, :], m_ref[0, 1:2, :]
    hs_ref[0:TM, :] = _norm_mod(_tile_of(xp_ref, xs_ref), nw, sh, sc).astype(BF16)
    halo = _norm_mod(jnp.concatenate([xn_ref[...], xv_ref[...]], axis=0), nw, sh, sc)
    hs_ref[TM:TM_EXT, :] = (halo * _halo_keep(i)).astype(BF16)
    z_ref[...] = _dot(hs_ref[0:TM, :], w_ref[:, 0:512]).astype(BF16)
    xc_ref[...] = _silu(_dwconv_tile(_dot(hs_ref[...], w_ref[:, 512:1280]), cw_ref[...], cb_ref[...], i))
    dt_ref[...] = _dot(hs_ref[0:TM, :], w_ref[:, 1280:1408])
    q = _dot(hs_ref[0:TM, :], w_ref[:, 1408:1920])
    k = _dot(hs_ref[0:TM, :], w_ref[:, 1920:2432])
    v = _dot(hs_ref[0:TM, :], w_ref[:, 2432:2944])
    vb_ref[...] = v.astype(BF16)
    ms_q = _dot((q * q).astype(BF16), seg_ref[...])
    ms_k = _dot((k * k).astype(BF16), seg_ref[...])
    qn_ref[...] = (q * lax.rsqrt(ms_q + NORM_EPS) * qw_ref[...] * NA_SCALE).astype(BF16)
    kn = k * lax.rsqrt(ms_k + NORM_EPS) * kw_ref[...]
    knb_ref[...] = kn.astype(BF16)

    @pl.when(pl.program_id(0) < PROMPT_TILES)
    def _():
        kn_ref[...] = kn
        v_ref[...] = v


def _row_spec(width):
    return pl.BlockSpec((TM, width), lambda i: (i, 0))


def _mod_spec():
    return pl.BlockSpec((1, SUBLANES, D_MODEL), lambda i: (_mod_row(i), 0, 0))


def _prompt_tile_spec(width=D_MODEL):
    return pl.BlockSpec((TM, width), lambda i: (jnp.minimum(i, PROMPT_TILES - 1), 0))


def _sample_tile_spec(width=D_MODEL):
    return pl.BlockSpec((TM, width), lambda i: (jnp.maximum(i - PROMPT_TILES, 0), 0))


def _inproj0(xp, xs, mrows, nw, w0, seg, qw, kw, conv_w, conv_b):
    widths = (512, 768, 128, 512, 512, 512)
    dtypes = (BF16, F32, F32, BF16, BF16, BF16)
    halo_per_tile = TM // SUBLANES
    last_halo = N_SAMPLE // SUBLANES - 1
    nxt = pl.BlockSpec((SUBLANES, D_MODEL),
                       lambda i: (jnp.clip((i - PROMPT_TILES + 1) * halo_per_tile, 0, last_halo), 0))
    prv = pl.BlockSpec((SUBLANES, D_MODEL),
                       lambda i: (jnp.clip((i - PROMPT_TILES) * halo_per_tile - 1, 0, last_halo), 0))
    return pl.pallas_call(
        _inproj0_kernel,
        grid=(N_TILES,),
        in_specs=[
            _prompt_tile_spec(), _sample_tile_spec(), nxt, prv, _mod_spec(), _const_spec((1, D_MODEL)),
            _const_spec((D_MODEL, W0_COLS)), _const_spec((512, 512)),
            _const_spec((1, 512)), _const_spec((1, 512)), _const_spec((3, SSD_XBC)), _const_spec((1, SSD_XBC)),
        ],
        out_specs=[_row_spec(w) for w in widths] + [_prompt_tile_spec(NA_WIDTH)] * 2,
        out_shape=[jax.ShapeDtypeStruct((N_TOK, w), d) for w, d in zip(widths, dtypes)]
                  + [jax.ShapeDtypeStruct((N_PROMPT, NA_WIDTH), F32)] * 2,
        scratch_shapes=[pltpu.VMEM((TM_EXT, D_MODEL), BF16)],
        compiler_params=_cparams(),
        name="inproj0",
    )(xp, xs, xs, xs, mrows, nw, w0, seg, qw, kw, conv_w, conv_b)


def _inproj1_kernel(x_ref, m_ref, nw_ref, w_ref, gqk_ref, gv_ref, gg_ref, hqf_ref, hi_ref, hg_ref, hs_ref):
    hs_ref[...] = _norm_mod(x_ref[...], nw_ref[...], m_ref[0, 0:1, :], m_ref[0, 1:2, :]).astype(BF16)
    gqk_ref[...] = _dot(hs_ref[...], w_ref[:, 0:640])
    gv_ref[...] = _dot(hs_ref[...], w_ref[:, 640:1152]).astype(BF16)
    gg_ref[...] = _dot(hs_ref[...], w_ref[:, 1152:1664]).astype(BF16)
    hqf_ref[...] = _dot(hs_ref[...], w_ref[:, 1664:2432])
    hi_ref[...] = _dot(hs_ref[...], w_ref[:, 2432:2944]).astype(BF16)
    hg_ref[...] = _dot(hs_ref[...], w_ref[:, 2944:3456]).astype(BF16)


def _inproj1(x, mrows, nw, w1):
    widths = (640, 512, 512, 768, 512, 512)
    dtypes = (F32, BF16, BF16, F32, BF16, BF16)
    return pl.pallas_call(
        _inproj1_kernel,
        grid=(N_TILES,),
        in_specs=[_row_spec(D_MODEL), _mod_spec(), _const_spec((1, D_MODEL)),
                  _const_spec((D_MODEL, W1_COLS))],
        out_specs=[_row_spec(w) for w in widths],
        out_shape=[jax.ShapeDtypeStruct((N_TOK, w), d) for w, d in zip(widths, dtypes)],
        scratch_shapes=[pltpu.VMEM((TM, D_MODEL), BF16)],
        compiler_params=_cparams(),
        name="inproj1",
    )(x, mrows, nw, w1)


def _seq_of(j):
    jj = j - PROMPT_BLKS
    is_p = j < PROMPT_BLKS
    seq = jnp.where(is_p, j, BATCH + jj // BLKS_PER_SAMPLE)
    blk = jnp.where(is_p, 0, jj % BLKS_PER_SAMPLE)
    nblk = jnp.where(is_p, 1, BLKS_PER_SAMPLE)
    return seq, blk, nblk


def _bwd_blk(j):
    _, blk, nblk = _seq_of(j)
    return j - blk + (nblk - 1 - blk)


def _state_idx(j):
    seq, _, _ = _seq_of(j)
    return jnp.maximum(seq - BATCH, 0)


def _fwd_spec(width):
    return pl.BlockSpec((RB, width), lambda j: (j, 0))


def _bwd_spec(width):
    return pl.BlockSpec((RB, width), lambda j: (_bwd_blk(j), 0))


def _tri(n, upper):
    r = lax.broadcasted_iota(jnp.int32, (n, n), 0)
    c = lax.broadcasted_iota(jnp.int32, (n, n), 1)
    return (c >= r) if upper else (c <= r)


def _diag_blocks(x, rows):
    lh = lax.broadcasted_iota(jnp.int32, (1, 4 * SSD_HEAD_DIM), 1) >> 6
    out = x[3 * rows:4 * rows]
    for h in (2, 1, 0):
        out = jnp.where(lh == h, x[h * rows:(h + 1) * rows], out)
    return out


def _ssd_kernel(xf_ref, dtf_ref, xb_ref, dtb_ref, s0f_ref, s0b_ref, dtbias_ref, alog_ref, dsk_ref, ex_ref,
                of_ref, ob_ref, sf_ref, sb_ref, st_ref):
    j = pl.program_id(0)
    _, blk, _ = _seq_of(j)
    is_prompt = j < PROMPT_BLKS

    @pl.when(blk == 0)
    def _():
        st_ref[0] = jnp.where(is_prompt, 0.0, s0f_ref[0])
        st_ref[1] = jnp.where(is_prompt, 0.0, s0b_ref[0])

    a_neg = -jnp.exp(alog_ref[...])
    a_col = jnp.broadcast_to(a_neg, (SUBLANES, 128)).T
    L = SSD_L
    nch = RB // L
    xc_refs = (xf_ref, xb_ref)
    dt_refs = (dtf_ref, dtb_ref)
    o_refs = (of_ref, ob_ref)

    units = [(d, c) for c in range(nch) for d in range(2)]
    row0 = {(d, c): ((nch - 1 - c) * L if d else c * L) for d, c in units}
    tri = [_tri(L, False), _tri(L, True)]
    tri_bf = [jnp.where(t, 1.0, 0.0).astype(BF16) for t in tri]
    lane = lax.broadcasted_iota(jnp.int32, (1, SSD_BC), 1)
    gmask = [lane < SSD_STATE, lane >= SSD_STATE]
    eye_bf = jnp.where(tri[0] & tri[1], 1.0, 0.0).astype(BF16)
    zero_blk = jnp.zeros((SSD_STATE, 4 * SSD_HEAD_DIM), F32)

    def gates(u, _):
        d, r0 = u[0], row0[u]
        dt = _softplus(dt_refs[d][r0:r0 + L, :] + dtbias_ref[...])
        cum = _dot_exact_lhs(tri_bf[d], dt * a_neg)
        dt8 = dt.T[8 * d:8 * d + 8]
        ct8 = _dot_exact_rhs(dt8 * a_col[8 * d:8 * d + 8, 0:1], tri_bf[1 - d])
        parts = _split3(cum[0:SUBLANES] if d else cum[L - SUBLANES:L])
        last_x = _dot(parts[0], ex_ref[d]) + _dot(parts[1], ex_ref[d]) + _dot(parts[2], ex_ref[d])
        e_last_x = jnp.exp(last_x[0:1] if d else last_x[SUBLANES - 1:SUBLANES])
        last_col = ct8[:, 0:1] if d else ct8[:, L - 1:L]
        f_end = jnp.exp(last_col - ct8) * dt8
        bm = xc_refs[d][r0:r0 + L, 512:640].astype(BF16)
        cm = xc_refs[d][r0:r0 + L, 640:768]
        b_t = _dot_nt(eye_bf, bm)
        cb_g = [_dot_nt(jnp.where(gmask[g], cm, 0.0).astype(BF16), bm) for g in range(2)]
        return cum, ct8, dt8, e_last_x, f_end, b_t, cb_g, cm

    def operands(u, p):
        d = u[0]
        cum, ct8, dt8, e_last_x, f_end, b_t, cb_g, cm = p
        lhs, k_t = [], []
        for g in range(2):
            ls, ks = [], []
            for hh in range(4):
                h = 4 * g + hh
                pc = jnp.broadcast_to(cum[:, 8 * d + h:8 * d + h + 1], (L, L))
                dec = jnp.exp(jnp.where(tri[d], pc - ct8[h:h + 1, :], -jnp.inf))
                w = cb_g[g] * dec * dt8[h:h + 1, :]
                ls.append(jnp.concatenate([w.astype(BF16), (cm * jnp.exp(pc)).astype(BF16)], axis=1))
                ks.append((b_t[SSD_STATE * g:SSD_STATE * (g + 1)] * f_end[h:h + 1, :]).astype(BF16))
            lhs.append(jnp.concatenate(ls, axis=0))
            k_t.append(jnp.concatenate(ks, axis=0))
        return lhs, k_t, e_last_x

    def state_update(u, p):
        d, r0 = u[0], row0[u]
        lhs, k_t, e_last_x = p
        x_bf = [xc_refs[d][r0:r0 + L, 256 * g:256 * (g + 1)].astype(BF16) for g in range(2)]
        up = [_diag_blocks(_dot(k_t[g], x_bf[g]), SSD_STATE) for g in range(2)]
        upd = jnp.concatenate([jnp.concatenate([up[0], zero_blk], axis=1),
                               jnp.concatenate([zero_blk, up[1]], axis=1)], axis=0)
        return lhs, x_bf, e_last_x, upd

    s = [st_ref[0], st_ref[1]]

    def outputs(u, p):
        d, r0 = u[0], row0[u]
        lhs, x_bf, e_last_x, upd = p
        s_bf = s[d].astype(BF16)
        o = jnp.concatenate(
            [_diag_blocks(_dot(lhs[g], jnp.concatenate([x_bf[g], s_bf[:, 256 * g:256 * (g + 1)]], axis=0)), L)
             for g in range(2)], axis=1)
        if d == 0:
            o = o + dsk_ref[...] * xf_ref[r0:r0 + L, 0:SSD_WIDTH]
        o_refs[d][r0:r0 + L, :] = o.astype(BF16)
        s[d] = e_last_x * s[d] + upd

    _skewed(units, (gates, operands, state_update, outputs))
    st_ref[0] = s[0]
    st_ref[1] = s[1]
    sf_ref[0] = s[0]
    sb_ref[0] = s[1]


def _ssd_pack_state(s):
    b = s.shape[0]
    g = s.reshape(b, 2, 4, SSD_STATE, SSD_HEAD_DIM).transpose(0, 1, 3, 2, 4).reshape(b, 2, SSD_STATE, 256)
    z = jnp.zeros((b, SSD_STATE, 256), F32)
    return jnp.concatenate([jnp.concatenate([g[:, 0], z], axis=2), jnp.concatenate([z, g[:, 1]], axis=2)], axis=1)


def _ssd_unpack_state(s):
    b = s.shape[0]
    g = jnp.stack([s[:, 0:SSD_STATE, 0:256], s[:, SSD_STATE:, 256:512]], axis=1)
    g = g.reshape(b, 2, SSD_STATE, 4, SSD_HEAD_DIM).transpose(0, 1, 3, 2, 4)
    return g.reshape(b, SSD_HEADS, SSD_STATE, SSD_HEAD_DIM)


def _ssd(xc, dt, s0f, s0b, dtbias, alog, dsk):
    st_shape = (1, 2 * SSD_STATE, SSD_WIDTH)
    st_spec = pl.BlockSpec(st_shape, lambda j: (_state_idx(j), 0, 0))
    so_spec = pl.BlockSpec(st_shape, lambda j: (_seq_of(j)[0], 0, 0))
    so_shape = jax.ShapeDtypeStruct((N_SEQ, 2 * SSD_STATE, SSD_WIDTH), F32)
    col = jnp.arange(128)[:, None]
    lane_head = jnp.arange(SSD_WIDTH)[None, :] // SSD_HEAD_DIM
    expand = jnp.stack([col == lane_head, col == SSD_HEADS + lane_head]).astype(BF16)
    of, ob, sf, sb = pl.pallas_call(
        _ssd_kernel,
        grid=(N_BLK,),
        in_specs=[
            _fwd_spec(SSD_XBC), _fwd_spec(128), _bwd_spec(SSD_XBC), _bwd_spec(128), st_spec, st_spec,
            _const_spec((1, 128)), _const_spec((1, 128)), _const_spec((1, SSD_WIDTH)),
            _const_spec((2, 128, SSD_WIDTH)),
        ],
        out_specs=[_fwd_spec(SSD_WIDTH), _bwd_spec(SSD_WIDTH), so_spec, so_spec],
        out_shape=[jax.ShapeDtypeStruct((N_TOK, SSD_WIDTH), BF16)] * 2 + [so_shape, so_shape],
        scratch_shapes=[pltpu.VMEM((2, 2 * SSD_STATE, SSD_WIDTH), F32)],
        compiler_params=_cparams(),
        name="ssd",
    )(xc, dt, xc, dt, _ssd_pack_state(s0f), _ssd_pack_state(s0b), dtbias, alog, dsk, expand)
    return of, ob, _ssd_unpack_state(sf[:BATCH]), _ssd_unpack_state(sb[:BATCH])


NA_PAIRS = NA_HEADS // 2


def _stack_pair(qt):
    lower = lax.broadcasted_iota(jnp.int32, (1, 2 * NA_HEAD_DIM), 1) < NA_HEAD_DIM
    zero = jnp.zeros_like(qt)
    return jnp.concatenate([jnp.where(lower, qt, zero), jnp.where(lower, zero, qt)], axis=0)


def _unstack_pair(x, n):
    lower = lax.broadcasted_iota(jnp.int32, (1, 2 * NA_HEAD_DIM), 1) < NA_HEAD_DIM
    return jnp.where(lower, x[0:n], x[n:2 * n])


def _na_ctx_kernel(q_ref, k_ref, v_ref, y_ref, ko_ref, vo_ref):
    tiles = [slice(128 * i, 128 * (i + 1)) for i in range(NA_PAIRS)]
    s = [_dot_nt(_stack_pair(q_ref[:, ts]), k_ref[:, ts].astype(BF16)) for ts in tiles]
    p, l = [], []
    for i in range(NA_PAIRS):
        e = jnp.exp(s[i] - jnp.max(s[i], axis=-1, keepdims=True))
        l.append(jnp.sum(e, axis=-1, keepdims=True))
        p.append(e.astype(BF16))
    o = [_dot(p[i], v_ref[:, tiles[i]].astype(BF16)) for i in range(NA_PAIRS)]
    for i in range(NA_PAIRS):
        y = _unstack_pair(o[i], SEQ) / _unstack_pair(jnp.broadcast_to(l[i], o[i].shape), SEQ)
        y_ref[:, tiles[i]] = y.astype(BF16)
    for h in range(NA_HEADS):
        sl = slice(NA_HEAD_DIM * h, NA_HEAD_DIM * (h + 1))
        ko_ref[0, h] = k_ref[:, sl]
        vo_ref[0, h] = v_ref[:, sl]


def _na_ctx(qn, kn, v):
    blk = lambda w: pl.BlockSpec((SEQ, w), lambda b: (b, 0))
    hm = pl.BlockSpec((1, NA_HEADS, SEQ, NA_HEAD_DIM), lambda b: (b, 0, 0, 0))
    hm_shape = jax.ShapeDtypeStruct((BATCH, NA_HEADS, SEQ, NA_HEAD_DIM), F32)
    return pl.pallas_call(
        _na_ctx_kernel,
        grid=(BATCH,),
        in_specs=[blk(NA_WIDTH)] * 3,
        out_specs=[blk(NA_WIDTH), hm, hm],
        out_shape=[jax.ShapeDtypeStruct((N_TOK, NA_WIDTH), BF16), hm_shape, hm_shape],
        compiler_params=_cparams(),
        name="na_ctx",
    )(qn, kn, v)


GRID_ROWS = DEC_SEQ // GRID_W
NA_LOC = NA_WIN_ROWS * GRID_W
NA_MASKED = -1e30


def _na_lat_kernel(q_ref, k_ref, v_ref, kc_ref, vc_ref, bt_ref, yin_ref, y_ref):
    del yin_ref
    rb = pl.program_id(1)

    tiles = [slice(128 * i, 128 * (i + 1)) for i in range(NA_PAIRS)]

    def rows(it, carry):
        units, q0, k0, dr0 = [], {}, {}, {}
        for rr_ in range(NA_ROWS_PER_ITER):
            r = it * NA_ROWS_PER_ITER + rr_
            rr = rb * NA_R + r
            rs = jnp.clip(rr - NA_WIN_ROWS // 2, 0, GRID_ROWS - NA_WIN_ROWS)
            dr0[rr_] = rs - rr + (NA_WIN_ROWS - 1)
            q0[rr_] = pl.multiple_of(r * GRID_W, GRID_W)
            k0[rr_] = pl.multiple_of(rs * GRID_W, GRID_W)
            units += [(rr_, i) for i in range(NA_PAIRS)]
        def scores(u, _):
            r, i = u
            qq = _stack_pair(q_ref[pl.ds(q0[r], GRID_W), tiles[i]])
            s_loc = (_dot_nt(qq, k_ref[pl.ds(k0[r], NA_LOC), tiles[i]])
                     + bt_ref[dr0[r], 2 * i:2 * i + 2].reshape(2 * GRID_W, NA_LOC))
            return s_loc, _dot_nt(qq, kc_ref[0, :, tiles[i]])

        def softmax(u, s):
            s_loc, s_ctx = s
            m = jnp.maximum(jnp.max(s_loc, axis=-1, keepdims=True), jnp.max(s_ctx, axis=-1, keepdims=True))
            e_loc = jnp.exp(s_loc - m)
            e_ctx = jnp.exp(s_ctx - m)
            l = jnp.sum(e_loc, axis=-1, keepdims=True) + jnp.sum(e_ctx, axis=-1, keepdims=True)
            return e_loc.astype(BF16), e_ctx.astype(BF16), l

        def weighted(u, p):
            r, i = u
            p_loc, p_ctx, l = p
            o = _dot(p_loc, v_ref[pl.ds(k0[r], NA_LOC), tiles[i]]) + _dot(p_ctx, vc_ref[0, :, tiles[i]])
            y = _unstack_pair(o, GRID_W) / _unstack_pair(jnp.broadcast_to(l, o.shape), GRID_W)
            y_ref[pl.ds(q0[r], GRID_W), tiles[i]] = y.astype(BF16)

        _skewed(units, (scores, softmax, weighted))
        return carry

    lax.fori_loop(0, NA_R // NA_ROWS_PER_ITER, rows, 0)


def _na_lat(qn, knb, vb, kc, vc, btab, y_in):
    rows_per_step = NA_R * GRID_W
    steps = GRID_ROWS // NA_R
    off_q = N_PROMPT // rows_per_step
    off_s = N_PROMPT // DEC_SEQ
    qspec = pl.BlockSpec((rows_per_step, NA_WIDTH), lambda b, r: (off_q + b * steps + r, 0))
    kvspec = pl.BlockSpec((DEC_SEQ, NA_WIDTH), lambda b, r: (off_s + b, 0))
    cspec = pl.BlockSpec((1, PAST_LEN, NA_WIDTH), lambda b, r: (b, 0, 0))
    token_major = lambda a: a.transpose(0, 2, 1, 3).reshape(DEC_BATCH, PAST_LEN, NA_WIDTH).astype(BF16)
    kc, vc = token_major(kc), token_major(vc)
    return pl.pallas_call(
        _na_lat_kernel,
        grid=(DEC_BATCH, steps),
        in_specs=[qspec, kvspec, kvspec, cspec, cspec,
                  _const_spec((NA_WIN_ROWS, NA_HEADS, GRID_W, NA_LOC)),
                  pl.BlockSpec(memory_space=pl.ANY)],
        out_specs=qspec,
        out_shape=jax.ShapeDtypeStruct((N_TOK, NA_WIDTH), BF16),
        input_output_aliases={6: 0},
        compiler_params=_cparams(2),
        name="na_lat",
    )(qn, knb, vb, kc, vc, btab, y_in)


def _na_bias_table(rpb):
    col = jnp.arange(GRID_W)
    col_start = jnp.clip(col - NA_WIN_COLS // 2, 0, GRID_W - NA_WIN_COLS)
    ok = (col[None, :] >= col_start[:, None]) & (col[None, :] < col_start[:, None] + NA_WIN_COLS)
    d_col = jnp.clip(col[None, :] - col[:, None], -(NA_WIN_COLS - 1), NA_WIN_COLS - 1) + (NA_WIN_COLS - 1)
    onehot = (d_col[:, :, None] == jnp.arange(2 * NA_WIN_COLS - 1)).astype(F32)
    t = jnp.einsum('hrd,cxd->hcrx', rpb, onehot, precision=HIGHEST)
    t = jnp.where(ok[None, :, None, :], t, NA_MASKED)
    b = jnp.stack([t[:, :, a:a + NA_WIN_ROWS, :] for a in range(NA_WIN_ROWS)])
    return b.reshape(NA_WIN_ROWS, NA_HEADS, GRID_W, NA_LOC)


LIN_HEADS = 4
LIN_DK = 64
LIN_DV = 128
LIN_QK = LIN_HEADS * LIN_DK
LIN_V = LIN_HEADS * LIN_DV


def _log_sigmoid(x):
    return jnp.minimum(x, 0.0) - jnp.log(1.0 + jnp.exp(-jnp.abs(x)))


def _lin_kernel(gqf_ref, gvf_ref, gqb_ref, gvb_ref, hqf_ref, hvf_ref, hqb_ref, hvb_ref,
                sgf_ref, sgb_ref, shf_ref, shb_ref, wa_ref, ba_ref, lbl_ref,
                ogf_ref, ogb_ref, ohf_ref, ohb_ref, ngf_ref, ngb_ref, nhf_ref, nhb_ref, worst_ref,
                st_ref, fb_ref, *, exact):
    j = pl.program_id(0)
    _, blk, _ = _seq_of(j)
    is_prompt = j < PROMPT_BLKS
    s0_refs = (sgf_ref, sgb_ref, shf_ref, shb_ref)

    @pl.when(blk == 0)
    def _():
        for i in range(4):
            st_ref[i] = jnp.where(is_prompt, 0.0, s0_refs[i][0])

    L = LIN_L
    nch = RB // L
    qk_refs = ((gqf_ref, gqb_ref), (hqf_ref, hqb_ref))
    v_refs = ((gvf_ref, gvb_ref), (hvf_ref, hvb_ref))
    o_refs = ((ogf_ref, ogb_ref), (ohf_ref, ohb_ref))
    lane = lax.broadcasted_iota(jnp.int32, (1, LIN_QK), 1)
    head_mask = [(lane >> 6) == h for h in range(LIN_HEADS)]
    r4 = lax.broadcasted_iota(jnp.int32, (LIN_HEADS * L, L), 0) & (L - 1)
    c4 = lax.broadcasted_iota(jnp.int32, (LIN_HEADS * L, L), 1)
    tri4 = [c4 <= r4, c4 >= r4]
    eye = (lax.broadcasted_iota(jnp.int32, (LIN_QK, LIN_QK), 0)
           == lax.broadcasted_iota(jnp.int32, (LIN_QK, LIN_QK), 1))
    tri_bf = [jnp.where(_tri(L, bool(d)), 1.0, 0.0).astype(BF16) for d in range(2)]

    def lower_bound(d):
        l0 = lbl_ref[d, 0:1, :]
        l1 = lbl_ref[d, 1:2, :]
        mx = jnp.maximum(l0, l1)
        e0 = jnp.exp(l0 - mx)
        e1 = jnp.exp(l1 - mx)
        p0 = e0 / (e0 + e1)
        p1 = e1 / (e0 + e1)
        return (p0 + p1) - p0

    lbs = (lower_bound(0), lower_bound(1))
    units = [(m, d, c) for c in range(nch) for d in range(2) for m in range(2)]
    row0 = {u: ((nch - 1 - u[2]) * L if u[1] else u[2] * L) for u in units}

    def gates(u, _):
        m, d, _ = u
        r0 = row0[u]
        x_ref = qk_refs[m][d]
        if m == 0:
            q = x_ref[r0:r0 + L, 0:256] * (GLA_DK ** -0.5)
            k = x_ref[r0:r0 + L, 256:512]
            ga_hi, ga_lo = _split2(x_ref[r0:r0 + L, 512:640])
            wa_hi, wa_lo = _split2(wa_ref[d])
            x = _dot(ga_hi, wa_hi) + _dot(ga_lo, wa_hi) + _dot(ga_hi, wa_lo) + ba_ref[d]
            g = _log_sigmoid(x) / GLA_GATE_NORM
        else:
            q = x_ref[r0:r0 + L, 0:256]
            f = lbs[d] + (1.0 - lbs[d]) * _sigmoid(x_ref[r0:r0 + L, 256 * (d + 1):256 * (d + 2)])
            k = 1.0 - f
            g = jnp.log(f)
        return q, k, _dot_exact_lhs(tri_bf[d], g)

    lasts = []

    def scale(u, p):
        m, d, _ = u
        q, k, cum = p
        last = cum[0:1, :] if d else cum[L - 1:L, :]
        lasts.append(last)
        q_in = q * jnp.exp(cum)
        k_out = (k * jnp.exp(-cum)).astype(BF16)
        k_end_t = (k * jnp.exp(last - cum)).T.astype(BF16)
        qs = jnp.concatenate([jnp.where(hm, q_in, 0.0) for hm in head_mask], axis=0).astype(BF16)
        dcol = jnp.sum(jnp.where(eye, jnp.exp(last), 0.0), axis=1, keepdims=True)
        v = v_refs[m][d][row0[u]:row0[u] + L, :].astype(BF16)
        return qs, k_out, k_end_t, dcol, v

    def products(u, p):
        qs, k_out, k_end_t, dcol, v = p
        upd = jnp.concatenate(
            [_dot(k_end_t[LIN_DK * h:LIN_DK * (h + 1)], v[:, LIN_DV * h:LIN_DV * (h + 1)]) for h in range(LIN_HEADS)],
            axis=0)
        return qs, _dot_nt(qs, k_out), upd, dcol, v

    def exact_scores(u):
        d = u[1]
        q, k, cum = gates(u, None)
        fb_ref[0], fb_ref[1], fb_ref[2] = q, k, cum
        head_sum = jnp.where((lax.broadcasted_iota(jnp.int32, (LIN_QK, 128), 0) >> 6)
                             == lax.broadcasted_iota(jnp.int32, (LIN_QK, 128), 1), 1.0, 0.0).astype(BF16)
        t_col = lax.broadcasted_iota(jnp.int32, (L, 1), 0)
        s_row = lax.broadcasted_iota(jnp.int32, (1, L), 1)

        def key_row(s, acc):
            ks = fb_ref[1, pl.ds(s, 1), :]
            cs = fb_ref[2, pl.ds(s, 1), :]
            w = fb_ref[0] * ks * jnp.exp(jnp.minimum(fb_ref[2] - cs, 0.0))
            cols = _dot_exact_rhs(w, head_sum)
            cols = jnp.where((t_col <= s) if d else (t_col >= s), cols, 0.0)
            return acc + jnp.concatenate(
                [jnp.where(s_row == s, cols[:, h:h + 1], 0.0) for h in range(LIN_HEADS)], axis=0)

        return lax.fori_loop(0, L, key_row, jnp.zeros((LIN_HEADS * L, L), F32)).astype(BF16)

    def mask(u, p):
        qs, a, upd, dcol, v = p
        a = exact_scores(u) if exact else jnp.where(tri4[u[1]], a, 0.0).astype(BF16)
        return qs, a, upd, dcol, v

    def intra(u, p):
        qs, a, upd, dcol, v = p
        o_intra = jnp.concatenate(
            [_dot(a[h * L:(h + 1) * L], v[:, LIN_DV * h:LIN_DV * (h + 1)]) for h in range(LIN_HEADS)], axis=0)
        return qs, o_intra, upd, dcol

    s = [st_ref[i] for i in range(4)]

    def outputs(u, p):
        m, d, _ = u
        qs, o_intra, upd, dcol = p
        i = 2 * m + d
        o = o_intra + _dot(qs, s[i].astype(BF16))
        o_refs[m][d][row0[u]:row0[u] + L, :] = jnp.concatenate(
            [o[h * L:(h + 1) * L] for h in range(LIN_HEADS)], axis=1).astype(BF16)
        s[i] = dcol * s[i] + upd

    _staged(units, (gates, scale, products, mask, intra, outputs))
    for i, n_ref in enumerate((ngf_ref, ngb_ref, nhf_ref, nhb_ref)):
        st_ref[i] = s[i]
        n_ref[0] = s[i]
    worst_ref[0] = jnp.broadcast_to(functools.reduce(jnp.minimum, lasts), (SUBLANES, LIN_QK))


def _lin(gqk, gv, hqf, hi, states, wa, ba, lbl):
    st_spec = pl.BlockSpec((1, LIN_QK, LIN_DV), lambda j: (_state_idx(j), 0, 0))
    so_spec = pl.BlockSpec((1, LIN_QK, LIN_DV), lambda j: (_seq_of(j)[0], 0, 0))
    so_shape = jax.ShapeDtypeStruct((N_SEQ, LIN_QK, LIN_DV), F32)
    o_shape = jax.ShapeDtypeStruct((N_TOK, LIN_V), BF16)
    worst_spec = pl.BlockSpec((1, SUBLANES, LIN_QK), lambda j: (j, 0, 0))
    worst_shape = jax.ShapeDtypeStruct((N_BLK, SUBLANES, LIN_QK), F32)
    packed = [s.reshape(DEC_BATCH, LIN_QK, LIN_DV) for s in states]

    def run(exact):
        return pl.pallas_call(
            functools.partial(_lin_kernel, exact=exact),
            grid=(N_BLK,),
            in_specs=[_fwd_spec(640), _fwd_spec(LIN_V), _bwd_spec(640), _bwd_spec(LIN_V),
                      _fwd_spec(768), _fwd_spec(LIN_V), _bwd_spec(768), _bwd_spec(LIN_V)]
                     + [st_spec] * 4 + [_const_spec(wa.shape), _const_spec(ba.shape), _const_spec(lbl.shape)],
            out_specs=[_fwd_spec(LIN_V), _bwd_spec(LIN_V), _fwd_spec(LIN_V), _bwd_spec(LIN_V)] + [so_spec] * 4
                      + [worst_spec],
            out_shape=[o_shape] * 4 + [so_shape] * 4 + [worst_shape],
            scratch_shapes=[pltpu.VMEM((4, LIN_QK, LIN_DV), F32), pltpu.VMEM((3, LIN_L, LIN_QK), F32)],
            compiler_params=_cparams(),
            name="lin_exact" if exact else "lin",
        )(gqk, gv, gqk, gv, hqf, hi, hqf, hi, *packed, wa, ba, lbl)

    fast = run(False)
    outs = lax.cond(jnp.min(fast[8]) < -LIN_SAFE_LOG_DECAY, lambda: tuple(run(True)[:8]), lambda: tuple(fast[:8]))
    unpack = lambda s: s[:BATCH].reshape(BATCH, LIN_HEADS, LIN_DK, LIN_DV)
    return outs[:4], [unpack(s) for s in outs[4:]]


def _f32(ref):
    return ref[...].astype(F32)


def _outproj0_kernel(xp_ref, xs_ref, m_ref, of_ref, ob_ref, z_ref, yb_ref, nw_ref, w_ref, o_ref):
    ya = (_f32(of_ref) + _f32(ob_ref)) * _silu(_f32(z_ref))
    ms = jnp.mean(ya * ya, axis=-1, keepdims=True)
    ya = ya * lax.rsqrt(ms + NORM_EPS) * nw_ref[...]
    y = _dot(ya.astype(BF16), w_ref[0:512, :]) + _dot(yb_ref[...], w_ref[512:1024, :])
    o_ref[...] = _tile_of(xp_ref, xs_ref) + m_ref[0, 2:3, :] * y


def _outproj0(xp, xs, mrows, of, ob, z, yb, nw, w):
    return pl.pallas_call(
        _outproj0_kernel,
        grid=(N_TILES,),
        in_specs=[_prompt_tile_spec(), _sample_tile_spec(), _mod_spec(), _row_spec(512), _row_spec(512),
                  _row_spec(512), _row_spec(512), _const_spec((1, 512)), _const_spec((D_MODEL, D_MODEL))],
        out_specs=_row_spec(D_MODEL),
        out_shape=jax.ShapeDtypeStruct((N_TOK, D_MODEL), F32),
        compiler_params=_cparams(),
        name="outproj0",
    )(xp, xs, mrows, of, ob, z, yb, nw, w)


def _head_rms128(o, w):
    parts = []
    for h in range(LIN_HEADS):
        oh = o[:, LIN_DV * h:LIN_DV * (h + 1)]
        ms = jnp.mean(oh * oh, axis=-1, keepdims=True)
        parts.append(oh * lax.rsqrt(ms + NORM_EPS) * w)
    return jnp.concatenate(parts, axis=-1)


def _outproj1_kernel(x_ref, m_ref, gf_ref, gb_ref, gg_ref, hf_ref, hb_ref, hg_ref,
                     gw_ref, hw_ref, w_ref, o_ref):
    yc = _head_rms128(_f32(gf_ref) + _f32(gb_ref), gw_ref[...]) * _silu(_f32(gg_ref))
    yd = _head_rms128(_f32(hf_ref) + _f32(hb_ref), hw_ref[...]) * _silu(_f32(hg_ref))
    y = _dot(yc.astype(BF16), w_ref[0:512, :]) + _dot(yd.astype(BF16), w_ref[512:1024, :])
    o_ref[...] = x_ref[...] + m_ref[0, 2:3, :] * y


def _outproj1(x, mrows, gf, gb, gg, hf, hb, hg, gw, hw, w):
    return pl.pallas_call(
        _outproj1_kernel,
        grid=(N_TILES,),
        in_specs=[_row_spec(D_MODEL), _mod_spec()] + [_row_spec(512)] * 6
                 + [_const_spec((1, LIN_DV)), _const_spec((1, LIN_DV)), _const_spec((D_MODEL, D_MODEL))],
        out_specs=_row_spec(D_MODEL),
        out_shape=jax.ShapeDtypeStruct((N_TOK, D_MODEL), F32),
        compiler_params=_cparams(),
        name="outproj1",
    )(x, mrows, gf, gb, gg, hf, hb, hg, gw, hw, w)


HALO_PER_TILE = TM // SUBLANES


def _ffn_kernel(x_ref, xp_ref, xn_ref, m_ref, nw_ref, wu_ref, cw_ref, cb_ref, wd_ref, *rest, n_out):
    o_refs, (hs_ref, act_ref) = rest[:n_out], rest[n_out:]
    i = pl.program_id(0)
    nw = nw_ref[...]
    sh = m_ref[0, 3:4, :]
    sc = m_ref[0, 4:5, :]
    x = x_ref[...]
    hs_ref[0:TM, :] = _norm_mod(x, nw, sh, sc).astype(BF16)
    halo = _norm_mod(jnp.concatenate([xn_ref[...], xp_ref[...]], axis=0), nw, sh, sc)
    hs_ref[TM:TM_EXT, :] = (halo * _halo_keep(i)).astype(BF16)
    is_prompt = i < PROMPT_TILES

    for c in range(FFN_DIM // FFN_CH):
        cs = slice(2 * c * FFN_CH, 2 * (c + 1) * FFN_CH)
        u = _dwconv_tile(_dot(hs_ref[...], wu_ref[:, cs]), cw_ref[:, cs], cb_ref[:, cs], i)
        act_ref[:, c * FFN_CH:(c + 1) * FFN_CH] = (_silu(u[:, 0:FFN_CH]) * u[:, FFN_CH:2 * FFN_CH]).astype(BF16)
    out = x + m_ref[0, 5:6, :] * _dot(act_ref[...], wd_ref[...])
    if len(o_refs) == 1:
        o_refs[0][...] = out
    else:
        @pl.when(is_prompt)
        def _():
            o_refs[0][...] = out

        @pl.when(jnp.logical_not(is_prompt))
        def _():
            o_refs[1][...] = out


def _ffn_interleave(a):
    lead = a.shape[:-1]
    n = FFN_DIM // FFN_CH
    return a.reshape(*lead, 2, n, FFN_CH).swapaxes(-3, -2).reshape(*lead, 2 * FFN_DIM)


def _ffn(x, mrows, nw, wu, cw, cb, wd, layer, split_out):
    per_layer = lambda *shape: pl.BlockSpec((None,) + shape, lambda i: (layer,) + (0,) * len(shape))
    prev = pl.BlockSpec((SUBLANES, D_MODEL), lambda i: (jnp.maximum(i * HALO_PER_TILE - 1, 0), 0))
    nxt = pl.BlockSpec((SUBLANES, D_MODEL),
                       lambda i: (jnp.minimum((i + 1) * HALO_PER_TILE, N_TOK // SUBLANES - 1), 0))
    single = dict(pipeline_mode=pl.Buffered(1))
    if split_out:
        out_specs = [_prompt_tile_spec(), _sample_tile_spec()]
        out_shape = [jax.ShapeDtypeStruct((N_PROMPT, D_MODEL), F32), jax.ShapeDtypeStruct((N_SAMPLE, D_MODEL), F32)]
    else:
        out_specs = [_row_spec(D_MODEL)]
        out_shape = [jax.ShapeDtypeStruct((N_TOK, D_MODEL), F32)]
    return pl.pallas_call(
        functools.partial(_ffn_kernel, n_out=len(out_specs)),
        grid=(N_TILES,),
        in_specs=[_row_spec(D_MODEL), prev, nxt, _mod_spec(), _const_spec((1, D_MODEL)),
                  pl.BlockSpec((None, D_MODEL, 2 * FFN_DIM), lambda i: (layer, 0, 0), **single),
                  per_layer(3, 2 * FFN_DIM), per_layer(1, 2 * FFN_DIM),
                  pl.BlockSpec((None, FFN_DIM, D_MODEL), lambda i: (layer, 0, 0), **single)],
        out_specs=out_specs,
        out_shape=out_shape,
        scratch_shapes=[pltpu.VMEM((TM_EXT, D_MODEL), BF16), pltpu.VMEM((TM, FFN_DIM), BF16)],
        compiler_params=_cparams(),
        name="ffn",
    )(x, x, x, mrows, nw, wu, cw, cb, wd)


def _pad_lanes(v, width=128):
    v = v.reshape(1, -1)
    return jnp.pad(v, ((0, 0), (0, width - v.shape[1])))


def kernel(x_prompt, x_sample, cache_na_k_l0, cache_na_v_l0, state_ssd_fwd_l0, state_ssd_bwd_l0,
           state_gla_fwd_l1, state_gla_bwd_l1, state_hgrn_fwd_l1, state_hgrn_bwd_l1, c,
           c_ctx, w_ada, b_ada, norm_w, ffn_w_up, ffn_conv_w, ffn_conv_b, ffn_w_down,
           w_in_l0, w_out_l0, ssd_conv_w_l0, ssd_conv_b_l0, ssd_dt_bias_l0, ssd_a_log_l0, ssd_d_l0,
           ssd_norm_w_l0, na_q_norm_l0, na_k_norm_l0, na_rpb_l0,
           w_in_l1, w_out_l1, gla_wa2_l1, gla_ba2_l1, gla_norm_w_l1, hgrn_lb_logits, hgrn_norm_w_l1):
    xp = x_prompt.reshape(N_PROMPT, D_MODEL)
    xs = x_sample.reshape(N_SAMPLE, D_MODEL)

    cvec8 = jnp.zeros((SUBLANES, D_MODEL), F32).at[0:DEC_BATCH].set(c).at[CTX_MOD_ROW].set(c_ctx)
    mods = _mods(cvec8, w_ada, b_ada)
    mods = mods.reshape(2, SUBLANES, 6, D_MODEL)
    mods = jnp.pad(mods, ((0, 0), (0, 0), (0, SUBLANES - 6), (0, 0)))

    zpad = lambda n: jnp.zeros((D_MODEL, n), BF16)
    w0b, w1b = w_in_l0.astype(BF16), w_in_l1.astype(BF16)
    w0 = jnp.concatenate([w0b[:, :1296], zpad(112), w0b[:, 1296:]], axis=1)
    w1 = jnp.concatenate([w1b[:, 0:512], w1b[:, 1536:1568], zpad(96), w1b[:, 512:1536], w1b[:, 1568:3360]], axis=1)
    seg = jnp.kron(jnp.eye(NA_HEADS, dtype=F32), jnp.full((NA_HEAD_DIM, NA_HEAD_DIM), 1.0 / NA_HEAD_DIM, F32)).astype(BF16)

    m0 = mods[0]
    z, xc, dt, qn, knb, vb, kn, v = _inproj0(
        xp, xs, m0, norm_w[0, 0].reshape(1, D_MODEL), w0, seg,
        jnp.tile(na_q_norm_l0, NA_HEADS).reshape(1, NA_WIDTH), jnp.tile(na_k_norm_l0, NA_HEADS).reshape(1, NA_WIDTH),
        ssd_conv_w_l0, ssd_conv_b_l0.reshape(1, SSD_XBC))
    of, ob, ssd_f, ssd_b = _ssd(
        xc, dt, state_ssd_fwd_l0, state_ssd_bwd_l0, _pad_lanes(ssd_dt_bias_l0), _pad_lanes(ssd_a_log_l0),
        jnp.repeat(ssd_d_l0, SSD_HEAD_DIM).reshape(1, SSD_WIDTH))
    yb, na_k, na_v = _na_ctx(qn, kn, v)
    yb = _na_lat(qn, knb, vb, cache_na_k_l0, cache_na_v_l0, _na_bias_table(na_rpb_l0), yb)
    x = _outproj0(xp, xs, m0, of, ob, z, yb, ssd_norm_w_l0.reshape(1, SSD_WIDTH), w_out_l0.astype(BF16))
    ffn_weights = (_ffn_interleave(ffn_w_up).astype(BF16), _ffn_interleave(ffn_conv_w),
                   _ffn_interleave(ffn_conv_b)[:, None, :], ffn_w_down.astype(BF16))
    x, = _ffn(x, m0, norm_w[0, 1].reshape(1, D_MODEL), *ffn_weights, layer=0, split_out=False)

    m1 = mods[1]
    gqk, gv, gg, hqf, hi, hg = _inproj1(x, m1, norm_w[1, 0].reshape(1, D_MODEL), w1)
    wa_pad = jnp.zeros((2, 128, LIN_QK), F32)
    wa_pad = wa_pad.at[0, 0:GLA_RANK].set(gla_wa2_l1[0]).at[1, GLA_RANK:2 * GLA_RANK].set(gla_wa2_l1[1])
    (gf, gb, hf, hb), (gla_f, gla_b, hgrn_f, hgrn_b) = _lin(
        gqk, gv, hqf, hi, (state_gla_fwd_l1, state_gla_bwd_l1, state_hgrn_fwd_l1, state_hgrn_bwd_l1),
        wa_pad, gla_ba2_l1.reshape(2, 1, LIN_QK), hgrn_lb_logits)
    x = _outproj1(x, m1, gf, gb, gg, hf, hb, hg, gla_norm_w_l1.reshape(1, LIN_DV),
                  hgrn_norm_w_l1.reshape(1, LIN_DV), w_out_l1.astype(BF16))
    y_p, y_s = _ffn(x, m1, norm_w[1, 1].reshape(1, D_MODEL), *ffn_weights, layer=1, split_out=True)
    return (y_p.reshape(BATCH, SEQ, D_MODEL), y_s.reshape(DEC_BATCH, DEC_SEQ, D_MODEL),
            na_k, na_v, ssd_f, ssd_b, gla_f, gla_b, hgrn_f, hgrn_b)
```

```python
import functools

import jax
import jax.numpy as jnp
from jax import lax
from jax.experimental import pallas as pl
from jax.experimental.pallas import tpu as pltpu

F32 = jnp.float32
BF16 = jnp.bfloat16
HIGHEST = lax.Precision.HIGHEST

D_MODEL = 1024
BATCH = 32
SEQ = 256
DEC_BATCH = 4
DEC_SEQ = 4096
PAST_LEN = 256
GRID_W = 64
NORM_EPS = 1e-6
N_PROMPT = BATCH * SEQ
N_SAMPLE = DEC_BATCH * DEC_SEQ
N_TOK = N_PROMPT + N_SAMPLE

SSD_HEADS = 8
SSD_HEAD_DIM = 64
SSD_STATE = 64
SSD_WIDTH = 512
SSD_BC = 128
SSD_XBC = 768
NA_HEADS = 8
NA_HEAD_DIM = 64
NA_WIDTH = 512
NA_WIN_ROWS = 8
NA_WIN_COLS = 16
NA_SCALE = NA_HEAD_DIM ** -0.5
GLA_HEADS = 4
GLA_DK = 64
GLA_DV = 128
GLA_RANK = 16
GLA_GATE_NORM = 16.0
HGRN_HEADS = 4
FFN_DIM = 2816

V7X_VMEM_BYTES = 64 * 1024 * 1024
VMEM_LIMIT = 56 * 1024 * 1024
SUBLANES = 8

TM = 512
N_TILES = N_TOK // TM
PROMPT_TILES = N_PROMPT // TM
TILES_PER_SAMPLE = DEC_SEQ // TM
CTX_MOD_ROW = DEC_BATCH

RB = 256
N_BLK = N_TOK // RB
PROMPT_BLKS = N_PROMPT // RB
BLKS_PER_SAMPLE = DEC_SEQ // RB
N_SEQ = BATCH + DEC_BATCH
SSD_L = 128
LIN_L = 64
LIN_SAFE_LOG_DECAY = 60.0
NA_R = 8
NA_ROWS_PER_ITER = 4
FFN_CH = 256
W0_COLS = 2944
W1_COLS = 3456


def _cparams(n_axes=1):
    return pltpu.CompilerParams(dimension_semantics=("arbitrary",) * n_axes,
                                vmem_limit_bytes=VMEM_LIMIT)


def _const_spec(shape):
    nd = len(shape)
    return pl.BlockSpec(shape, lambda *_: (0,) * nd)


def _sigmoid(x):
    return 1.0 / (1.0 + jnp.exp(-x))


def _silu(x):
    return x * _sigmoid(x)


def _softplus(x):
    return jnp.maximum(x, 0.0) + jnp.log(1.0 + jnp.exp(-jnp.abs(x)))


def _mod_row(i):
    return jnp.where(i < PROMPT_TILES, CTX_MOD_ROW, (i - PROMPT_TILES) // TILES_PER_SAMPLE)


def _dot(a, b, **kw):
    return jnp.dot(a, b, preferred_element_type=F32, **kw)


def _dot_nt(a, b):
    return lax.dot_general(a, b, (((1,), (1,)), ((), ())), preferred_element_type=F32)


def _split2(x):
    hi = x.astype(BF16)
    lo = (x - hi.astype(F32)).astype(BF16)
    return hi, lo


def _split3(x):
    hi = x.astype(BF16)
    r = x - hi.astype(F32)
    mid = r.astype(BF16)
    lo = (r - mid.astype(F32)).astype(BF16)
    return hi, mid, lo


def _dot_exact_lhs(a_bf, x):
    hi, lo = _split2(x)
    return _dot(a_bf, hi) + _dot(a_bf, lo)


def _dot_exact_rhs(x, b_bf):
    hi, lo = _split2(x)
    return _dot(hi, b_bf) + _dot(lo, b_bf)


def _skewed(units, stages):
    results = {}
    for step in range(len(units) + len(stages) - 1):
        for k, stage in enumerate(stages):
            i = step - k
            if 0 <= i < len(units):
                results[(k, i)] = stage(units[i], results.pop((k - 1, i), None))


def _staged(units, stages):
    results = {i: None for i in range(len(units))}
    for stage in stages:
        for i, u in enumerate(units):
            results[i] = stage(u, results[i])


def _dot_tn(a, b):
    return lax.dot_general(a, b, (((0,), (0,)), ((), ())), preferred_element_type=F32)


MODS_NB = 1536


def _mods_kernel(c_ref, w_ref, b_ref, o_ref):
    s_hi, s_lo = _split2(_silu(c_ref[...]))
    w_hi, w_lo = _split2(w_ref[0])
    o_ref[0] = _dot(s_hi, w_hi) + _dot(s_lo, w_hi) + _dot(s_hi, w_lo) + b_ref[0]


def _mods(cvec8, w_ada, b_ada):
    depth = w_ada.shape[0]
    nb = 6 * D_MODEL // MODS_NB
    return pl.pallas_call(
        _mods_kernel,
        grid=(depth, nb),
        in_specs=[
            _const_spec((SUBLANES, D_MODEL)),
            pl.BlockSpec((1, D_MODEL, MODS_NB), lambda l, j: (l, 0, j)),
            pl.BlockSpec((1, 1, MODS_NB), lambda l, j: (l, 0, j)),
        ],
        out_specs=pl.BlockSpec((1, SUBLANES, MODS_NB), lambda l, j: (l, 0, j)),
        out_shape=jax.ShapeDtypeStruct((depth, SUBLANES, 6 * D_MODEL), F32),
        compiler_params=_cparams(2),
        name="mods",
    )(cvec8, w_ada, b_ada.reshape(depth, 1, 6 * D_MODEL))


def _norm_mod(x, nw, sh, sc):
    ms = jnp.mean(x * x, axis=-1, keepdims=True)
    y = x * lax.rsqrt(ms + NORM_EPS) * nw
    return y * (1.0 + sc) + sh


def _head_rms(x, seg, w):
    ms = _dot((x * x).astype(BF16), seg)
    return x * lax.rsqrt(ms + NORM_EPS) * w


def _tile_of(xp_ref, xs_ref):
    return jnp.where(pl.program_id(0) < PROMPT_TILES, xp_ref[...], xs_ref[...])


HALO_ROWS = 2 * SUBLANES
TM_EXT = TM + HALO_ROWS


def _halo_keep(i):
    seq = jnp.where(i < PROMPT_TILES, SEQ, DEC_SEQ)
    keep_prev = jnp.where(((i * TM) & (seq - 1)) == 0, 0.0, 1.0)
    keep_next = jnp.where(((i * TM + TM) & (seq - 1)) == 0, 0.0, 1.0)
    hrow = lax.broadcasted_iota(jnp.int32, (HALO_ROWS, 1), 0)
    return jnp.where(hrow < SUBLANES, keep_next, keep_prev)


def _dwconv_tile(u_all, w, bias, i):
    um = pltpu.roll(u_all, 1, axis=0)[0:TM] * w[0:1]
    up = pltpu.roll(u_all, TM_EXT - 1, axis=0)[0:TM] * w[2:3]
    out = bias + um + u_all[0:TM] * w[1:2] + up
    edge_on = jnp.where(i < PROMPT_TILES, 1.0, 0.0)
    erow = lax.broadcasted_iota(jnp.int32, (HALO_ROWS, 1), 0)
    at_first = jnp.where(erow == SUBLANES, edge_on, 0.0)
    at_last = jnp.where(erow == SUBLANES - 1, edge_on, 0.0)
    pieces, r = [], 0
    for edge in range(SEQ, TM, SEQ):
        sl = slice(edge - SUBLANES, edge + SUBLANES)
        pieces += [out[r:edge - SUBLANES], out[sl] - at_first * um[sl] - at_last * up[sl]]
        r = edge + SUBLANES
    return jnp.concatenate(pieces + [out[r:TM]], axis=0)


def _inproj0_kernel(xp_ref, xs_ref, xn_ref, xv_ref, m_ref, nw_ref, w_ref, seg_ref, qw_ref, kw_ref, cw_ref, cb_ref,
                    z_ref, xc_ref, dt_ref, qn_ref, knb_ref, vb_ref, kn_ref, v_ref, hs_ref):
    i = pl.program_id(0)
    nw, sh, sc = nw_ref[...], m_ref[0, 0:1, :], m_ref[0, 1:2, :]
    hs_ref[0:TM, :] = _norm_mod(_tile_of(xp_ref, xs_ref), nw, sh, sc).astype(BF16)
    halo = _norm_mod(jnp.concatenate([xn_ref[...], xv_ref[...]], axis=0), nw, sh, sc)
    hs_ref[TM:TM_EXT, :] = (halo * _halo_keep(i)).astype(BF16)
    q = _dot(hs_ref[0:TM, :], w_ref[:, 1408:1920])
    k = _dot(hs_ref[0:TM, :], w_ref[:, 1920:2432])
    xc_ref[...] = _silu(_dwconv_tile(_dot(hs_ref[...], w_ref[:, 512:1280]), cw_ref[...], cb_ref[...], i))
    ms_q = _dot((q * q).astype(BF16), seg_ref[...])
    ms_k = _dot((k * k).astype(BF16), seg_ref[...])
    z_ref[...] = _dot(hs_ref[0:TM, :], w_ref[:, 0:512]).astype(BF16)
    dt_ref[...] = _dot(hs_ref[0:TM, :], w_ref[:, 1280:1408])
    v = _dot(hs_ref[0:TM, :], w_ref[:, 2432:2944])
    vb_ref[...] = v.astype(BF16)
    qn_ref[...] = (q * lax.rsqrt(ms_q + NORM_EPS) * qw_ref[...] * NA_SCALE).astype(BF16)
    kn = k * lax.rsqrt(ms_k + NORM_EPS) * kw_ref[...]
    knb_ref[...] = kn.astype(BF16)

    @pl.when(pl.program_id(0) < PROMPT_TILES)
    def _():
        kn_ref[...] = kn
        v_ref[...] = v


def _row_spec(width):
    return pl.BlockSpec((TM, width), lambda i: (i, 0))


def _mod_spec():
    return pl.BlockSpec((1, SUBLANES, D_MODEL), lambda i: (_mod_row(i), 0, 0))


def _prompt_tile_spec(width=D_MODEL):
    return pl.BlockSpec((TM, width), lambda i: (jnp.minimum(i, PROMPT_TILES - 1), 0))


def _sample_tile_spec(width=D_MODEL):
    return pl.BlockSpec((TM, width), lambda i: (jnp.maximum(i - PROMPT_TILES, 0), 0))


def _halo_specs(width, rows, total_rows, first_tile=0):
    per_tile = TM // rows
    last = total_rows // rows - 1
    nxt = pl.BlockSpec((rows, width), lambda i: (jnp.clip((i - first_tile + 1) * per_tile, 0, last), 0))
    prv = pl.BlockSpec((rows, width), lambda i: (jnp.clip((i - first_tile) * per_tile - 1, 0, last), 0))
    return [nxt, prv]


def _inproj0(xp, xs, mrows, nw, w0, seg, qw, kw, conv_w, conv_b):
    widths = (512, 768, 128, 512, 512, 512)
    dtypes = (BF16, F32, F32, BF16, BF16, BF16)
    nxt, prv = _halo_specs(D_MODEL, SUBLANES, N_SAMPLE, PROMPT_TILES)
    return pl.pallas_call(
        _inproj0_kernel,
        grid=(N_TILES,),
        in_specs=[
            _prompt_tile_spec(), _sample_tile_spec(), nxt, prv, _mod_spec(), _const_spec((1, D_MODEL)),
            _const_spec((D_MODEL, W0_COLS)), _const_spec((512, 512)),
            _const_spec((1, 512)), _const_spec((1, 512)), _const_spec((3, SSD_XBC)), _const_spec((1, SSD_XBC)),
        ],
        out_specs=[_row_spec(w) for w in widths] + [_prompt_tile_spec(NA_WIDTH)] * 2,
        out_shape=[jax.ShapeDtypeStruct((N_TOK, w), d) for w, d in zip(widths, dtypes)]
                  + [jax.ShapeDtypeStruct((N_PROMPT, NA_WIDTH), F32)] * 2,
        scratch_shapes=[pltpu.VMEM((TM_EXT, D_MODEL), BF16)],
        compiler_params=_cparams(),
        name="inproj0",
    )(xp, xs, xs, xs, mrows, nw, w0, seg, qw, kw, conv_w, conv_b)


def _inproj1_kernel(x_ref, m_ref, nw_ref, w_ref, gqk_ref, gv_ref, gg_ref, hqf_ref, hi_ref, hg_ref, hs_ref):
    hs_ref[...] = _norm_mod(x_ref[...], nw_ref[...], m_ref[0, 0:1, :], m_ref[0, 1:2, :]).astype(BF16)
    gqk_ref[...] = _dot(hs_ref[...], w_ref[:, 0:640])
    gv_ref[...] = _dot(hs_ref[...], w_ref[:, 640:1152]).astype(BF16)
    gg_ref[...] = _dot(hs_ref[...], w_ref[:, 1152:1664]).astype(BF16)
    hqf_ref[...] = _dot(hs_ref[...], w_ref[:, 1664:2432])
    hi_ref[...] = _dot(hs_ref[...], w_ref[:, 2432:2944]).astype(BF16)
    hg_ref[...] = _dot(hs_ref[...], w_ref[:, 2944:3456]).astype(BF16)


def _inproj1(x, mrows, nw, w1):
    widths = (640, 512, 512, 768, 512, 512)
    dtypes = (F32, BF16, BF16, F32, BF16, BF16)
    return pl.pallas_call(
        _inproj1_kernel,
        grid=(N_TILES,),
        in_specs=[_row_spec(D_MODEL), _mod_spec(), _const_spec((1, D_MODEL)),
                  _const_spec((D_MODEL, W1_COLS))],
        out_specs=[_row_spec(w) for w in widths],
        out_shape=[jax.ShapeDtypeStruct((N_TOK, w), d) for w, d in zip(widths, dtypes)],
        scratch_shapes=[pltpu.VMEM((TM, D_MODEL), BF16)],
        compiler_params=_cparams(),
        name="inproj1",
    )(x, mrows, nw, w1)


def _seq_of(j):
    jj = j - PROMPT_BLKS
    is_p = j < PROMPT_BLKS
    seq = jnp.where(is_p, j, BATCH + jj // BLKS_PER_SAMPLE)
    blk = jnp.where(is_p, 0, jj % BLKS_PER_SAMPLE)
    nblk = jnp.where(is_p, 1, BLKS_PER_SAMPLE)
    return seq, blk, nblk


def _bwd_blk(j):
    _, blk, nblk = _seq_of(j)
    return j - blk + (nblk - 1 - blk)


def _state_idx(j):
    seq, _, _ = _seq_of(j)
    return jnp.maximum(seq - BATCH, 0)


def _fwd_spec(width):
    return pl.BlockSpec((RB, width), lambda j: (j, 0))


def _bwd_spec(width):
    return pl.BlockSpec((RB, width), lambda j: (_bwd_blk(j), 0))


def _tri(n, upper):
    r = lax.broadcasted_iota(jnp.int32, (n, n), 0)
    c = lax.broadcasted_iota(jnp.int32, (n, n), 1)
    return (c >= r) if upper else (c <= r)


def _diag_blocks(x, rows):
    lh = lax.broadcasted_iota(jnp.int32, (1, 4 * SSD_HEAD_DIM), 1) >> 6
    out = x[3 * rows:4 * rows]
    for h in (2, 1, 0):
        out = jnp.where(lh == h, x[h * rows:(h + 1) * rows], out)
    return out


def _ssd_kernel(xf_ref, dtf_ref, xb_ref, dtb_ref, s0f_ref, s0b_ref, dtbias_ref, alog_ref, dsk_ref, ex_ref,
                of_ref, ob_ref, sf_ref, sb_ref, st_ref):
    j = pl.program_id(0)
    _, blk, _ = _seq_of(j)
    is_prompt = j < PROMPT_BLKS

    @pl.when(blk == 0)
    def _():
        st_ref[0] = jnp.where(is_prompt, 0.0, s0f_ref[0])
        st_ref[1] = jnp.where(is_prompt, 0.0, s0b_ref[0])

    a_neg = -jnp.exp(alog_ref[...])
    a_col = jnp.broadcast_to(a_neg, (SUBLANES, 128)).T
    L = SSD_L
    nch = RB // L
    xc_refs = (xf_ref, xb_ref)
    dt_refs = (dtf_ref, dtb_ref)
    o_refs = (of_ref, ob_ref)

    units = [(d, c) for c in range(nch) for d in range(2)]
    row0 = {(d, c): ((nch - 1 - c) * L if d else c * L) for d, c in units}
    tri = [_tri(L, False), _tri(L, True)]
    tri_bf = [jnp.where(t, 1.0, 0.0).astype(BF16) for t in tri]
    lane = lax.broadcasted_iota(jnp.int32, (1, SSD_BC), 1)
    gmask = [lane < SSD_STATE, lane >= SSD_STATE]
    eye_bf = jnp.where(tri[0] & tri[1], 1.0, 0.0).astype(BF16)
    zero_blk = jnp.zeros((SSD_STATE, 4 * SSD_HEAD_DIM), F32)

    def gates(u, _):
        d, r0 = u[0], row0[u]
        dt = _softplus(dt_refs[d][r0:r0 + L, :] + dtbias_ref[...])
        cum = _dot_exact_lhs(tri_bf[d], dt * a_neg)
        dt8 = dt.T[8 * d:8 * d + 8]
        ct8 = _dot_exact_rhs(dt8 * a_col[8 * d:8 * d + 8, 0:1], tri_bf[1 - d])
        parts = _split3(cum[0:SUBLANES] if d else cum[L - SUBLANES:L])
        last_x = _dot(parts[0], ex_ref[d]) + _dot(parts[1], ex_ref[d]) + _dot(parts[2], ex_ref[d])
        e_last_x = jnp.exp(last_x[0:1] if d else last_x[SUBLANES - 1:SUBLANES])
        last_col = ct8[:, 0:1] if d else ct8[:, L - 1:L]
        f_end = jnp.exp(last_col - ct8) * dt8
        bm = xc_refs[d][r0:r0 + L, 512:640].astype(BF16)
        cm = xc_refs[d][r0:r0 + L, 640:768]
        b_t = _dot_nt(eye_bf, bm)
        cb_g = [_dot_nt(jnp.where(gmask[g], cm, 0.0).astype(BF16), bm) for g in range(2)]
        return cum, ct8, dt8, e_last_x, f_end, b_t, cb_g, cm

    def operands(u, p):
        d = u[0]
        cum, ct8, dt8, e_last_x, f_end, b_t, cb_g, cm = p
        lhs, k_t = [], []
        for g in range(2):
            ls, ks = [], []
            for hh in range(4):
                h = 4 * g + hh
                pc = jnp.broadcast_to(cum[:, 8 * d + h:8 * d + h + 1], (L, L))
                dec = jnp.exp(jnp.where(tri[d], pc - ct8[h:h + 1, :], -jnp.inf))
                w = cb_g[g] * dec * dt8[h:h + 1, :]
                ls.append(jnp.concatenate([w.astype(BF16), (cm * jnp.exp(pc)).astype(BF16)], axis=1))
                ks.append((b_t[SSD_STATE * g:SSD_STATE * (g + 1)] * f_end[h:h + 1, :]).astype(BF16))
            lhs.append(jnp.concatenate(ls, axis=0))
            k_t.append(jnp.concatenate(ks, axis=0))
        return lhs, k_t, e_last_x

    def state_update(u, p):
        d, r0 = u[0], row0[u]
        lhs, k_t, e_last_x = p
        x_bf = [xc_refs[d][r0:r0 + L, 256 * g:256 * (g + 1)].astype(BF16) for g in range(2)]
        up = [_diag_blocks(_dot(k_t[g], x_bf[g]), SSD_STATE) for g in range(2)]
        upd = jnp.concatenate([jnp.concatenate([up[0], zero_blk], axis=1),
                               jnp.concatenate([zero_blk, up[1]], axis=1)], axis=0)
        return lhs, x_bf, e_last_x, upd

    s = [st_ref[0], st_ref[1]]

    def outputs(u, p):
        d, r0 = u[0], row0[u]
        lhs, x_bf, e_last_x, upd = p
        s_bf = s[d].astype(BF16)
        o = jnp.concatenate(
            [_diag_blocks(_dot(lhs[g], jnp.concatenate([x_bf[g], s_bf[:, 256 * g:256 * (g + 1)]], axis=0)), L)
             for g in range(2)], axis=1)
        if d == 0:
            o = o + dsk_ref[...] * xf_ref[r0:r0 + L, 0:SSD_WIDTH]
        o_refs[d][r0:r0 + L, :] = o.astype(BF16)
        s[d] = e_last_x * s[d] + upd

    _skewed(units, (gates, operands, state_update, outputs))
    st_ref[0] = s[0]
    st_ref[1] = s[1]
    sf_ref[0] = s[0]
    sb_ref[0] = s[1]


def _ssd_pack_state(s):
    b = s.shape[0]
    g = s.reshape(b, 2, 4, SSD_STATE, SSD_HEAD_DIM).transpose(0, 1, 3, 2, 4).reshape(b, 2, SSD_STATE, 256)
    z = jnp.zeros((b, SSD_STATE, 256), F32)
    return jnp.concatenate([jnp.concatenate([g[:, 0], z], axis=2), jnp.concatenate([z, g[:, 1]], axis=2)], axis=1)


def _ssd_unpack_state(s):
    b = s.shape[0]
    g = jnp.stack([s[:, 0:SSD_STATE, 0:256], s[:, SSD_STATE:, 256:512]], axis=1)
    g = g.reshape(b, 2, SSD_STATE, 4, SSD_HEAD_DIM).transpose(0, 1, 3, 2, 4)
    return g.reshape(b, SSD_HEADS, SSD_STATE, SSD_HEAD_DIM)


def _ssd(xc, dt, s0f, s0b, dtbias, alog, dsk):
    st_shape = (1, 2 * SSD_STATE, SSD_WIDTH)
    st_spec = pl.BlockSpec(st_shape, lambda j: (_state_idx(j), 0, 0))
    so_spec = pl.BlockSpec(st_shape, lambda j: (_seq_of(j)[0], 0, 0))
    so_shape = jax.ShapeDtypeStruct((N_SEQ, 2 * SSD_STATE, SSD_WIDTH), F32)
    col = jnp.arange(128)[:, None]
    lane_head = jnp.arange(SSD_WIDTH)[None, :] // SSD_HEAD_DIM
    expand = jnp.stack([col == lane_head, col == SSD_HEADS + lane_head]).astype(BF16)
    of, ob, sf, sb = pl.pallas_call(
        _ssd_kernel,
        grid=(N_BLK,),
        in_specs=[
            _fwd_spec(SSD_XBC), _fwd_spec(128), _bwd_spec(SSD_XBC), _bwd_spec(128), st_spec, st_spec,
            _const_spec((1, 128)), _const_spec((1, 128)), _const_spec((1, SSD_WIDTH)),
            _const_spec((2, 128, SSD_WIDTH)),
        ],
        out_specs=[_fwd_spec(SSD_WIDTH), _bwd_spec(SSD_WIDTH), so_spec, so_spec],
        out_shape=[jax.ShapeDtypeStruct((N_TOK, SSD_WIDTH), BF16)] * 2 + [so_shape, so_shape],
        scratch_shapes=[pltpu.VMEM((2, 2 * SSD_STATE, SSD_WIDTH), F32)],
        compiler_params=_cparams(),
        name="ssd",
    )(xc, dt, xc, dt, _ssd_pack_state(s0f), _ssd_pack_state(s0b), dtbias, alog, dsk, expand)
    return of, ob, _ssd_unpack_state(sf[:BATCH]), _ssd_unpack_state(sb[:BATCH])


NA_PAIRS = NA_HEADS // 2


def _stack_pair(qt):
    lower = lax.broadcasted_iota(jnp.int32, (1, 2 * NA_HEAD_DIM), 1) < NA_HEAD_DIM
    zero = jnp.zeros_like(qt)
    return jnp.concatenate([jnp.where(lower, qt, zero), jnp.where(lower, zero, qt)], axis=0)


def _unstack_pair(x, n):
    lower = lax.broadcasted_iota(jnp.int32, (1, 2 * NA_HEAD_DIM), 1) < NA_HEAD_DIM
    return jnp.where(lower, x[0:n], x[n:2 * n])


def _na_ctx_kernel(q_ref, k_ref, v_ref, y_ref, ko_ref, vo_ref):
    tiles = [slice(128 * i, 128 * (i + 1)) for i in range(NA_PAIRS)]
    s = [_dot_nt(_stack_pair(q_ref[:, ts]), k_ref[:, ts].astype(BF16)) for ts in tiles]
    p, l = [], []
    for i in range(NA_PAIRS):
        e = jnp.exp(s[i] - jnp.max(s[i], axis=-1, keepdims=True))
        l.append(jnp.sum(e, axis=-1, keepdims=True))
        p.append(e.astype(BF16))
    o = [_dot(p[i], v_ref[:, tiles[i]].astype(BF16)) for i in range(NA_PAIRS)]
    for i in range(NA_PAIRS):
        y = _unstack_pair(o[i], SEQ) / _unstack_pair(jnp.broadcast_to(l[i], o[i].shape), SEQ)
        y_ref[:, tiles[i]] = y.astype(BF16)
    for h in range(NA_HEADS):
        sl = slice(NA_HEAD_DIM * h, NA_HEAD_DIM * (h + 1))
        ko_ref[0, h] = k_ref[:, sl]
        vo_ref[0, h] = v_ref[:, sl]


def _na_ctx(qn, kn, v):
    blk = lambda w: pl.BlockSpec((SEQ, w), lambda b: (b, 0))
    hm = pl.BlockSpec((1, NA_HEADS, SEQ, NA_HEAD_DIM), lambda b: (b, 0, 0, 0))
    hm_shape = jax.ShapeDtypeStruct((BATCH, NA_HEADS, SEQ, NA_HEAD_DIM), F32)
    return pl.pallas_call(
        _na_ctx_kernel,
        grid=(BATCH,),
        in_specs=[blk(NA_WIDTH)] * 3,
        out_specs=[blk(NA_WIDTH), hm, hm],
        out_shape=[jax.ShapeDtypeStruct((N_TOK, NA_WIDTH), BF16), hm_shape, hm_shape],
        compiler_params=_cparams(),
        name="na_ctx",
    )(qn, kn, v)


GRID_ROWS = DEC_SEQ // GRID_W
NA_LOC = NA_WIN_ROWS * GRID_W
NA_MASKED = -1e30


def _na_lat_kernel(q_ref, k_ref, v_ref, kc_ref, vc_ref, bt_ref, yin_ref, y_ref):
    del yin_ref
    rb = pl.program_id(1)

    tiles = [slice(128 * i, 128 * (i + 1)) for i in range(NA_PAIRS)]

    def rows(it, carry):
        units, q0, k0, dr0 = [], {}, {}, {}
        for rr_ in range(NA_ROWS_PER_ITER):
            r = it * NA_ROWS_PER_ITER + rr_
            rr = rb * NA_R + r
            rs = jnp.clip(rr - NA_WIN_ROWS // 2, 0, GRID_ROWS - NA_WIN_ROWS)
            dr0[rr_] = rs - rr + (NA_WIN_ROWS - 1)
            q0[rr_] = pl.multiple_of(r * GRID_W, GRID_W)
            k0[rr_] = pl.multiple_of(rs * GRID_W, GRID_W)
            units += [(rr_, i) for i in range(NA_PAIRS)]
        def scores(u, _):
            r, i = u
            qq = _stack_pair(q_ref[pl.ds(q0[r], GRID_W), tiles[i]])
            s_loc = (_dot_nt(qq, k_ref[pl.ds(k0[r], NA_LOC), tiles[i]])
                     + bt_ref[dr0[r], 2 * i:2 * i + 2].reshape(2 * GRID_W, NA_LOC))
            return s_loc, _dot_nt(qq, kc_ref[0, :, tiles[i]])

        def softmax(u, s):
            s_loc, s_ctx = s
            m = jnp.maximum(jnp.max(s_loc, axis=-1, keepdims=True), jnp.max(s_ctx, axis=-1, keepdims=True))
            e_loc = jnp.exp(s_loc - m)
            e_ctx = jnp.exp(s_ctx - m)
            l = jnp.sum(e_loc, axis=-1, keepdims=True) + jnp.sum(e_ctx, axis=-1, keepdims=True)
            return e_loc.astype(BF16), e_ctx.astype(BF16), l

        def weighted(u, p):
            r, i = u
            p_loc, p_ctx, l = p
            o = _dot(p_loc, v_ref[pl.ds(k0[r], NA_LOC), tiles[i]]) + _dot(p_ctx, vc_ref[0, :, tiles[i]])
            y = _unstack_pair(o, GRID_W) / _unstack_pair(jnp.broadcast_to(l, o.shape), GRID_W)
            y_ref[pl.ds(q0[r], GRID_W), tiles[i]] = y.astype(BF16)

        _skewed(units, (scores, softmax, weighted))
        return carry

    lax.fori_loop(0, NA_R // NA_ROWS_PER_ITER, rows, 0)


def _na_lat(qn, knb, vb, kc, vc, btab, y_in):
    rows_per_step = NA_R * GRID_W
    steps = GRID_ROWS // NA_R
    off_q = N_PROMPT // rows_per_step
    off_s = N_PROMPT // DEC_SEQ
    qspec = pl.BlockSpec((rows_per_step, NA_WIDTH), lambda b, r: (off_q + b * steps + r, 0))
    kvspec = pl.BlockSpec((DEC_SEQ, NA_WIDTH), lambda b, r: (off_s + b, 0))
    cspec = pl.BlockSpec((1, PAST_LEN, NA_WIDTH), lambda b, r: (b, 0, 0))
    token_major = lambda a: a.transpose(0, 2, 1, 3).reshape(DEC_BATCH, PAST_LEN, NA_WIDTH).astype(BF16)
    kc, vc = token_major(kc), token_major(vc)
    return pl.pallas_call(
        _na_lat_kernel,
        grid=(DEC_BATCH, steps),
        in_specs=[qspec, kvspec, kvspec, cspec, cspec,
                  _const_spec((NA_WIN_ROWS, NA_HEADS, GRID_W, NA_LOC)),
                  pl.BlockSpec(memory_space=pl.ANY)],
        out_specs=qspec,
        out_shape=jax.ShapeDtypeStruct((N_TOK, NA_WIDTH), BF16),
        input_output_aliases={6: 0},
        compiler_params=_cparams(2),
        name="na_lat",
    )(qn, knb, vb, kc, vc, btab, y_in)


def _na_bias_table(rpb):
    col = jnp.arange(GRID_W)
    col_start = jnp.clip(col - NA_WIN_COLS // 2, 0, GRID_W - NA_WIN_COLS)
    ok = (col[None, :] >= col_start[:, None]) & (col[None, :] < col_start[:, None] + NA_WIN_COLS)
    d_col = jnp.clip(col[None, :] - col[:, None], -(NA_WIN_COLS - 1), NA_WIN_COLS - 1) + (NA_WIN_COLS - 1)
    onehot = (d_col[:, :, None] == jnp.arange(2 * NA_WIN_COLS - 1)).astype(F32)
    t = jnp.einsum('hrd,cxd->hcrx', rpb, onehot, precision=HIGHEST)
    t = jnp.where(ok[None, :, None, :], t, NA_MASKED)
    b = jnp.stack([t[:, :, a:a + NA_WIN_ROWS, :] for a in range(NA_WIN_ROWS)])
    return b.reshape(NA_WIN_ROWS, NA_HEADS, GRID_W, NA_LOC)


LIN_HEADS = 4
LIN_DK = 64
LIN_DV = 128
LIN_QK = LIN_HEADS * LIN_DK
LIN_V = LIN_HEADS * LIN_DV


def _log_sigmoid(x):
    return jnp.minimum(x, 0.0) - jnp.log(1.0 + jnp.exp(-jnp.abs(x)))


def _lin_kernel(gqf_ref, gvf_ref, gqb_ref, gvb_ref, hqf_ref, hvf_ref, hqb_ref, hvb_ref,
                sgf_ref, sgb_ref, shf_ref, shb_ref, wa_ref, ba_ref, lbl_ref,
                ogf_ref, ogb_ref, ohf_ref, ohb_ref, ngf_ref, ngb_ref, nhf_ref, nhb_ref, worst_ref,
                st_ref, fb_ref, *, exact):
    j = pl.program_id(0)
    _, blk, _ = _seq_of(j)
    is_prompt = j < PROMPT_BLKS
    s0_refs = (sgf_ref, sgb_ref, shf_ref, shb_ref)

    @pl.when(blk == 0)
    def _():
        for i in range(4):
            st_ref[i] = jnp.where(is_prompt, 0.0, s0_refs[i][0])

    L = LIN_L
    nch = RB // L
    qk_refs = ((gqf_ref, gqb_ref), (hqf_ref, hqb_ref))
    v_refs = ((gvf_ref, gvb_ref), (hvf_ref, hvb_ref))
    o_refs = ((ogf_ref, ogb_ref), (ohf_ref, ohb_ref))
    lane = lax.broadcasted_iota(jnp.int32, (1, LIN_QK), 1)
    head_mask = [(lane >> 6) == h for h in range(LIN_HEADS)]
    r4 = lax.broadcasted_iota(jnp.int32, (LIN_HEADS * L, L), 0) & (L - 1)
    c4 = lax.broadcasted_iota(jnp.int32, (LIN_HEADS * L, L), 1)
    tri4 = [c4 <= r4, c4 >= r4]
    eye = (lax.broadcasted_iota(jnp.int32, (LIN_QK, LIN_QK), 0)
           == lax.broadcasted_iota(jnp.int32, (LIN_QK, LIN_QK), 1))
    tri_bf = [jnp.where(_tri(L, bool(d)), 1.0, 0.0).astype(BF16) for d in range(2)]

    def lower_bound(d):
        l0 = lbl_ref[d, 0:1, :]
        l1 = lbl_ref[d, 1:2, :]
        mx = jnp.maximum(l0, l1)
        e0 = jnp.exp(l0 - mx)
        e1 = jnp.exp(l1 - mx)
        p0 = e0 / (e0 + e1)
        p1 = e1 / (e0 + e1)
        return (p0 + p1) - p0

    lbs = (lower_bound(0), lower_bound(1))
    units = [(m, d, c) for c in range(nch) for d in range(2) for m in range(2)]
    row0 = {u: ((nch - 1 - u[2]) * L if u[1] else u[2] * L) for u in units}

    def gates(u, _):
        m, d, _ = u
        r0 = row0[u]
        x_ref = qk_refs[m][d]
        if m == 0:
            q = x_ref[r0:r0 + L, 0:256] * (GLA_DK ** -0.5)
            k = x_ref[r0:r0 + L, 256:512]
            ga_hi, ga_lo = _split2(x_ref[r0:r0 + L, 512:640])
            wa_hi, wa_lo = _split2(wa_ref[d])
            x = _dot(ga_hi, wa_hi) + _dot(ga_lo, wa_hi) + _dot(ga_hi, wa_lo) + ba_ref[d]
            g = _log_sigmoid(x) / GLA_GATE_NORM
        else:
            q = x_ref[r0:r0 + L, 0:256]
            f = lbs[d] + (1.0 - lbs[d]) * _sigmoid(x_ref[r0:r0 + L, 256 * (d + 1):256 * (d + 2)])
            k = 1.0 - f
            g = jnp.log(f)
        return q, k, _dot_exact_lhs(tri_bf[d], g)

    lasts = []

    def scale(u, p):
        m, d, _ = u
        q, k, cum = p
        last = cum[0:1, :] if d else cum[L - 1:L, :]
        lasts.append(last)
        q_in = q * jnp.exp(cum)
        k_out = (k * jnp.exp(-cum)).astype(BF16)
        k_end_t = (k * jnp.exp(last - cum)).T.astype(BF16)
        qs = jnp.concatenate([jnp.where(hm, q_in, 0.0) for hm in head_mask], axis=0).astype(BF16)
        dcol = jnp.sum(jnp.where(eye, jnp.exp(last), 0.0), axis=1, keepdims=True)
        v = v_refs[m][d][row0[u]:row0[u] + L, :].astype(BF16)
        return qs, k_out, k_end_t, dcol, v

    def products(u, p):
        qs, k_out, k_end_t, dcol, v = p
        upd = jnp.concatenate(
            [_dot(k_end_t[LIN_DK * h:LIN_DK * (h + 1)], v[:, LIN_DV * h:LIN_DV * (h + 1)]) for h in range(LIN_HEADS)],
            axis=0)
        return qs, _dot_nt(qs, k_out), upd, dcol, v

    def exact_scores(u):
        d = u[1]
        q, k, cum = gates(u, None)
        fb_ref[0], fb_ref[1], fb_ref[2] = q, k, cum
        head_sum = jnp.where((lax.broadcasted_iota(jnp.int32, (LIN_QK, 128), 0) >> 6)
                             == lax.broadcasted_iota(jnp.int32, (LIN_QK, 128), 1), 1.0, 0.0).astype(BF16)
        t_col = lax.broadcasted_iota(jnp.int32, (L, 1), 0)
        s_row = lax.broadcasted_iota(jnp.int32, (1, L), 1)

        def key_row(s, acc):
            ks = fb_ref[1, pl.ds(s, 1), :]
            cs = fb_ref[2, pl.ds(s, 1), :]
            w = fb_ref[0] * ks * jnp.exp(jnp.minimum(fb_ref[2] - cs, 0.0))
            cols = _dot_exact_rhs(w, head_sum)
            cols = jnp.where((t_col <= s) if d else (t_col >= s), cols, 0.0)
            return acc + jnp.concatenate(
                [jnp.where(s_row == s, cols[:, h:h + 1], 0.0) for h in range(LIN_HEADS)], axis=0)

        return lax.fori_loop(0, L, key_row, jnp.zeros((LIN_HEADS * L, L), F32)).astype(BF16)

    def mask(u, p):
        qs, a, upd, dcol, v = p
        a = exact_scores(u) if exact else jnp.where(tri4[u[1]], a, 0.0).astype(BF16)
        return qs, a, upd, dcol, v

    def intra(u, p):
        qs, a, upd, dcol, v = p
        o_intra = jnp.concatenate(
            [_dot(a[h * L:(h + 1) * L], v[:, LIN_DV * h:LIN_DV * (h + 1)]) for h in range(LIN_HEADS)], axis=0)
        return qs, o_intra, upd, dcol

    s = [st_ref[i] for i in range(4)]

    def outputs(u, p):
        m, d, _ = u
        qs, o_intra, upd, dcol = p
        i = 2 * m + d
        o = o_intra + _dot(qs, s[i].astype(BF16))
        o_refs[m][d][row0[u]:row0[u] + L, :] = jnp.concatenate(
            [o[h * L:(h + 1) * L] for h in range(LIN_HEADS)], axis=1).astype(BF16)
        s[i] = dcol * s[i] + upd

    _staged(units, (gates, scale, products, mask, intra, outputs))
    for i, n_ref in enumerate((ngf_ref, ngb_ref, nhf_ref, nhb_ref)):
        st_ref[i] = s[i]
        n_ref[0] = s[i]
    worst_ref[0] = jnp.broadcast_to(functools.reduce(jnp.minimum, lasts), (SUBLANES, LIN_QK))


def _lin(gqk, gv, hqf, hi, states, wa, ba, lbl):
    st_spec = pl.BlockSpec((1, LIN_QK, LIN_DV), lambda j: (_state_idx(j), 0, 0))
    so_spec = pl.BlockSpec((1, LIN_QK, LIN_DV), lambda j: (_seq_of(j)[0], 0, 0))
    so_shape = jax.ShapeDtypeStruct((N_SEQ, LIN_QK, LIN_DV), F32)
    o_shape = jax.ShapeDtypeStruct((N_TOK, LIN_V), BF16)
    worst_spec = pl.BlockSpec((1, SUBLANES, LIN_QK), lambda j: (j, 0, 0))
    worst_shape = jax.ShapeDtypeStruct((N_BLK, SUBLANES, LIN_QK), F32)
    packed = [s.reshape(DEC_BATCH, LIN_QK, LIN_DV) for s in states]

    def run(exact):
        return pl.pallas_call(
            functools.partial(_lin_kernel, exact=exact),
            grid=(N_BLK,),
            in_specs=[_fwd_spec(640), _fwd_spec(LIN_V), _bwd_spec(640), _bwd_spec(LIN_V),
                      _fwd_spec(768), _fwd_spec(LIN_V), _bwd_spec(768), _bwd_spec(LIN_V)]
                     + [st_spec] * 4 + [_const_spec(wa.shape), _const_spec(ba.shape), _const_spec(lbl.shape)],
            out_specs=[_fwd_spec(LIN_V), _bwd_spec(LIN_V), _fwd_spec(LIN_V), _bwd_spec(LIN_V)] + [so_spec] * 4
                      + [worst_spec],
            out_shape=[o_shape] * 4 + [so_shape] * 4 + [worst_shape],
            scratch_shapes=[pltpu.VMEM((4, LIN_QK, LIN_DV), F32), pltpu.VMEM((3, LIN_L, LIN_QK), F32)],
            compiler_params=_cparams(),
            name="lin_exact" if exact else "lin",
        )(gqk, gv, gqk, gv, hqf, hi, hqf, hi, *packed, wa, ba, lbl)

    fast = run(False)
    outs = lax.cond(jnp.min(fast[8]) < -LIN_SAFE_LOG_DECAY, lambda: tuple(run(True)[:8]), lambda: tuple(fast[:8]))
    unpack = lambda s: s[:BATCH].reshape(BATCH, LIN_HEADS, LIN_DK, LIN_DV)
    return outs[:4], [unpack(s) for s in outs[4:]]


def _f32(ref):
    return ref[...].astype(F32)


BF16_ROWS = 16


def _ext_rows(refs):
    t_ref, n_ref, p_ref = refs
    p = _f32(p_ref)
    return jnp.concatenate([_f32(t_ref), _f32(n_ref)[0:SUBLANES], p[p.shape[0] - SUBLANES:]], axis=0)


def _mix0(refs):
    of, ob, z, yb, (nw_ref,) = refs[0:3], refs[3:6], refs[6:9], refs[9:12], refs[12:]
    ya = (_ext_rows(of) + _ext_rows(ob)) * _silu(_ext_rows(z))
    ms = jnp.mean(ya * ya, axis=-1, keepdims=True)
    ya = ya * lax.rsqrt(ms + NORM_EPS) * nw_ref[...]
    return ya.astype(BF16), _ext_rows(yb).astype(BF16)


def _head_rms128(o, w):
    parts = []
    for h in range(LIN_HEADS):
        oh = o[:, LIN_DV * h:LIN_DV * (h + 1)]
        ms = jnp.mean(oh * oh, axis=-1, keepdims=True)
        parts.append(oh * lax.rsqrt(ms + NORM_EPS) * w)
    return jnp.concatenate(parts, axis=-1)


def _mix1(refs):
    gf, gb, gg, hf, hb, hg, (gw_ref, hw_ref) = (refs[0:3], refs[3:6], refs[6:9], refs[9:12], refs[12:15],
                                               refs[15:18], refs[18:])
    yc = _head_rms128(_ext_rows(gf) + _ext_rows(gb), gw_ref[...]) * _silu(_ext_rows(gg))
    yd = _head_rms128(_ext_rows(hf) + _ext_rows(hb), hw_ref[...]) * _silu(_ext_rows(hg))
    return yc.astype(BF16), yd.astype(BF16)


def _outffn_kernel(*refs, layer, n_out):
    i = pl.program_id(0)
    is_prompt = i < PROMPT_TILES
    if layer == 0:
        xp_ref, xs_ref, xn_ref, xv_ref = refs[0:4]
        x0 = jnp.concatenate([_tile_of(xp_ref, xs_ref), xn_ref[...], xv_ref[...]], axis=0)
        n_mix, mix = 13, _mix0
        refs = refs[4:]
    else:
        x0 = _ext_rows(refs[0:3])
        n_mix, mix = 20, _mix1
        refs = refs[3:]
    m_ref, refs = refs[0], refs[1:]
    ya, yb = mix(refs[:n_mix])
    wo_ref, nw_ref, wu_ref, cw_ref, cb_ref, wd_ref = refs[n_mix:n_mix + 6]
    rest = refs[n_mix + 6:]
    o_refs, (hs_ref, act_ref, x1_ref) = rest[:n_out], rest[n_out:]

    x1 = x0 + m_ref[0, 2:3, :] * (_dot(ya, wo_ref[0:512, :]) + _dot(yb, wo_ref[512:1024, :]))
    x1_ref[...] = x1[0:TM]
    nw = nw_ref[...]
    sh = m_ref[0, 3:4, :]
    sc = m_ref[0, 4:5, :]
    hs_ref[0:TM, :] = _norm_mod(x1[0:TM], nw, sh, sc).astype(BF16)
    hs_ref[TM:TM_EXT, :] = (_norm_mod(x1[TM:TM_EXT], nw, sh, sc) * _halo_keep(i)).astype(BF16)

    def conv_cols(c0):
        cs = slice(c0, c0 + FFN_CH)
        return _dwconv_tile(_dot(hs_ref[...], wu_ref[:, cs]), cw_ref[:, cs], cb_ref[:, cs], i)

    for c in range(FFN_DIM // FFN_CH):
        a = conv_cols(c * FFN_CH)
        b = conv_cols(FFN_DIM + c * FFN_CH)
        act_ref[:, c * FFN_CH:(c + 1) * FFN_CH] = (_silu(a) * b).astype(BF16)
    out = x1_ref[...] + m_ref[0, 5:6, :] * _dot(act_ref[...], wd_ref[...])
    if len(o_refs) == 1:
        o_refs[0][...] = out
    else:
        @pl.when(is_prompt)
        def _():
            o_refs[0][...] = out

        @pl.when(jnp.logical_not(is_prompt))
        def _():
            o_refs[1][...] = out


def _outffn(x_args, mix_arrays, mix_params, mrows, w_out, nw, wu, cw, cb, wd, layer):
    per_layer = lambda *shape: pl.BlockSpec((None,) + shape, lambda i: (layer,) + (0,) * len(shape))
    single = dict(pipeline_mode=pl.Buffered(1))
    if layer == 0:
        xp, xs = x_args
        x_ops = [xp, xs, xs, xs]
        x_specs = [_prompt_tile_spec(), _sample_tile_spec()] + _halo_specs(D_MODEL, SUBLANES, N_SAMPLE, PROMPT_TILES)
        out_specs = [_row_spec(D_MODEL)]
        out_shape = [jax.ShapeDtypeStruct((N_TOK, D_MODEL), F32)]
    else:
        x_ops = [x_args[0]] * 3
        x_specs = [_row_spec(D_MODEL)] + _halo_specs(D_MODEL, SUBLANES, N_TOK)
        out_specs = [_prompt_tile_spec(), _sample_tile_spec()]
        out_shape = [jax.ShapeDtypeStruct((N_PROMPT, D_MODEL), F32), jax.ShapeDtypeStruct((N_SAMPLE, D_MODEL), F32)]
    mix_ops, mix_specs = [], []
    for a in mix_arrays:
        mix_ops += [a] * 3
        mix_specs += [_row_spec(a.shape[1])] + _halo_specs(a.shape[1], BF16_ROWS, N_TOK)
    return pl.pallas_call(
        functools.partial(_outffn_kernel, layer=layer, n_out=len(out_specs)),
        grid=(N_TILES,),
        in_specs=x_specs + [_mod_spec()] + mix_specs + [_const_spec(p.shape) for p in mix_params]
                 + [pl.BlockSpec((D_MODEL, D_MODEL), lambda i: (0, 0), **single), _const_spec((1, D_MODEL)),
                    pl.BlockSpec((None, D_MODEL, 2 * FFN_DIM), lambda i: (layer, 0, 0), **single),
                    per_layer(3, 2 * FFN_DIM), per_layer(1, 2 * FFN_DIM),
                    pl.BlockSpec((None, FFN_DIM, D_MODEL), lambda i: (layer, 0, 0), **single)],
        out_specs=out_specs,
        out_shape=out_shape,
        scratch_shapes=[pltpu.VMEM((TM_EXT, D_MODEL), BF16), pltpu.VMEM((TM, FFN_DIM), BF16),
                        pltpu.VMEM((TM, D_MODEL), F32)],
        compiler_params=_cparams(),
        name="outffn",
    )(*x_ops, mrows, *mix_ops, *mix_params, w_out, nw, wu, cw, cb, wd)


def _pad_lanes(v, width=128):
    v = v.reshape(1, -1)
    return jnp.pad(v, ((0, 0), (0, width - v.shape[1])))


def kernel(x_prompt, x_sample, cache_na_k_l0, cache_na_v_l0, state_ssd_fwd_l0, state_ssd_bwd_l0,
           state_gla_fwd_l1, state_gla_bwd_l1, state_hgrn_fwd_l1, state_hgrn_bwd_l1, c,
           c_ctx, w_ada, b_ada, norm_w, ffn_w_up, ffn_conv_w, ffn_conv_b, ffn_w_down,
           w_in_l0, w_out_l0, ssd_conv_w_l0, ssd_conv_b_l0, ssd_dt_bias_l0, ssd_a_log_l0, ssd_d_l0,
           ssd_norm_w_l0, na_q_norm_l0, na_k_norm_l0, na_rpb_l0,
           w_in_l1, w_out_l1, gla_wa2_l1, gla_ba2_l1, gla_norm_w_l1, hgrn_lb_logits, hgrn_norm_w_l1):
    xp = x_prompt.reshape(N_PROMPT, D_MODEL)
    xs = x_sample.reshape(N_SAMPLE, D_MODEL)

    cvec8 = jnp.zeros((SUBLANES, D_MODEL), F32).at[0:DEC_BATCH].set(c).at[CTX_MOD_ROW].set(c_ctx)
    mods = _mods(cvec8, w_ada, b_ada)
    mods = mods.reshape(2, SUBLANES, 6, D_MODEL)
    mods = jnp.pad(mods, ((0, 0), (0, 0), (0, SUBLANES - 6), (0, 0)))

    zpad = lambda n: jnp.zeros((D_MODEL, n), BF16)
    w0b, w1b = w_in_l0.astype(BF16), w_in_l1.astype(BF16)
    w0 = jnp.concatenate([w0b[:, :1296], zpad(112), w0b[:, 1296:]], axis=1)
    w1 = jnp.concatenate([w1b[:, 0:512], w1b[:, 1536:1568], zpad(96), w1b[:, 512:1536], w1b[:, 1568:3360]], axis=1)
    seg = jnp.kron(jnp.eye(NA_HEADS, dtype=F32), jnp.full((NA_HEAD_DIM, NA_HEAD_DIM), 1.0 / NA_HEAD_DIM, F32)).astype(BF16)

    m0 = mods[0]
    z, xc, dt, qn, knb, vb, kn, v = _inproj0(
        xp, xs, m0, norm_w[0, 0].reshape(1, D_MODEL), w0, seg,
        jnp.tile(na_q_norm_l0, NA_HEADS).reshape(1, NA_WIDTH), jnp.tile(na_k_norm_l0, NA_HEADS).reshape(1, NA_WIDTH),
        ssd_conv_w_l0, ssd_conv_b_l0.reshape(1, SSD_XBC))
    of, ob, ssd_f, ssd_b = _ssd(
        xc, dt, state_ssd_fwd_l0, state_ssd_bwd_l0, _pad_lanes(ssd_dt_bias_l0), _pad_lanes(ssd_a_log_l0),
        jnp.repeat(ssd_d_l0, SSD_HEAD_DIM).reshape(1, SSD_WIDTH))
    yb, na_k, na_v = _na_ctx(qn, kn, v)
    yb = _na_lat(qn, knb, vb, cache_na_k_l0, cache_na_v_l0, _na_bias_table(na_rpb_l0), yb)
    ffn_weights = (ffn_w_up.astype(BF16), ffn_conv_w, ffn_conv_b[:, None, :], ffn_w_down.astype(BF16))
    x, = _outffn((xp, xs), (of, ob, z, yb), (ssd_norm_w_l0.reshape(1, SSD_WIDTH),), m0, w_out_l0.astype(BF16),
                 norm_w[0, 1].reshape(1, D_MODEL), *ffn_weights, layer=0)

    m1 = mods[1]
    gqk, gv, gg, hqf, hi, hg = _inproj1(x, m1, norm_w[1, 0].reshape(1, D_MODEL), w1)
    wa_pad = jnp.zeros((2, 128, LIN_QK), F32)
    wa_pad = wa_pad.at[0, 0:GLA_RANK].set(gla_wa2_l1[0]).at[1, GLA_RANK:2 * GLA_RANK].set(gla_wa2_l1[1])
    (gf, gb, hf, hb), (gla_f, gla_b, hgrn_f, hgrn_b) = _lin(
        gqk, gv, hqf, hi, (state_gla_fwd_l1, state_gla_bwd_l1, state_hgrn_fwd_l1, state_hgrn_bwd_l1),
        wa_pad, gla_ba2_l1.reshape(2, 1, LIN_QK), hgrn_lb_logits)
    y_p, y_s = _outffn((x,), (gf, gb, gg, hf, hb, hg),
                       (gla_norm_w_l1.reshape(1, LIN_DV), hgrn_norm_w_l1.reshape(1, LIN_DV)), m1,
                       w_out_l1.astype(BF16), norm_w[1, 1].reshape(1, D_MODEL), *ffn_weights, layer=1)
    return (y_p.reshape(BATCH, SEQ, D_MODEL), y_s.reshape(DEC_BATCH, DEC_SEQ, D_MODEL),
            na_k, na_v, ssd_f, ssd_b, gla_f, gla_b, hgrn_f, hgrn_b)
```

```python
import functools

import jax
import jax.numpy as jnp
from jax import lax
from jax.experimental import pallas as pl
from jax.experimental.pallas import tpu as pltpu

F32 = jnp.float32
BF16 = jnp.bfloat16
HIGHEST = lax.Precision.HIGHEST

D_MODEL = 1024
BATCH = 32
SEQ = 256
DEC_BATCH = 4
DEC_SEQ = 4096
PAST_LEN = 256
GRID_W = 64
NORM_EPS = 1e-6
N_PROMPT = BATCH * SEQ
N_SAMPLE = DEC_BATCH * DEC_SEQ
N_TOK = N_PROMPT + N_SAMPLE

SSD_HEADS = 8
SSD_HEAD_DIM = 64
SSD_STATE = 64
SSD_WIDTH = 512
SSD_BC = 128
SSD_XBC = 768
NA_HEADS = 8
NA_HEAD_DIM = 64
NA_WIDTH = 512
NA_WIN_ROWS = 8
NA_WIN_COLS = 16
NA_SCALE = NA_HEAD_DIM ** -0.5
GLA_HEADS = 4
GLA_DK = 64
GLA_DV = 128
GLA_RANK = 16
GLA_GATE_NORM = 16.0
HGRN_HEADS = 4
FFN_DIM = 2816

V7X_VMEM_BYTES = 64 * 1024 * 1024
VMEM_LIMIT = 56 * 1024 * 1024
SUBLANES = 8

TM = 512
N_TILES = N_TOK // TM
PROMPT_TILES = N_PROMPT // TM
TILES_PER_SAMPLE = DEC_SEQ // TM
CTX_MOD_ROW = DEC_BATCH

RB = 256
N_BLK = N_TOK // RB
PROMPT_BLKS = N_PROMPT // RB
BLKS_PER_SAMPLE = DEC_SEQ // RB
N_SEQ = BATCH + DEC_BATCH
SSD_L = 128
LIN_L = 64
LIN_SAFE_LOG_DECAY = 60.0
NA_R = 8
NA_ROWS_PER_ITER = 8
FFN_CH = 256
W0_COLS = 2944
W1_COLS = 3456


def _cparams(n_axes=1):
    return pltpu.CompilerParams(dimension_semantics=("arbitrary",) * n_axes,
                                vmem_limit_bytes=VMEM_LIMIT)


def _const_spec(shape):
    nd = len(shape)
    return pl.BlockSpec(shape, lambda *_: (0,) * nd)


def _sigmoid(x):
    return 1.0 / (1.0 + jnp.exp(-x))


def _silu(x):
    return x * _sigmoid(x)


def _softplus(x):
    return jnp.maximum(x, 0.0) + jnp.log(1.0 + jnp.exp(-jnp.abs(x)))


def _mod_row(i):
    return jnp.where(i < PROMPT_TILES, CTX_MOD_ROW, (i - PROMPT_TILES) // TILES_PER_SAMPLE)


def _dot(a, b, **kw):
    return jnp.dot(a, b, preferred_element_type=F32, **kw)


def _dot_nt(a, b):
    return lax.dot_general(a, b, (((1,), (1,)), ((), ())), preferred_element_type=F32)


def _split2(x):
    hi = x.astype(BF16)
    lo = (x - hi.astype(F32)).astype(BF16)
    return hi, lo


def _split3(x):
    hi = x.astype(BF16)
    r = x - hi.astype(F32)
    mid = r.astype(BF16)
    lo = (r - mid.astype(F32)).astype(BF16)
    return hi, mid, lo


def _dot_exact_lhs(a_bf, x):
    hi, lo = _split2(x)
    return _dot(a_bf, hi) + _dot(a_bf, lo)


def _dot_exact_rhs(x, b_bf):
    hi, lo = _split2(x)
    return _dot(hi, b_bf) + _dot(lo, b_bf)


def _skewed(units, stages):
    results = {}
    for step in range(len(units) + len(stages) - 1):
        for k, stage in enumerate(stages):
            i = step - k
            if 0 <= i < len(units):
                results[(k, i)] = stage(units[i], results.pop((k - 1, i), None))


def _staged(units, stages):
    results = {i: None for i in range(len(units))}
    for stage in stages:
        for i, u in enumerate(units):
            results[i] = stage(u, results[i])


def _dot_tn(a, b):
    return lax.dot_general(a, b, (((0,), (0,)), ((), ())), preferred_element_type=F32)


MODS_NB = 1536


def _mods_kernel(c_ref, w_ref, b_ref, o_ref):
    s_hi, s_lo = _split2(_silu(c_ref[...]))
    w_hi, w_lo = _split2(w_ref[0])
    o_ref[0] = _dot(s_hi, w_hi) + _dot(s_lo, w_hi) + _dot(s_hi, w_lo) + b_ref[0]


def _mods(cvec8, w_ada, b_ada):
    depth = w_ada.shape[0]
    nb = 6 * D_MODEL // MODS_NB
    return pl.pallas_call(
        _mods_kernel,
        grid=(depth, nb),
        in_specs=[
            _const_spec((SUBLANES, D_MODEL)),
            pl.BlockSpec((1, D_MODEL, MODS_NB), lambda l, j: (l, 0, j)),
            pl.BlockSpec((1, 1, MODS_NB), lambda l, j: (l, 0, j)),
        ],
        out_specs=pl.BlockSpec((1, SUBLANES, MODS_NB), lambda l, j: (l, 0, j)),
        out_shape=jax.ShapeDtypeStruct((depth, SUBLANES, 6 * D_MODEL), F32),
        compiler_params=_cparams(2),
        name="mods",
    )(cvec8, w_ada, b_ada.reshape(depth, 1, 6 * D_MODEL))


def _norm_mod(x, nw, sh, sc):
    ms = jnp.mean(x * x, axis=-1, keepdims=True)
    y = x * lax.rsqrt(ms + NORM_EPS) * nw
    return y * (1.0 + sc) + sh


def _head_rms(x, seg, w):
    ms = _dot((x * x).astype(BF16), seg)
    return x * lax.rsqrt(ms + NORM_EPS) * w


def _tile_of(xp_ref, xs_ref):
    return jnp.where(pl.program_id(0) < PROMPT_TILES, xp_ref[...], xs_ref[...])


HALO_ROWS = 2 * SUBLANES
TM_EXT = TM + HALO_ROWS


def _halo_keep(i):
    seq = jnp.where(i < PROMPT_TILES, SEQ, DEC_SEQ)
    keep_prev = jnp.where(((i * TM) & (seq - 1)) == 0, 0.0, 1.0)
    keep_next = jnp.where(((i * TM + TM) & (seq - 1)) == 0, 0.0, 1.0)
    hrow = lax.broadcasted_iota(jnp.int32, (HALO_ROWS, 1), 0)
    return jnp.where(hrow < SUBLANES, keep_next, keep_prev)


def _dwconv_tile(u_all, w, bias, i):
    um = pltpu.roll(u_all, 1, axis=0)[0:TM] * w[0:1]
    up = pltpu.roll(u_all, TM_EXT - 1, axis=0)[0:TM] * w[2:3]
    out = bias + um + u_all[0:TM] * w[1:2] + up
    edge_on = jnp.where(i < PROMPT_TILES, 1.0, 0.0)
    erow = lax.broadcasted_iota(jnp.int32, (HALO_ROWS, 1), 0)
    at_first = jnp.where(erow == SUBLANES, edge_on, 0.0)
    at_last = jnp.where(erow == SUBLANES - 1, edge_on, 0.0)
    pieces, r = [], 0
    for edge in range(SEQ, TM, SEQ):
        sl = slice(edge - SUBLANES, edge + SUBLANES)
        pieces += [out[r:edge - SUBLANES], out[sl] - at_first * um[sl] - at_last * up[sl]]
        r = edge + SUBLANES
    return jnp.concatenate(pieces + [out[r:TM]], axis=0)


def _inproj0_kernel(xp_ref, xs_ref, xn_ref, xv_ref, m_ref, nw_ref, w_ref, seg_ref, qw_ref, kw_ref, cw_ref, cb_ref,
                    z_ref, xc_ref, dt_ref, qn_ref, knb_ref, vb_ref, kn_ref, v_ref, hs_ref):
    i = pl.program_id(0)
    nw, sh, sc = nw_ref[...], m_ref[0, 0:1, :], m_ref[0, 1:2, :]
    hs_ref[0:TM, :] = _norm_mod(_tile_of(xp_ref, xs_ref), nw, sh, sc).astype(BF16)
    halo = _norm_mod(jnp.concatenate([xn_ref[...], xv_ref[...]], axis=0), nw, sh, sc)
    hs_ref[TM:TM_EXT, :] = (halo * _halo_keep(i)).astype(BF16)
    q = _dot(hs_ref[0:TM, :], w_ref[:, 1408:1920])
    k = _dot(hs_ref[0:TM, :], w_ref[:, 1920:2432])
    xc_ref[...] = _silu(_dwconv_tile(_dot(hs_ref[...], w_ref[:, 512:1280]), cw_ref[...], cb_ref[...], i))
    ms_q = _dot((q * q).astype(BF16), seg_ref[...])
    ms_k = _dot((k * k).astype(BF16), seg_ref[...])
    z_ref[...] = _dot(hs_ref[0:TM, :], w_ref[:, 0:512]).astype(BF16)
    dt_ref[...] = _dot(hs_ref[0:TM, :], w_ref[:, 1280:1408])
    v = _dot(hs_ref[0:TM, :], w_ref[:, 2432:2944])
    vb_ref[...] = v.astype(BF16)
    qn_ref[...] = (q * lax.rsqrt(ms_q + NORM_EPS) * qw_ref[...] * NA_SCALE).astype(BF16)
    kn = k * lax.rsqrt(ms_k + NORM_EPS) * kw_ref[...]
    knb_ref[...] = kn.astype(BF16)

    @pl.when(pl.program_id(0) < PROMPT_TILES)
    def _():
        kn_ref[...] = kn
        v_ref[...] = v


def _row_spec(width):
    return pl.BlockSpec((TM, width), lambda i: (i, 0))


def _mod_spec():
    return pl.BlockSpec((1, SUBLANES, D_MODEL), lambda i: (_mod_row(i), 0, 0))


def _prompt_tile_spec(width=D_MODEL):
    return pl.BlockSpec((TM, width), lambda i: (jnp.minimum(i, PROMPT_TILES - 1), 0))


def _sample_tile_spec(width=D_MODEL):
    return pl.BlockSpec((TM, width), lambda i: (jnp.maximum(i - PROMPT_TILES, 0), 0))


def _halo_specs(width, rows, total_rows, first_tile=0):
    per_tile = TM // rows
    last = total_rows // rows - 1
    nxt = pl.BlockSpec((rows, width), lambda i: (jnp.clip((i - first_tile + 1) * per_tile, 0, last), 0))
    prv = pl.BlockSpec((rows, width), lambda i: (jnp.clip((i - first_tile) * per_tile - 1, 0, last), 0))
    return [nxt, prv]


def _inproj0(xp, xs, mrows, nw, w0, seg, qw, kw, conv_w, conv_b):
    widths = (512, 768, 128, 512, 512, 512)
    dtypes = (BF16, F32, F32, BF16, BF16, BF16)
    nxt, prv = _halo_specs(D_MODEL, SUBLANES, N_SAMPLE, PROMPT_TILES)
    return pl.pallas_call(
        _inproj0_kernel,
        grid=(N_TILES,),
        in_specs=[
            _prompt_tile_spec(), _sample_tile_spec(), nxt, prv, _mod_spec(), _const_spec((1, D_MODEL)),
            _const_spec((D_MODEL, W0_COLS)), _const_spec((512, 512)),
            _const_spec((1, 512)), _const_spec((1, 512)), _const_spec((3, SSD_XBC)), _const_spec((1, SSD_XBC)),
        ],
        out_specs=[_row_spec(w) for w in widths] + [_prompt_tile_spec(NA_WIDTH)] * 2,
        out_shape=[jax.ShapeDtypeStruct((N_TOK, w), d) for w, d in zip(widths, dtypes)]
                  + [jax.ShapeDtypeStruct((N_PROMPT, NA_WIDTH), F32)] * 2,
        scratch_shapes=[pltpu.VMEM((TM_EXT, D_MODEL), BF16)],
        compiler_params=_cparams(),
        name="inproj0",
    )(xp, xs, xs, xs, mrows, nw, w0, seg, qw, kw, conv_w, conv_b)


def _inproj1_kernel(x_ref, m_ref, nw_ref, w_ref, gqk_ref, gv_ref, gg_ref, hqf_ref, hi_ref, hg_ref, hs_ref):
    hs_ref[...] = _norm_mod(x_ref[...], nw_ref[...], m_ref[0, 0:1, :], m_ref[0, 1:2, :]).astype(BF16)
    gqk_ref[...] = _dot(hs_ref[...], w_ref[:, 0:640])
    gv_ref[...] = _dot(hs_ref[...], w_ref[:, 640:1152]).astype(BF16)
    gg_ref[...] = _dot(hs_ref[...], w_ref[:, 1152:1664]).astype(BF16)
    hqf_ref[...] = _dot(hs_ref[...], w_ref[:, 1664:2432])
    hi_ref[...] = _dot(hs_ref[...], w_ref[:, 2432:2944]).astype(BF16)
    hg_ref[...] = _dot(hs_ref[...], w_ref[:, 2944:3456]).astype(BF16)


def _inproj1(x, mrows, nw, w1):
    widths = (640, 512, 512, 768, 512, 512)
    dtypes = (F32, BF16, BF16, F32, BF16, BF16)
    return pl.pallas_call(
        _inproj1_kernel,
        grid=(N_TILES,),
        in_specs=[_row_spec(D_MODEL), _mod_spec(), _const_spec((1, D_MODEL)),
                  _const_spec((D_MODEL, W1_COLS))],
        out_specs=[_row_spec(w) for w in widths],
        out_shape=[jax.ShapeDtypeStruct((N_TOK, w), d) for w, d in zip(widths, dtypes)],
        scratch_shapes=[pltpu.VMEM((TM, D_MODEL), BF16)],
        compiler_params=_cparams(),
        name="inproj1",
    )(x, mrows, nw, w1)


def _seq_of(j):
    jj = j - PROMPT_BLKS
    is_p = j < PROMPT_BLKS
    seq = jnp.where(is_p, j, BATCH + jj // BLKS_PER_SAMPLE)
    blk = jnp.where(is_p, 0, jj % BLKS_PER_SAMPLE)
    nblk = jnp.where(is_p, 1, BLKS_PER_SAMPLE)
    return seq, blk, nblk


def _bwd_blk(j):
    _, blk, nblk = _seq_of(j)
    return j - blk + (nblk - 1 - blk)


def _state_idx(j):
    seq, _, _ = _seq_of(j)
    return jnp.maximum(seq - BATCH, 0)


def _fwd_spec(width):
    return pl.BlockSpec((RB, width), lambda j: (j, 0))


def _bwd_spec(width):
    return pl.BlockSpec((RB, width), lambda j: (_bwd_blk(j), 0))


def _tri(n, upper):
    r = lax.broadcasted_iota(jnp.int32, (n, n), 0)
    c = lax.broadcasted_iota(jnp.int32, (n, n), 1)
    return (c >= r) if upper else (c <= r)


def _diag_blocks(x, rows):
    lh = lax.broadcasted_iota(jnp.int32, (1, 4 * SSD_HEAD_DIM), 1) >> 6
    out = x[3 * rows:4 * rows]
    for h in (2, 1, 0):
        out = jnp.where(lh == h, x[h * rows:(h + 1) * rows], out)
    return out


def _ssd_kernel(xf_ref, dtf_ref, xb_ref, dtb_ref, s0f_ref, s0b_ref, dtbias_ref, alog_ref, dsk_ref, ex_ref,
                of_ref, ob_ref, sf_ref, sb_ref, st_ref):
    j = pl.program_id(0)
    _, blk, _ = _seq_of(j)
    is_prompt = j < PROMPT_BLKS

    @pl.when(blk == 0)
    def _():
        st_ref[0] = jnp.where(is_prompt, 0.0, s0f_ref[0])
        st_ref[1] = jnp.where(is_prompt, 0.0, s0b_ref[0])

    a_neg = -jnp.exp(alog_ref[...])
    a_col = jnp.broadcast_to(a_neg, (SUBLANES, 128)).T
    L = SSD_L
    nch = RB // L
    xc_refs = (xf_ref, xb_ref)
    dt_refs = (dtf_ref, dtb_ref)
    o_refs = (of_ref, ob_ref)

    units = [(d, c) for c in range(nch) for d in range(2)]
    row0 = {(d, c): ((nch - 1 - c) * L if d else c * L) for d, c in units}
    tri = [_tri(L, False), _tri(L, True)]
    tri_bf = [jnp.where(t, 1.0, 0.0).astype(BF16) for t in tri]
    lane = lax.broadcasted_iota(jnp.int32, (1, SSD_BC), 1)
    gmask = [lane < SSD_STATE, lane >= SSD_STATE]
    eye_bf = jnp.where(tri[0] & tri[1], 1.0, 0.0).astype(BF16)
    zero_blk = jnp.zeros((SSD_STATE, 4 * SSD_HEAD_DIM), F32)

    def gates(u, _):
        d, r0 = u[0], row0[u]
        dt = _softplus(dt_refs[d][r0:r0 + L, :] + dtbias_ref[...])
        cum = _dot_exact_lhs(tri_bf[d], dt * a_neg)
        dt8 = dt.T[8 * d:8 * d + 8]
        ct8 = _dot_exact_rhs(dt8 * a_col[8 * d:8 * d + 8, 0:1], tri_bf[1 - d])
        parts = _split3(cum[0:SUBLANES] if d else cum[L - SUBLANES:L])
        last_x = _dot(parts[0], ex_ref[d]) + _dot(parts[1], ex_ref[d]) + _dot(parts[2], ex_ref[d])
        e_last_x = jnp.exp(last_x[0:1] if d else last_x[SUBLANES - 1:SUBLANES])
        last_col = ct8[:, 0:1] if d else ct8[:, L - 1:L]
        f_end = jnp.exp(last_col - ct8) * dt8
        bm = xc_refs[d][r0:r0 + L, 512:640].astype(BF16)
        cm = xc_refs[d][r0:r0 + L, 640:768]
        b_t = _dot_nt(eye_bf, bm)
        cb_g = [_dot_nt(jnp.where(gmask[g], cm, 0.0).astype(BF16), bm) for g in range(2)]
        return cum, ct8, dt8, e_last_x, f_end, b_t, cb_g, cm

    def operands(u, p):
        d = u[0]
        cum, ct8, dt8, e_last_x, f_end, b_t, cb_g, cm = p
        lhs, k_t = [], []
        for g in range(2):
            ls, ks = [], []
            for hh in range(4):
                h = 4 * g + hh
                pc = jnp.broadcast_to(cum[:, 8 * d + h:8 * d + h + 1], (L, L))
                dec = jnp.exp(jnp.where(tri[d], pc - ct8[h:h + 1, :], -jnp.inf))
                w = cb_g[g] * dec * dt8[h:h + 1, :]
                ls.append(jnp.concatenate([w.astype(BF16), (cm * jnp.exp(pc)).astype(BF16)], axis=1))
                ks.append((b_t[SSD_STATE * g:SSD_STATE * (g + 1)] * f_end[h:h + 1, :]).astype(BF16))
            lhs.append(jnp.concatenate(ls, axis=0))
            k_t.append(jnp.concatenate(ks, axis=0))
        return lhs, k_t, e_last_x

    def state_update(u, p):
        d, r0 = u[0], row0[u]
        lhs, k_t, e_last_x = p
        x_bf = [xc_refs[d][r0:r0 + L, 256 * g:256 * (g + 1)].astype(BF16) for g in range(2)]
        up = [_diag_blocks(_dot(k_t[g], x_bf[g]), SSD_STATE) for g in range(2)]
        upd = jnp.concatenate([jnp.concatenate([up[0], zero_blk], axis=1),
                               jnp.concatenate([zero_blk, up[1]], axis=1)], axis=0)
        return lhs, x_bf, e_last_x, upd

    s = [st_ref[0], st_ref[1]]

    def outputs(u, p):
        d, r0 = u[0], row0[u]
        lhs, x_bf, e_last_x, upd = p
        s_bf = s[d].astype(BF16)
        o = jnp.concatenate(
            [_diag_blocks(_dot(lhs[g], jnp.concatenate([x_bf[g], s_bf[:, 256 * g:256 * (g + 1)]], axis=0)), L)
             for g in range(2)], axis=1)
        if d == 0:
            o = o + dsk_ref[...] * xf_ref[r0:r0 + L, 0:SSD_WIDTH]
        o_refs[d][r0:r0 + L, :] = o.astype(BF16)
        s[d] = e_last_x * s[d] + upd

    _skewed(units, (gates, operands, state_update, outputs))
    st_ref[0] = s[0]
    st_ref[1] = s[1]

    @pl.when(is_prompt)
    def _():
        sf_ref[0] = s[0]
        sb_ref[0] = s[1]


def _ssd_pack_state(s):
    b = s.shape[0]
    g = s.reshape(b, 2, 4, SSD_STATE, SSD_HEAD_DIM).transpose(0, 1, 3, 2, 4).reshape(b, 2, SSD_STATE, 256)
    z = jnp.zeros((b, SSD_STATE, 256), F32)
    return jnp.concatenate([jnp.concatenate([g[:, 0], z], axis=2), jnp.concatenate([z, g[:, 1]], axis=2)], axis=1)


def _ssd_unpack_state(s):
    b = s.shape[0]
    g = jnp.stack([s[:, 0:SSD_STATE, 0:256], s[:, SSD_STATE:, 256:512]], axis=1)
    g = g.reshape(b, 2, SSD_STATE, 4, SSD_HEAD_DIM).transpose(0, 1, 3, 2, 4)
    return g.reshape(b, SSD_HEADS, SSD_STATE, SSD_HEAD_DIM)


def _ssd(xc, dt, s0f, s0b, dtbias, alog, dsk):
    st_shape = (1, 2 * SSD_STATE, SSD_WIDTH)
    st_spec = pl.BlockSpec(st_shape, lambda j: (_state_idx(j), 0, 0))
    so_spec = pl.BlockSpec(st_shape, lambda j: (jnp.minimum(j, BATCH - 1), 0, 0))
    so_shape = jax.ShapeDtypeStruct((BATCH, 2 * SSD_STATE, SSD_WIDTH), F32)
    col = jnp.arange(128)[:, None]
    lane_head = jnp.arange(SSD_WIDTH)[None, :] // SSD_HEAD_DIM
    expand = jnp.stack([col == lane_head, col == SSD_HEADS + lane_head]).astype(BF16)
    of, ob, sf, sb = pl.pallas_call(
        _ssd_kernel,
        grid=(N_BLK,),
        in_specs=[
            _fwd_spec(SSD_XBC), _fwd_spec(128), _bwd_spec(SSD_XBC), _bwd_spec(128), st_spec, st_spec,
            _const_spec((1, 128)), _const_spec((1, 128)), _const_spec((1, SSD_WIDTH)),
            _const_spec((2, 128, SSD_WIDTH)),
        ],
        out_specs=[_fwd_spec(SSD_WIDTH), _bwd_spec(SSD_WIDTH), so_spec, so_spec],
        out_shape=[jax.ShapeDtypeStruct((N_TOK, SSD_WIDTH), BF16)] * 2 + [so_shape, so_shape],
        scratch_shapes=[pltpu.VMEM((2, 2 * SSD_STATE, SSD_WIDTH), F32)],
        compiler_params=_cparams(),
        name="ssd",
    )(xc, dt, xc, dt, _ssd_pack_state(s0f), _ssd_pack_state(s0b), dtbias, alog, dsk, expand)
    return of, ob, _ssd_unpack_state(sf), _ssd_unpack_state(sb)


NA_PAIRS = NA_HEADS // 2


def _stack_pair(qt):
    lower = lax.broadcasted_iota(jnp.int32, (1, 2 * NA_HEAD_DIM), 1) < NA_HEAD_DIM
    zero = jnp.zeros_like(qt)
    return jnp.concatenate([jnp.where(lower, qt, zero), jnp.where(lower, zero, qt)], axis=0)


def _unstack_pair(x, n):
    lower = lax.broadcasted_iota(jnp.int32, (1, 2 * NA_HEAD_DIM), 1) < NA_HEAD_DIM
    return jnp.where(lower, x[0:n], x[n:2 * n])


def _na_ctx_kernel(q_ref, k_ref, v_ref, y_ref, ko_ref, vo_ref):
    tiles = [slice(128 * i, 128 * (i + 1)) for i in range(NA_PAIRS)]
    s = [_dot_nt(_stack_pair(q_ref[:, ts]), k_ref[:, ts].astype(BF16)) for ts in tiles]
    p, l = [], []
    for i in range(NA_PAIRS):
        e = jnp.exp(s[i] - jnp.max(s[i], axis=-1, keepdims=True))
        l.append(jnp.sum(e, axis=-1, keepdims=True))
        p.append(e.astype(BF16))
    o = [_dot(p[i], v_ref[:, tiles[i]].astype(BF16)) for i in range(NA_PAIRS)]
    for i in range(NA_PAIRS):
        y = _unstack_pair(o[i], SEQ) / _unstack_pair(jnp.broadcast_to(l[i], o[i].shape), SEQ)
        y_ref[:, tiles[i]] = y.astype(BF16)
    for h in range(NA_HEADS):
        sl = slice(NA_HEAD_DIM * h, NA_HEAD_DIM * (h + 1))
        ko_ref[0, h] = k_ref[:, sl]
        vo_ref[0, h] = v_ref[:, sl]


def _na_ctx(qn, kn, v):
    blk = lambda w: pl.BlockSpec((SEQ, w), lambda b: (b, 0))
    hm = pl.BlockSpec((1, NA_HEADS, SEQ, NA_HEAD_DIM), lambda b: (b, 0, 0, 0))
    hm_shape = jax.ShapeDtypeStruct((BATCH, NA_HEADS, SEQ, NA_HEAD_DIM), F32)
    return pl.pallas_call(
        _na_ctx_kernel,
        grid=(BATCH,),
        in_specs=[blk(NA_WIDTH)] * 3,
        out_specs=[blk(NA_WIDTH), hm, hm],
        out_shape=[jax.ShapeDtypeStruct((N_TOK, NA_WIDTH), BF16), hm_shape, hm_shape],
        compiler_params=_cparams(),
        name="na_ctx",
    )(qn, kn, v)


GRID_ROWS = DEC_SEQ // GRID_W
NA_LOC = NA_WIN_ROWS * GRID_W
NA_MASKED = -1e30


def _na_lat_kernel(q_ref, k_ref, v_ref, kc_ref, vc_ref, bt_ref, yin_ref, y_ref):
    del yin_ref
    rb = pl.program_id(1)

    tiles = [slice(128 * i, 128 * (i + 1)) for i in range(NA_PAIRS)]

    def rows(it, carry):
        units, q0, k0, dr0 = [], {}, {}, {}
        for rr_ in range(NA_ROWS_PER_ITER):
            r = it * NA_ROWS_PER_ITER + rr_
            rr = rb * NA_R + r
            rs = jnp.clip(rr - NA_WIN_ROWS // 2, 0, GRID_ROWS - NA_WIN_ROWS)
            dr0[rr_] = rs - rr + (NA_WIN_ROWS - 1)
            q0[rr_] = pl.multiple_of(r * GRID_W, GRID_W)
            k0[rr_] = pl.multiple_of(rs * GRID_W, GRID_W)
            units += [(rr_, i) for i in range(NA_PAIRS)]
        def scores(u, _):
            r, i = u
            qq = _stack_pair(q_ref[pl.ds(q0[r], GRID_W), tiles[i]])
            s_loc = (_dot_nt(qq, k_ref[pl.ds(k0[r], NA_LOC), tiles[i]])
                     + bt_ref[dr0[r], 2 * i:2 * i + 2].reshape(2 * GRID_W, NA_LOC))
            return s_loc, _dot_nt(qq, kc_ref[0, :, tiles[i]])

        def softmax(u, s):
            s_loc, s_ctx = s
            m = jnp.maximum(jnp.max(s_loc, axis=-1, keepdims=True), jnp.max(s_ctx, axis=-1, keepdims=True))
            e_loc = jnp.exp(s_loc - m)
            e_ctx = jnp.exp(s_ctx - m)
            l = jnp.sum(e_loc, axis=-1, keepdims=True) + jnp.sum(e_ctx, axis=-1, keepdims=True)
            return e_loc.astype(BF16), e_ctx.astype(BF16), l

        def weighted(u, p):
            r, i = u
            p_loc, p_ctx, l = p
            o = _dot(p_loc, v_ref[pl.ds(k0[r], NA_LOC), tiles[i]]) + _dot(p_ctx, vc_ref[0, :, tiles[i]])
            y = _unstack_pair(o, GRID_W) / _unstack_pair(jnp.broadcast_to(l, o.shape), GRID_W)
            y_ref[pl.ds(q0[r], GRID_W), tiles[i]] = y.astype(BF16)

        _skewed(units, (scores, softmax, weighted))
        return carry

    lax.fori_loop(0, NA_R // NA_ROWS_PER_ITER, rows, 0)


def _na_lat(qn, knb, vb, kc, vc, btab, y_in):
    rows_per_step = NA_R * GRID_W
    steps = GRID_ROWS // NA_R
    off_q = N_PROMPT // rows_per_step
    off_s = N_PROMPT // DEC_SEQ
    qspec = pl.BlockSpec((rows_per_step, NA_WIDTH), lambda b, r: (off_q + b * steps + r, 0))
    kvspec = pl.BlockSpec((DEC_SEQ, NA_WIDTH), lambda b, r: (off_s + b, 0))
    cspec = pl.BlockSpec((1, PAST_LEN, NA_WIDTH), lambda b, r: (b, 0, 0))
    token_major = lambda a: a.transpose(0, 2, 1, 3).reshape(DEC_BATCH, PAST_LEN, NA_WIDTH).astype(BF16)
    kc, vc = token_major(kc), token_major(vc)
    return pl.pallas_call(
        _na_lat_kernel,
        grid=(DEC_BATCH, steps),
        in_specs=[qspec, kvspec, kvspec, cspec, cspec,
                  _const_spec((NA_WIN_ROWS, NA_HEADS, GRID_W, NA_LOC)),
                  pl.BlockSpec(memory_space=pl.ANY)],
        out_specs=qspec,
        out_shape=jax.ShapeDtypeStruct((N_TOK, NA_WIDTH), BF16),
        input_output_aliases={6: 0},
        compiler_params=_cparams(2),
        name="na_lat",
    )(qn, knb, vb, kc, vc, btab, y_in)


def _na_bias_table(rpb):
    col = jnp.arange(GRID_W)
    col_start = jnp.clip(col - NA_WIN_COLS // 2, 0, GRID_W - NA_WIN_COLS)
    ok = (col[None, :] >= col_start[:, None]) & (col[None, :] < col_start[:, None] + NA_WIN_COLS)
    d_col = jnp.clip(col[None, :] - col[:, None], -(NA_WIN_COLS - 1), NA_WIN_COLS - 1) + (NA_WIN_COLS - 1)
    onehot = (d_col[:, :, None] == jnp.arange(2 * NA_WIN_COLS - 1)).astype(F32)
    t = jnp.einsum('hrd,cxd->hcrx', rpb, onehot, precision=HIGHEST)
    t = jnp.where(ok[None, :, None, :], t, NA_MASKED)
    b = jnp.stack([t[:, :, a:a + NA_WIN_ROWS, :] for a in range(NA_WIN_ROWS)])
    return b.reshape(NA_WIN_ROWS, NA_HEADS, GRID_W, NA_LOC)


LIN_HEADS = 4
LIN_DK = 64
LIN_DV = 128
LIN_QK = LIN_HEADS * LIN_DK
LIN_V = LIN_HEADS * LIN_DV


def _log_sigmoid(x):
    return jnp.minimum(x, 0.0) - jnp.log(1.0 + jnp.exp(-jnp.abs(x)))


def _lin_kernel(gqf_ref, gvf_ref, gqb_ref, gvb_ref, hqf_ref, hvf_ref, hqb_ref, hvb_ref,
                sgf_ref, sgb_ref, shf_ref, shb_ref, wa_ref, ba_ref, lbl_ref,
                ogf_ref, ogb_ref, ohf_ref, ohb_ref, ngf_ref, ngb_ref, nhf_ref, nhb_ref, worst_ref,
                st_ref, fb_ref, *, exact):
    j = pl.program_id(0)
    _, blk, _ = _seq_of(j)
    is_prompt = j < PROMPT_BLKS
    s0_refs = (sgf_ref, sgb_ref, shf_ref, shb_ref)

    @pl.when(blk == 0)
    def _():
        for i in range(4):
            st_ref[i] = jnp.where(is_prompt, 0.0, s0_refs[i][0])

    L = LIN_L
    nch = RB // L
    qk_refs = ((gqf_ref, gqb_ref), (hqf_ref, hqb_ref))
    v_refs = ((gvf_ref, gvb_ref), (hvf_ref, hvb_ref))
    o_refs = ((ogf_ref, ogb_ref), (ohf_ref, ohb_ref))
    lane = lax.broadcasted_iota(jnp.int32, (1, LIN_QK), 1)
    head_mask = [(lane >> 6) == h for h in range(LIN_HEADS)]
    r4 = lax.broadcasted_iota(jnp.int32, (LIN_HEADS * L, L), 0) & (L - 1)
    c4 = lax.broadcasted_iota(jnp.int32, (LIN_HEADS * L, L), 1)
    tri4 = [c4 <= r4, c4 >= r4]
    eye = (lax.broadcasted_iota(jnp.int32, (LIN_QK, LIN_QK), 0)
           == lax.broadcasted_iota(jnp.int32, (LIN_QK, LIN_QK), 1))
    tri_bf = [jnp.where(_tri(L, bool(d)), 1.0, 0.0).astype(BF16) for d in range(2)]

    def lower_bound(d):
        l0 = lbl_ref[d, 0:1, :]
        l1 = lbl_ref[d, 1:2, :]
        mx = jnp.maximum(l0, l1)
        e0 = jnp.exp(l0 - mx)
        e1 = jnp.exp(l1 - mx)
        p0 = e0 / (e0 + e1)
        p1 = e1 / (e0 + e1)
        return (p0 + p1) - p0

    lbs = (lower_bound(0), lower_bound(1))
    units = [(m, d, c) for c in range(nch) for d in range(2) for m in range(2)]
    row0 = {u: ((nch - 1 - u[2]) * L if u[1] else u[2] * L) for u in units}

    def gates(u, _):
        m, d, _ = u
        r0 = row0[u]
        x_ref = qk_refs[m][d]
        if m == 0:
            q = x_ref[r0:r0 + L, 0:256] * (GLA_DK ** -0.5)
            k = x_ref[r0:r0 + L, 256:512]
            ga_hi, ga_lo = _split2(x_ref[r0:r0 + L, 512:640])
            wa_hi, wa_lo = _split2(wa_ref[d])
            x = _dot(ga_hi, wa_hi) + _dot(ga_lo, wa_hi) + _dot(ga_hi, wa_lo) + ba_ref[d]
            g = _log_sigmoid(x) / GLA_GATE_NORM
        else:
            q = x_ref[r0:r0 + L, 0:256]
            f = lbs[d] + (1.0 - lbs[d]) * _sigmoid(x_ref[r0:r0 + L, 256 * (d + 1):256 * (d + 2)])
            k = 1.0 - f
            g = jnp.log(f)
        return q, k, _dot_exact_lhs(tri_bf[d], g)

    lasts = []

    def scale(u, p):
        m, d, _ = u
        q, k, cum = p
        last = cum[0:1, :] if d else cum[L - 1:L, :]
        lasts.append(last)
        q_in = q * jnp.exp(cum)
        k_out = (k * jnp.exp(-cum)).astype(BF16)
        k_end_t = (k * jnp.exp(last - cum)).T.astype(BF16)
        qs = jnp.concatenate([jnp.where(hm, q_in, 0.0) for hm in head_mask], axis=0).astype(BF16)
        dcol = jnp.sum(jnp.where(eye, jnp.exp(last), 0.0), axis=1, keepdims=True)
        v = v_refs[m][d][row0[u]:row0[u] + L, :].astype(BF16)
        return qs, k_out, k_end_t, dcol, v

    def products(u, p):
        qs, k_out, k_end_t, dcol, v = p
        upd = jnp.concatenate(
            [_dot(k_end_t[LIN_DK * h:LIN_DK * (h + 1)], v[:, LIN_DV * h:LIN_DV * (h + 1)]) for h in range(LIN_HEADS)],
            axis=0)
        return qs, _dot_nt(qs, k_out), upd, dcol, v

    def exact_scores(u):
        d = u[1]
        q, k, cum = gates(u, None)
        fb_ref[0], fb_ref[1], fb_ref[2] = q, k, cum
        head_sum = jnp.where((lax.broadcasted_iota(jnp.int32, (LIN_QK, 128), 0) >> 6)
                             == lax.broadcasted_iota(jnp.int32, (LIN_QK, 128), 1), 1.0, 0.0).astype(BF16)
        t_col = lax.broadcasted_iota(jnp.int32, (L, 1), 0)
        s_row = lax.broadcasted_iota(jnp.int32, (1, L), 1)

        def key_row(s, acc):
            ks = fb_ref[1, pl.ds(s, 1), :]
            cs = fb_ref[2, pl.ds(s, 1), :]
            w = fb_ref[0] * ks * jnp.exp(jnp.minimum(fb_ref[2] - cs, 0.0))
            cols = _dot_exact_rhs(w, head_sum)
            cols = jnp.where((t_col <= s) if d else (t_col >= s), cols, 0.0)
            return acc + jnp.concatenate(
                [jnp.where(s_row == s, cols[:, h:h + 1], 0.0) for h in range(LIN_HEADS)], axis=0)

        return lax.fori_loop(0, L, key_row, jnp.zeros((LIN_HEADS * L, L), F32)).astype(BF16)

    def mask(u, p):
        qs, a, upd, dcol, v = p
        a = exact_scores(u) if exact else jnp.where(tri4[u[1]], a, 0.0).astype(BF16)
        return qs, a, upd, dcol, v

    def intra(u, p):
        qs, a, upd, dcol, v = p
        o_intra = jnp.concatenate(
            [_dot(a[h * L:(h + 1) * L], v[:, LIN_DV * h:LIN_DV * (h + 1)]) for h in range(LIN_HEADS)], axis=0)
        return qs, o_intra, upd, dcol

    s = [st_ref[i] for i in range(4)]

    def outputs(u, p):
        m, d, _ = u
        qs, o_intra, upd, dcol = p
        i = 2 * m + d
        o = o_intra + _dot(qs, s[i].astype(BF16))
        o_refs[m][d][row0[u]:row0[u] + L, :] = jnp.concatenate(
            [o[h * L:(h + 1) * L] for h in range(LIN_HEADS)], axis=1).astype(BF16)
        s[i] = dcol * s[i] + upd

    _skewed(units, (gates, scale, products, mask, intra, outputs))
    for i in range(4):
        st_ref[i] = s[i]

    @pl.when(is_prompt)
    def _():
        for i, n_ref in enumerate((ngf_ref, ngb_ref, nhf_ref, nhb_ref)):
            n_ref[0] = s[i]
    worst_ref[0] = jnp.broadcast_to(functools.reduce(jnp.minimum, lasts), (SUBLANES, LIN_QK))


def _lin(gqk, gv, hqf, hi, states, wa, ba, lbl):
    st_spec = pl.BlockSpec((1, LIN_QK, LIN_DV), lambda j: (_state_idx(j), 0, 0))
    so_spec = pl.BlockSpec((1, LIN_QK, LIN_DV), lambda j: (jnp.minimum(j, BATCH - 1), 0, 0))
    so_shape = jax.ShapeDtypeStruct((BATCH, LIN_QK, LIN_DV), F32)
    o_shape = jax.ShapeDtypeStruct((N_TOK, LIN_V), BF16)
    worst_spec = pl.BlockSpec((1, SUBLANES, LIN_QK), lambda j: (j, 0, 0))
    worst_shape = jax.ShapeDtypeStruct((N_BLK, SUBLANES, LIN_QK), F32)
    packed = [s.reshape(DEC_BATCH, LIN_QK, LIN_DV) for s in states]

    def run(exact):
        return pl.pallas_call(
            functools.partial(_lin_kernel, exact=exact),
            grid=(N_BLK,),
            in_specs=[_fwd_spec(640), _fwd_spec(LIN_V), _bwd_spec(640), _bwd_spec(LIN_V),
                      _fwd_spec(768), _fwd_spec(LIN_V), _bwd_spec(768), _bwd_spec(LIN_V)]
                     + [st_spec] * 4 + [_const_spec(wa.shape), _const_spec(ba.shape), _const_spec(lbl.shape)],
            out_specs=[_fwd_spec(LIN_V), _bwd_spec(LIN_V), _fwd_spec(LIN_V), _bwd_spec(LIN_V)] + [so_spec] * 4
                      + [worst_spec],
            out_shape=[o_shape] * 4 + [so_shape] * 4 + [worst_shape],
            scratch_shapes=[pltpu.VMEM((4, LIN_QK, LIN_DV), F32), pltpu.VMEM((3, LIN_L, LIN_QK), F32)],
            compiler_params=_cparams(),
            name="lin_exact" if exact else "lin",
        )(gqk, gv, gqk, gv, hqf, hi, hqf, hi, *packed, wa, ba, lbl)

    fast = run(False)
    outs = lax.cond(jnp.min(fast[8]) < -LIN_SAFE_LOG_DECAY, lambda: tuple(run(True)[:8]), lambda: tuple(fast[:8]))
    unpack = lambda s: s.reshape(BATCH, LIN_HEADS, LIN_DK, LIN_DV)
    return outs[:4], [unpack(s) for s in outs[4:]]


def _f32(ref):
    return ref[...].astype(F32)


BF16_ROWS = 16


def _ext_rows(refs):
    t_ref, n_ref, p_ref = refs
    p = _f32(p_ref)
    return jnp.concatenate([_f32(t_ref), _f32(n_ref)[0:SUBLANES], p[p.shape[0] - SUBLANES:]], axis=0)


def _mix0(refs):
    of, ob, z, yb, (nw_ref,) = refs[0:3], refs[3:6], refs[6:9], refs[9:12], refs[12:]
    ya = (_ext_rows(of) + _ext_rows(ob)) * _silu(_ext_rows(z))
    ms = jnp.mean(ya * ya, axis=-1, keepdims=True)
    ya = ya * lax.rsqrt(ms + NORM_EPS) * nw_ref[...]
    return ya.astype(BF16), _ext_rows(yb).astype(BF16)


def _head_rms128(o, w):
    parts = []
    for h in range(LIN_HEADS):
        oh = o[:, LIN_DV * h:LIN_DV * (h + 1)]
        ms = jnp.mean(oh * oh, axis=-1, keepdims=True)
        parts.append(oh * lax.rsqrt(ms + NORM_EPS) * w)
    return jnp.concatenate(parts, axis=-1)


def _mix1(refs):
    gf, gb, gg, hf, hb, hg, (gw_ref, hw_ref) = (refs[0:3], refs[3:6], refs[6:9], refs[9:12], refs[12:15],
                                               refs[15:18], refs[18:])
    yc = _head_rms128(_ext_rows(gf) + _ext_rows(gb), gw_ref[...]) * _silu(_ext_rows(gg))
    yd = _head_rms128(_ext_rows(hf) + _ext_rows(hb), hw_ref[...]) * _silu(_ext_rows(hg))
    return yc.astype(BF16), yd.astype(BF16)


def _outffn_kernel(*refs, layer, n_out):
    i = pl.program_id(0)
    is_prompt = i < PROMPT_TILES
    if layer == 0:
        xp_ref, xs_ref, xn_ref, xv_ref = refs[0:4]
        x0 = jnp.concatenate([_tile_of(xp_ref, xs_ref), xn_ref[...], xv_ref[...]], axis=0)
        n_mix, mix = 13, _mix0
        refs = refs[4:]
    else:
        x0 = _ext_rows(refs[0:3])
        n_mix, mix = 20, _mix1
        refs = refs[3:]
    m_ref, refs = refs[0], refs[1:]
    ya, yb = mix(refs[:n_mix])
    wo_ref, nw_ref, wu_ref, cw_ref, cb_ref, wd_ref = refs[n_mix:n_mix + 6]
    rest = refs[n_mix + 6:]
    o_refs, (hs_ref, act_ref, x1_ref) = rest[:n_out], rest[n_out:]

    x1 = x0 + m_ref[0, 2:3, :] * (_dot(ya, wo_ref[0:512, :]) + _dot(yb, wo_ref[512:1024, :]))
    x1_ref[...] = x1[0:TM]
    nw = nw_ref[...]
    sh = m_ref[0, 3:4, :]
    sc = m_ref[0, 4:5, :]
    hs_ref[0:TM, :] = _norm_mod(x1[0:TM], nw, sh, sc).astype(BF16)
    hs_ref[TM:TM_EXT, :] = (_norm_mod(x1[TM:TM_EXT], nw, sh, sc) * _halo_keep(i)).astype(BF16)

    def conv_cols(c0):
        cs = slice(c0, c0 + FFN_CH)
        return _dwconv_tile(_dot(hs_ref[...], wu_ref[:, cs]), cw_ref[:, cs], cb_ref[:, cs], i)

    for c in range(FFN_DIM // FFN_CH):
        a = conv_cols(c * FFN_CH)
        b = conv_cols(FFN_DIM + c * FFN_CH)
        act_ref[:, c * FFN_CH:(c + 1) * FFN_CH] = (_silu(a) * b).astype(BF16)
    out = x1_ref[...] + m_ref[0, 5:6, :] * _dot(act_ref[...], wd_ref[...])
    if len(o_refs) == 1:
        o_refs[0][...] = out
    else:
        @pl.when(is_prompt)
        def _():
            o_refs[0][...] = out

        @pl.when(jnp.logical_not(is_prompt))
        def _():
            o_refs[1][...] = out


def _outffn(x_args, mix_arrays, mix_params, mrows, w_out, nw, wu, cw, cb, wd, layer):
    per_layer = lambda *shape: pl.BlockSpec((None,) + shape, lambda i: (layer,) + (0,) * len(shape))
    single = dict(pipeline_mode=pl.Buffered(1))
    if layer == 0:
        xp, xs = x_args
        x_ops = [xp, xs, xs, xs]
        x_specs = [_prompt_tile_spec(), _sample_tile_spec()] + _halo_specs(D_MODEL, SUBLANES, N_SAMPLE, PROMPT_TILES)
        out_specs = [_row_spec(D_MODEL)]
        out_shape = [jax.ShapeDtypeStruct((N_TOK, D_MODEL), F32)]
    else:
        x_ops = [x_args[0]] * 3
        x_specs = [_row_spec(D_MODEL)] + _halo_specs(D_MODEL, SUBLANES, N_TOK)
        out_specs = [_prompt_tile_spec(), _sample_tile_spec()]
        out_shape = [jax.ShapeDtypeStruct((N_PROMPT, D_MODEL), F32), jax.ShapeDtypeStruct((N_SAMPLE, D_MODEL), F32)]
    mix_ops, mix_specs = [], []
    for a in mix_arrays:
        mix_ops += [a] * 3
        mix_specs += [_row_spec(a.shape[1])] + _halo_specs(a.shape[1], BF16_ROWS, N_TOK)
    return pl.pallas_call(
        functools.partial(_outffn_kernel, layer=layer, n_out=len(out_specs)),
        grid=(N_TILES,),
        in_specs=x_specs + [_mod_spec()] + mix_specs + [_const_spec(p.shape) for p in mix_params]
                 + [pl.BlockSpec((D_MODEL, D_MODEL), lambda i: (0, 0), **single), _const_spec((1, D_MODEL)),
                    pl.BlockSpec((None, D_MODEL, 2 * FFN_DIM), lambda i: (layer, 0, 0), **single),
                    per_layer(3, 2 * FFN_DIM), per_layer(1, 2 * FFN_DIM),
                    pl.BlockSpec((None, FFN_DIM, D_MODEL), lambda i: (layer, 0, 0), **single)],
        out_specs=out_specs,
        out_shape=out_shape,
        scratch_shapes=[pltpu.VMEM((TM_EXT, D_MODEL), BF16), pltpu.VMEM((TM, FFN_DIM), BF16),
                        pltpu.VMEM((TM, D_MODEL), F32)],
        compiler_params=_cparams(),
        name="outffn",
    )(*x_ops, mrows, *mix_ops, *mix_params, w_out, nw, wu, cw, cb, wd)


def _pad_lanes(v, width=128):
    v = v.reshape(1, -1)
    return jnp.pad(v, ((0, 0), (0, width - v.shape[1])))


def kernel(x_prompt, x_sample, cache_na_k_l0, cache_na_v_l0, state_ssd_fwd_l0, state_ssd_bwd_l0,
           state_gla_fwd_l1, state_gla_bwd_l1, state_hgrn_fwd_l1, state_hgrn_bwd_l1, c,
           c_ctx, w_ada, b_ada, norm_w, ffn_w_up, ffn_conv_w, ffn_conv_b, ffn_w_down,
           w_in_l0, w_out_l0, ssd_conv_w_l0, ssd_conv_b_l0, ssd_dt_bias_l0, ssd_a_log_l0, ssd_d_l0,
           ssd_norm_w_l0, na_q_norm_l0, na_k_norm_l0, na_rpb_l0,
           w_in_l1, w_out_l1, gla_wa2_l1, gla_ba2_l1, gla_norm_w_l1, hgrn_lb_logits, hgrn_norm_w_l1):
    xp = x_prompt.reshape(N_PROMPT, D_MODEL)
    xs = x_sample.reshape(N_SAMPLE, D_MODEL)

    cvec8 = jnp.zeros((SUBLANES, D_MODEL), F32).at[0:DEC_BATCH].set(c).at[CTX_MOD_ROW].set(c_ctx)
    mods = _mods(cvec8, w_ada, b_ada)
    mods = mods.reshape(2, SUBLANES, 6, D_MODEL)
    mods = jnp.pad(mods, ((0, 0), (0, 0), (0, SUBLANES - 6), (0, 0)))

    zpad = lambda n: jnp.zeros((D_MODEL, n), BF16)
    w0b, w1b = w_in_l0.astype(BF16), w_in_l1.astype(BF16)
    w0 = jnp.concatenate([w0b[:, :1296], zpad(112), w0b[:, 1296:]], axis=1)
    w1 = jnp.concatenate([w1b[:, 0:512], w1b[:, 1536:1568], zpad(96), w1b[:, 512:1536], w1b[:, 1568:3360]], axis=1)
    seg = jnp.kron(jnp.eye(NA_HEADS, dtype=F32), jnp.full((NA_HEAD_DIM, NA_HEAD_DIM), 1.0 / NA_HEAD_DIM, F32)).astype(BF16)

    m0 = mods[0]
    z, xc, dt, qn, knb, vb, kn, v = _inproj0(
        xp, xs, m0, norm_w[0, 0].reshape(1, D_MODEL), w0, seg,
        jnp.tile(na_q_norm_l0, NA_HEADS).reshape(1, NA_WIDTH), jnp.tile(na_k_norm_l0, NA_HEADS).reshape(1, NA_WIDTH),
        ssd_conv_w_l0, ssd_conv_b_l0.reshape(1, SSD_XBC))
    of, ob, ssd_f, ssd_b = _ssd(
        xc, dt, state_ssd_fwd_l0, state_ssd_bwd_l0, _pad_lanes(ssd_dt_bias_l0), _pad_lanes(ssd_a_log_l0),
        jnp.repeat(ssd_d_l0, SSD_HEAD_DIM).reshape(1, SSD_WIDTH))
    yb, na_k, na_v = _na_ctx(qn, kn, v)
    yb = _na_lat(qn, knb, vb, cache_na_k_l0, cache_na_v_l0, _na_bias_table(na_rpb_l0), yb)
    ffn_weights = (ffn_w_up.astype(BF16), ffn_conv_w, ffn_conv_b[:, None, :], ffn_w_down.astype(BF16))
    x, = _outffn((xp, xs), (of, ob, z, yb), (ssd_norm_w_l0.reshape(1, SSD_WIDTH),), m0, w_out_l0.astype(BF16),
                 norm_w[0, 1].reshape(1, D_MODEL), *ffn_weights, layer=0)

    m1 = mods[1]
    gqk, gv, gg, hqf, hi, hg = _inproj1(x, m1, norm_w[1, 0].reshape(1, D_MODEL), w1)
    wa_pad = jnp.zeros((2, 128, LIN_QK), F32)
    wa_pad = wa_pad.at[0, 0:GLA_RANK].set(gla_wa2_l1[0]).at[1, GLA_RANK:2 * GLA_RANK].set(gla_wa2_l1[1])
    (gf, gb, hf, hb), (gla_f, gla_b, hgrn_f, hgrn_b) = _lin(
        gqk, gv, hqf, hi, (state_gla_fwd_l1, state_gla_bwd_l1, state_hgrn_fwd_l1, state_hgrn_bwd_l1),
        wa_pad, gla_ba2_l1.reshape(2, 1, LIN_QK), hgrn_lb_logits)
    y_p, y_s = _outffn((x,), (gf, gb, gg, hf, hb, hg),
                       (gla_norm_w_l1.reshape(1, LIN_DV), hgrn_norm_w_l1.reshape(1, LIN_DV)), m1,
                       w_out_l1.astype(BF16), norm_w[1, 1].reshape(1, D_MODEL), *ffn_weights, layer=1)
    return (y_p.reshape(BATCH, SEQ, D_MODEL), y_s.reshape(DEC_BATCH, DEC_SEQ, D_MODEL),
            na_k, na_v, ssd_f, ssd_b, gla_f, gla_b, hgrn_f, hgrn_b)
```

```python
import functools

import jax
import jax.numpy as jnp
from jax import lax
from jax.experimental import pallas as pl
from jax.experimental.pallas import tpu as pltpu

F32 = jnp.float32
BF16 = jnp.bfloat16
HIGHEST = lax.Precision.HIGHEST

D_MODEL = 1024
BATCH = 32
SEQ = 256
DEC_BATCH = 4
DEC_SEQ = 4096
PAST_LEN = 256
GRID_W = 64
NORM_EPS = 1e-6
N_PROMPT = BATCH * SEQ
N_SAMPLE = DEC_BATCH * DEC_SEQ
N_TOK = N_PROMPT + N_SAMPLE

SSD_HEADS = 8
SSD_HEAD_DIM = 64
SSD_STATE = 64
SSD_WIDTH = 512
SSD_BC = 128
SSD_XBC = 768
NA_HEADS = 8
NA_HEAD_DIM = 64
NA_WIDTH = 512
NA_WIN_ROWS = 8
NA_WIN_COLS = 16
NA_SCALE = NA_HEAD_DIM ** -0.5
GLA_DK = 64
GLA_RANK = 16
GLA_GATE_NORM = 16.0
FFN_DIM = 2816

VMEM_LIMIT = 56 * 1024 * 1024
SUBLANES = 8

TM = 512
N_TILES = N_TOK // TM
PROMPT_TILES = N_PROMPT // TM
TILES_PER_SAMPLE = DEC_SEQ // TM
CTX_MOD_ROW = DEC_BATCH

RB = 256
N_BLK = N_TOK // RB
PROMPT_BLKS = N_PROMPT // RB
BLKS_PER_SAMPLE = DEC_SEQ // RB
SSD_L = 128
LIN_L = 64
LIN_SAFE_LOG_DECAY = 60.0
NA_R = 8
NA_ROWS_PER_ITER = 8
FFN_CH = 256
W0_COLS = 2944
W1_COLS = 3456


def _cparams(n_axes=1):
    return pltpu.CompilerParams(dimension_semantics=("arbitrary",) * n_axes,
                                vmem_limit_bytes=VMEM_LIMIT)


def _const_spec(shape):
    nd = len(shape)
    return pl.BlockSpec(shape, lambda *_: (0,) * nd)


def _sigmoid(x):
    return 1.0 / (1.0 + jnp.exp(-x))


def _silu(x):
    return x * _sigmoid(x)


def _softplus(x):
    return jnp.maximum(x, 0.0) + jnp.log(1.0 + jnp.exp(-jnp.abs(x)))


def _mod_row(i):
    return jnp.where(i < PROMPT_TILES, CTX_MOD_ROW, (i - PROMPT_TILES) // TILES_PER_SAMPLE)


def _dot(a, b, **kw):
    return jnp.dot(a, b, preferred_element_type=F32, **kw)


def _dot_nt(a, b):
    return lax.dot_general(a, b, (((1,), (1,)), ((), ())), preferred_element_type=F32)


def _split2(x):
    hi = x.astype(BF16)
    lo = (x - hi.astype(F32)).astype(BF16)
    return hi, lo


def _split3(x):
    hi = x.astype(BF16)
    r = x - hi.astype(F32)
    mid = r.astype(BF16)
    lo = (r - mid.astype(F32)).astype(BF16)
    return hi, mid, lo


def _dot_exact_lhs(a_bf, x):
    hi, lo = _split2(x)
    return _dot(a_bf, hi) + _dot(a_bf, lo)


def _dot_exact_rhs(x, b_bf):
    hi, lo = _split2(x)
    return _dot(hi, b_bf) + _dot(lo, b_bf)


def _skewed(units, stages):
    results = {}
    for step in range(len(units) + len(stages) - 1):
        for k, stage in enumerate(stages):
            i = step - k
            if 0 <= i < len(units):
                results[(k, i)] = stage(units[i], results.pop((k - 1, i), None))


MODS_NB = 1536


def _mods_kernel(c_ref, w_ref, b_ref, o_ref):
    s_hi, s_lo = _split2(_silu(c_ref[...]))
    w_hi, w_lo = _split2(w_ref[0])
    o_ref[0] = _dot(s_hi, w_hi) + _dot(s_lo, w_hi) + _dot(s_hi, w_lo) + b_ref[0]


def _mods(cvec8, w_ada, b_ada):
    depth = w_ada.shape[0]
    nb = 6 * D_MODEL // MODS_NB
    return pl.pallas_call(
        _mods_kernel,
        grid=(depth, nb),
        in_specs=[
            _const_spec((SUBLANES, D_MODEL)),
            pl.BlockSpec((1, D_MODEL, MODS_NB), lambda l, j: (l, 0, j)),
            pl.BlockSpec((1, 1, MODS_NB), lambda l, j: (l, 0, j)),
        ],
        out_specs=pl.BlockSpec((1, SUBLANES, MODS_NB), lambda l, j: (l, 0, j)),
        out_shape=jax.ShapeDtypeStruct((depth, SUBLANES, 6 * D_MODEL), F32),
        compiler_params=_cparams(2),
        name="mods",
    )(cvec8, w_ada, b_ada.reshape(depth, 1, 6 * D_MODEL))


def _norm_mod(x, nw, sh, sc):
    ms = jnp.mean(x * x, axis=-1, keepdims=True)
    y = x * lax.rsqrt(ms + NORM_EPS) * nw
    return y * (1.0 + sc) + sh


def _tile_of(xp_ref, xs_ref):
    return jnp.where(pl.program_id(0) < PROMPT_TILES, xp_ref[...], xs_ref[...])


HALO_ROWS = 2 * SUBLANES
TM_EXT = TM + HALO_ROWS


def _halo_keep(i):
    seq = jnp.where(i < PROMPT_TILES, SEQ, DEC_SEQ)
    keep_prev = jnp.where(((i * TM) & (seq - 1)) == 0, 0.0, 1.0)
    keep_next = jnp.where(((i * TM + TM) & (seq - 1)) == 0, 0.0, 1.0)
    hrow = lax.broadcasted_iota(jnp.int32, (HALO_ROWS, 1), 0)
    return jnp.where(hrow < SUBLANES, keep_next, keep_prev)


def _dwconv_tile(u_all, w, bias, i):
    um = pltpu.roll(u_all, 1, axis=0)[0:TM] * w[0:1]
    up = pltpu.roll(u_all, TM_EXT - 1, axis=0)[0:TM] * w[2:3]
    out = bias + um + u_all[0:TM] * w[1:2] + up
    edge_on = jnp.where(i < PROMPT_TILES, 1.0, 0.0)
    erow = lax.broadcasted_iota(jnp.int32, (HALO_ROWS, 1), 0)
    at_first = jnp.where(erow == SUBLANES, edge_on, 0.0)
    at_last = jnp.where(erow == SUBLANES - 1, edge_on, 0.0)
    pieces, r = [], 0
    for edge in range(SEQ, TM, SEQ):
        sl = slice(edge - SUBLANES, edge + SUBLANES)
        pieces += [out[r:edge - SUBLANES], out[sl] - at_first * um[sl] - at_last * up[sl]]
        r = edge + SUBLANES
    return jnp.concatenate(pieces + [out[r:TM]], axis=0)


def _inproj0_kernel(xp_ref, xs_ref, xn_ref, xv_ref, m_ref, nw_ref, w_ref, seg_ref, qw_ref, kw_ref, cw_ref, cb_ref,
                    z_ref, xc_ref, dt_ref, qn_ref, knb_ref, vb_ref, kn_ref, v_ref, hs_ref):
    i = pl.program_id(0)
    nw, sh, sc = nw_ref[...], m_ref[0, 0:1, :], m_ref[0, 1:2, :]
    hs_ref[0:TM, :] = _norm_mod(_tile_of(xp_ref, xs_ref), nw, sh, sc).astype(BF16)
    halo = _norm_mod(jnp.concatenate([xn_ref[...], xv_ref[...]], axis=0), nw, sh, sc)
    hs_ref[TM:TM_EXT, :] = (halo * _halo_keep(i)).astype(BF16)
    q = _dot(hs_ref[0:TM, :], w_ref[:, 1408:1920])
    k = _dot(hs_ref[0:TM, :], w_ref[:, 1920:2432])
    xc_ref[...] = _silu(_dwconv_tile(_dot(hs_ref[...], w_ref[:, 512:1280]), cw_ref[...], cb_ref[...], i))
    ms_q = _dot((q * q).astype(BF16), seg_ref[...])
    ms_k = _dot((k * k).astype(BF16), seg_ref[...])
    z_ref[...] = _silu(_dot(hs_ref[0:TM, :], w_ref[:, 0:512])).astype(BF16)
    dt_ref[...] = _dot(hs_ref[0:TM, :], w_ref[:, 1280:1408])
    v = _dot(hs_ref[0:TM, :], w_ref[:, 2432:2944])
    vb_ref[...] = v.astype(BF16)
    qn_ref[...] = (q * lax.rsqrt(ms_q + NORM_EPS) * qw_ref[...] * NA_SCALE).astype(BF16)
    kn = k * lax.rsqrt(ms_k + NORM_EPS) * kw_ref[...]
    knb_ref[...] = kn.astype(BF16)

    @pl.when(pl.program_id(0) < PROMPT_TILES)
    def _():
        kn_ref[...] = kn
        v_ref[...] = v


def _row_spec(width):
    return pl.BlockSpec((TM, width), lambda i: (i, 0))


def _mod_spec():
    return pl.BlockSpec((1, SUBLANES, D_MODEL), lambda i: (_mod_row(i), 0, 0))


def _prompt_tile_spec(width=D_MODEL):
    return pl.BlockSpec((TM, width), lambda i: (jnp.minimum(i, PROMPT_TILES - 1), 0))


def _sample_tile_spec(width=D_MODEL):
    return pl.BlockSpec((TM, width), lambda i: (jnp.maximum(i - PROMPT_TILES, 0), 0))


def _halo_specs(width, rows, total_rows, first_tile=0):
    per_tile = TM // rows
    last = total_rows // rows - 1
    nxt = pl.BlockSpec((rows, width), lambda i: (jnp.clip((i - first_tile + 1) * per_tile, 0, last), 0))
    prv = pl.BlockSpec((rows, width), lambda i: (jnp.clip((i - first_tile) * per_tile - 1, 0, last), 0))
    return [nxt, prv]


def _inproj0(xp, xs, mrows, nw, w0, seg, qw, kw, conv_w, conv_b):
    widths = (512, 768, 128, 512, 512, 512)
    dtypes = (BF16, F32, F32, BF16, BF16, BF16)
    nxt, prv = _halo_specs(D_MODEL, SUBLANES, N_SAMPLE, PROMPT_TILES)
    return pl.pallas_call(
        _inproj0_kernel,
        grid=(N_TILES,),
        in_specs=[
            _prompt_tile_spec(), _sample_tile_spec(), nxt, prv, _mod_spec(), _const_spec((1, D_MODEL)),
            _const_spec((D_MODEL, W0_COLS)), _const_spec((512, 512)),
            _const_spec((1, 512)), _const_spec((1, 512)), _const_spec((3, SSD_XBC)), _const_spec((1, SSD_XBC)),
        ],
        out_specs=[_row_spec(w) for w in widths] + [_prompt_tile_spec(NA_WIDTH)] * 2,
        out_shape=[jax.ShapeDtypeStruct((N_TOK, w), d) for w, d in zip(widths, dtypes)]
                  + [jax.ShapeDtypeStruct((N_PROMPT, NA_WIDTH), F32)] * 2,
        scratch_shapes=[pltpu.VMEM((TM_EXT, D_MODEL), BF16)],
        compiler_params=_cparams(),
        name="inproj0",
    )(xp, xs, xs, xs, mrows, nw, w0, seg, qw, kw, conv_w, conv_b)


def _inproj1_kernel(x_ref, m_ref, nw_ref, w_ref, gqk_ref, gv_ref, gg_ref, hqf_ref, hi_ref, hg_ref, hs_ref):
    hs_ref[...] = _norm_mod(x_ref[...], nw_ref[...], m_ref[0, 0:1, :], m_ref[0, 1:2, :]).astype(BF16)
    gqk_ref[...] = _dot(hs_ref[...], w_ref[:, 0:640])
    gv_ref[...] = _dot(hs_ref[...], w_ref[:, 640:1152]).astype(BF16)
    gg_ref[...] = _silu(_dot(hs_ref[...], w_ref[:, 1152:1664])).astype(BF16)
    hqf_ref[...] = _dot(hs_ref[...], w_ref[:, 1664:2432])
    hi_ref[...] = _dot(hs_ref[...], w_ref[:, 2432:2944]).astype(BF16)
    hg_ref[...] = _silu(_dot(hs_ref[...], w_ref[:, 2944:3456])).astype(BF16)


def _inproj1(x, mrows, nw, w1):
    widths = (640, 512, 512, 768, 512, 512)
    dtypes = (F32, BF16, BF16, F32, BF16, BF16)
    return pl.pallas_call(
        _inproj1_kernel,
        grid=(N_TILES,),
        in_specs=[_row_spec(D_MODEL), _mod_spec(), _const_spec((1, D_MODEL)),
                  _const_spec((D_MODEL, W1_COLS))],
        out_specs=[_row_spec(w) for w in widths],
        out_shape=[jax.ShapeDtypeStruct((N_TOK, w), d) for w, d in zip(widths, dtypes)],
        scratch_shapes=[pltpu.VMEM((TM, D_MODEL), BF16)],
        compiler_params=_cparams(),
        name="inproj1",
    )(x, mrows, nw, w1)


def _seq_of(j):
    jj = j - PROMPT_BLKS
    is_p = j < PROMPT_BLKS
    seq = jnp.where(is_p, j, BATCH + jj // BLKS_PER_SAMPLE)
    blk = jnp.where(is_p, 0, jj % BLKS_PER_SAMPLE)
    nblk = jnp.where(is_p, 1, BLKS_PER_SAMPLE)
    return seq, blk, nblk


def _bwd_blk(j):
    _, blk, nblk = _seq_of(j)
    return j - blk + (nblk - 1 - blk)


def _state_idx(j):
    seq, _, _ = _seq_of(j)
    return jnp.maximum(seq - BATCH, 0)


def _fwd_spec(width):
    return pl.BlockSpec((RB, width), lambda j: (j, 0))


def _bwd_spec(width):
    return pl.BlockSpec((RB, width), lambda j: (_bwd_blk(j), 0))


def _tri(n, upper):
    r = lax.broadcasted_iota(jnp.int32, (n, n), 0)
    c = lax.broadcasted_iota(jnp.int32, (n, n), 1)
    return (c >= r) if upper else (c <= r)


def _diag_blocks(x, rows):
    lh = lax.broadcasted_iota(jnp.int32, (1, 4 * SSD_HEAD_DIM), 1) >> 6
    out = x[3 * rows:4 * rows]
    for h in (2, 1, 0):
        out = jnp.where(lh == h, x[h * rows:(h + 1) * rows], out)
    return out


def _ssd_kernel(xf_ref, dtf_ref, xb_ref, dtb_ref, s0f_ref, s0b_ref, dtbias_ref, alog_ref, dsk_ref, ex_ref,
                of_ref, ob_ref, sf_ref, sb_ref, st_ref):
    j = pl.program_id(0)
    _, blk, _ = _seq_of(j)
    is_prompt = j < PROMPT_BLKS

    @pl.when(blk == 0)
    def _():
        st_ref[0] = jnp.where(is_prompt, 0.0, s0f_ref[0])
        st_ref[1] = jnp.where(is_prompt, 0.0, s0b_ref[0])

    a_neg = -jnp.exp(alog_ref[...])
    a_col = jnp.broadcast_to(a_neg, (SUBLANES, 128)).T
    L = SSD_L
    nch = RB // L
    xc_refs = (xf_ref, xb_ref)
    dt_refs = (dtf_ref, dtb_ref)
    o_refs = (of_ref, ob_ref)

    units = [(d, c) for c in range(nch) for d in range(2)]
    row0 = {(d, c): ((nch - 1 - c) * L if d else c * L) for d, c in units}
    tri = [_tri(L, False), _tri(L, True)]
    tri_bf = [jnp.where(t, 1.0, 0.0).astype(BF16) for t in tri]
    lane = lax.broadcasted_iota(jnp.int32, (1, SSD_BC), 1)
    gmask = [lane < SSD_STATE, lane >= SSD_STATE]
    eye_bf = jnp.where(tri[0] & tri[1], 1.0, 0.0).astype(BF16)
    zero_blk = jnp.zeros((SSD_STATE, 4 * SSD_HEAD_DIM), F32)

    def gates(u, _):
        d, r0 = u[0], row0[u]
        dt = _softplus(dt_refs[d][r0:r0 + L, :] + dtbias_ref[...])
        cum = _dot_exact_lhs(tri_bf[d], dt * a_neg)
        dt8 = dt.T[8 * d:8 * d + 8]
        ct8 = _dot_exact_rhs(dt8 * a_col[8 * d:8 * d + 8, 0:1], tri_bf[1 - d])
        parts = _split3(cum[0:SUBLANES] if d else cum[L - SUBLANES:L])
        last_x = _dot(parts[0], ex_ref[d]) + _dot(parts[1], ex_ref[d]) + _dot(parts[2], ex_ref[d])
        e_last_x = jnp.exp(last_x[0:1] if d else last_x[SUBLANES - 1:SUBLANES])
        last_col = ct8[:, 0:1] if d else ct8[:, L - 1:L]
        f_end = jnp.exp(last_col - ct8) * dt8
        bm = xc_refs[d][r0:r0 + L, 512:640].astype(BF16)
        cm = xc_refs[d][r0:r0 + L, 640:768]
        b_t = _dot_nt(eye_bf, bm)
        cb_g = [_dot_nt(jnp.where(gmask[g], cm, 0.0).astype(BF16), bm) for g in range(2)]
        return cum, ct8, dt8, e_last_x, f_end, b_t, cb_g, cm

    def operands(u, p):
        d = u[0]
        cum, ct8, dt8, e_last_x, f_end, b_t, cb_g, cm = p
        lhs, k_t = [], []
        for g in range(2):
            ls, ks = [], []
            for hh in range(4):
                h = 4 * g + hh
                pc = jnp.broadcast_to(cum[:, 8 * d + h:8 * d + h + 1], (L, L))
                dec = jnp.exp(jnp.where(tri[d], pc - ct8[h:h + 1, :], -jnp.inf))
                w = cb_g[g] * dec * dt8[h:h + 1, :]
                ls.append(jnp.concatenate([w.astype(BF16), (cm * jnp.exp(pc)).astype(BF16)], axis=1))
                ks.append((b_t[SSD_STATE * g:SSD_STATE * (g + 1)] * f_end[h:h + 1, :]).astype(BF16))
            lhs.append(jnp.concatenate(ls, axis=0))
            k_t.append(jnp.concatenate(ks, axis=0))
        return lhs, k_t, e_last_x

    def state_update(u, p):
        d, r0 = u[0], row0[u]
        lhs, k_t, e_last_x = p
        x_bf = [xc_refs[d][r0:r0 + L, 256 * g:256 * (g + 1)].astype(BF16) for g in range(2)]
        up = [_diag_blocks(_dot(k_t[g], x_bf[g]), SSD_STATE) for g in range(2)]
        upd = jnp.concatenate([jnp.concatenate([up[0], zero_blk], axis=1),
                               jnp.concatenate([zero_blk, up[1]], axis=1)], axis=0)
        return lhs, x_bf, e_last_x, upd

    s = [st_ref[0], st_ref[1]]

    def outputs(u, p):
        d, r0 = u[0], row0[u]
        lhs, x_bf, e_last_x, upd = p
        s_bf = s[d].astype(BF16)
        o = jnp.concatenate(
            [_diag_blocks(_dot(lhs[g], jnp.concatenate([x_bf[g], s_bf[:, 256 * g:256 * (g + 1)]], axis=0)), L)
             for g in range(2)], axis=1)
        if d == 0:
            o = o + dsk_ref[...] * xf_ref[r0:r0 + L, 0:SSD_WIDTH]
        o_refs[d][r0:r0 + L, :] = o.astype(BF16)
        s[d] = e_last_x * s[d] + upd

    _skewed(units, (gates, operands, state_update, outputs))
    st_ref[0] = s[0]
    st_ref[1] = s[1]

    @pl.when(is_prompt)
    def _():
        sf_ref[0] = s[0]
        sb_ref[0] = s[1]


def _ssd_pack_state(s):
    b = s.shape[0]
    g = s.reshape(b, 2, 4, SSD_STATE, SSD_HEAD_DIM).transpose(0, 1, 3, 2, 4).reshape(b, 2, SSD_STATE, 256)
    z = jnp.zeros((b, SSD_STATE, 256), F32)
    return jnp.concatenate([jnp.concatenate([g[:, 0], z], axis=2), jnp.concatenate([z, g[:, 1]], axis=2)], axis=1)


def _ssd_unpack_state(s):
    b = s.shape[0]
    g = jnp.stack([s[:, 0:SSD_STATE, 0:256], s[:, SSD_STATE:, 256:512]], axis=1)
    g = g.reshape(b, 2, SSD_STATE, 4, SSD_HEAD_DIM).transpose(0, 1, 3, 2, 4)
    return g.reshape(b, SSD_HEADS, SSD_STATE, SSD_HEAD_DIM)


def _ssd(xc, dt, s0f, s0b, dtbias, alog, dsk):
    st_shape = (1, 2 * SSD_STATE, SSD_WIDTH)
    st_spec = pl.BlockSpec(st_shape, lambda j: (_state_idx(j), 0, 0))
    so_spec = pl.BlockSpec(st_shape, lambda j: (jnp.minimum(j, BATCH - 1), 0, 0))
    so_shape = jax.ShapeDtypeStruct((BATCH, 2 * SSD_STATE, SSD_WIDTH), F32)
    col = jnp.arange(128)[:, None]
    lane_head = jnp.arange(SSD_WIDTH)[None, :] // SSD_HEAD_DIM
    expand = jnp.stack([col == lane_head, col == SSD_HEADS + lane_head]).astype(BF16)
    of, ob, sf, sb = pl.pallas_call(
        _ssd_kernel,
        grid=(N_BLK,),
        in_specs=[
            _fwd_spec(SSD_XBC), _fwd_spec(128), _bwd_spec(SSD_XBC), _bwd_spec(128), st_spec, st_spec,
            _const_spec((1, 128)), _const_spec((1, 128)), _const_spec((1, SSD_WIDTH)),
            _const_spec((2, 128, SSD_WIDTH)),
        ],
        out_specs=[_fwd_spec(SSD_WIDTH), _bwd_spec(SSD_WIDTH), so_spec, so_spec],
        out_shape=[jax.ShapeDtypeStruct((N_TOK, SSD_WIDTH), BF16)] * 2 + [so_shape, so_shape],
        scratch_shapes=[pltpu.VMEM((2, 2 * SSD_STATE, SSD_WIDTH), F32)],
        compiler_params=_cparams(),
        name="ssd",
    )(xc, dt, xc, dt, _ssd_pack_state(s0f), _ssd_pack_state(s0b), dtbias, alog, dsk, expand)
    return of, ob, _ssd_unpack_state(sf), _ssd_unpack_state(sb)


NA_PAIRS = NA_HEADS // 2


def _stack_pair(qt):
    lower = lax.broadcasted_iota(jnp.int32, (1, 2 * NA_HEAD_DIM), 1) < NA_HEAD_DIM
    zero = jnp.zeros_like(qt)
    return jnp.concatenate([jnp.where(lower, qt, zero), jnp.where(lower, zero, qt)], axis=0)


def _unstack_pair(x, n):
    lower = lax.broadcasted_iota(jnp.int32, (1, 2 * NA_HEAD_DIM), 1) < NA_HEAD_DIM
    return jnp.where(lower, x[0:n], x[n:2 * n])


def _na_ctx_kernel(q_ref, k_ref, v_ref, y_ref, ko_ref, vo_ref):
    tiles = [slice(128 * i, 128 * (i + 1)) for i in range(NA_PAIRS)]
    s = [_dot_nt(_stack_pair(q_ref[:, ts]), k_ref[:, ts].astype(BF16)) for ts in tiles]
    p, l = [], []
    for i in range(NA_PAIRS):
        e = jnp.exp(s[i] - jnp.max(s[i], axis=-1, keepdims=True))
        l.append(jnp.sum(e, axis=-1, keepdims=True))
        p.append(e.astype(BF16))
    o = [_dot(p[i], v_ref[:, tiles[i]].astype(BF16)) for i in range(NA_PAIRS)]
    for i in range(NA_PAIRS):
        y = _unstack_pair(o[i], SEQ) / _unstack_pair(jnp.broadcast_to(l[i], o[i].shape), SEQ)
        y_ref[:, tiles[i]] = y.astype(BF16)
    for h in range(NA_HEADS):
        sl = slice(NA_HEAD_DIM * h, NA_HEAD_DIM * (h + 1))
        ko_ref[0, h] = k_ref[:, sl]
        vo_ref[0, h] = v_ref[:, sl]


def _na_ctx(qn, kn, v):
    blk = lambda w: pl.BlockSpec((SEQ, w), lambda b: (b, 0))
    hm = pl.BlockSpec((1, NA_HEADS, SEQ, NA_HEAD_DIM), lambda b: (b, 0, 0, 0))
    hm_shape = jax.ShapeDtypeStruct((BATCH, NA_HEADS, SEQ, NA_HEAD_DIM), F32)
    return pl.pallas_call(
        _na_ctx_kernel,
        grid=(BATCH,),
        in_specs=[blk(NA_WIDTH)] * 3,
        out_specs=[blk(NA_WIDTH), hm, hm],
        out_shape=[jax.ShapeDtypeStruct((N_TOK, NA_WIDTH), BF16), hm_shape, hm_shape],
        compiler_params=_cparams(),
        name="na_ctx",
    )(qn, kn, v)


GRID_ROWS = DEC_SEQ // GRID_W
NA_LOC = NA_WIN_ROWS * GRID_W
NA_MASKED = -1e30


def _na_lat_kernel(q_ref, k_ref, v_ref, kc_ref, vc_ref, bt_ref, yin_ref, y_ref):
    del yin_ref
    rb = pl.program_id(1)

    tiles = [slice(128 * i, 128 * (i + 1)) for i in range(NA_PAIRS)]

    def rows(it, carry):
        units, q0, k0, dr0 = [], {}, {}, {}
        for rr_ in range(NA_ROWS_PER_ITER):
            r = it * NA_ROWS_PER_ITER + rr_
            rr = rb * NA_R + r
            rs = jnp.clip(rr - NA_WIN_ROWS // 2, 0, GRID_ROWS - NA_WIN_ROWS)
            dr0[rr_] = rs - rr + (NA_WIN_ROWS - 1)
            q0[rr_] = pl.multiple_of(r * GRID_W, GRID_W)
            k0[rr_] = pl.multiple_of(rs * GRID_W, GRID_W)
            units += [(rr_, i) for i in range(NA_PAIRS)]
        def scores(u, _):
            r, i = u
            qq = _stack_pair(q_ref[pl.ds(q0[r], GRID_W), tiles[i]])
            s_loc = (_dot_nt(qq, k_ref[pl.ds(k0[r], NA_LOC), tiles[i]])
                     + bt_ref[dr0[r], 2 * i:2 * i + 2].reshape(2 * GRID_W, NA_LOC))
            return s_loc, _dot_nt(qq, kc_ref[0, :, tiles[i]])

        def softmax(u, s):
            s_loc, s_ctx = s
            m = jnp.maximum(jnp.max(s_loc, axis=-1, keepdims=True), jnp.max(s_ctx, axis=-1, keepdims=True))
            e_loc = jnp.exp(s_loc - m)
            e_ctx = jnp.exp(s_ctx - m)
            l = jnp.sum(e_loc, axis=-1, keepdims=True) + jnp.sum(e_ctx, axis=-1, keepdims=True)
            return e_loc.astype(BF16), e_ctx.astype(BF16), l

        def weighted(u, p):
            r, i = u
            p_loc, p_ctx, l = p
            o = _dot(p_loc, v_ref[pl.ds(k0[r], NA_LOC), tiles[i]]) + _dot(p_ctx, vc_ref[0, :, tiles[i]])
            y = _unstack_pair(o, GRID_W) / _unstack_pair(jnp.broadcast_to(l, o.shape), GRID_W)
            y_ref[pl.ds(q0[r], GRID_W), tiles[i]] = y.astype(BF16)

        _skewed(units, (scores, softmax, weighted))
        return carry

    lax.fori_loop(0, NA_R // NA_ROWS_PER_ITER, rows, 0)


def _na_lat(qn, knb, vb, kc, vc, btab, y_in):
    rows_per_step = NA_R * GRID_W
    steps = GRID_ROWS // NA_R
    off_q = N_PROMPT // rows_per_step
    off_s = N_PROMPT // DEC_SEQ
    qspec = pl.BlockSpec((rows_per_step, NA_WIDTH), lambda b, r: (off_q + b * steps + r, 0))
    kvspec = pl.BlockSpec((DEC_SEQ, NA_WIDTH), lambda b, r: (off_s + b, 0))
    cspec = pl.BlockSpec((1, PAST_LEN, NA_WIDTH), lambda b, r: (b, 0, 0))
    token_major = lambda a: a.transpose(0, 2, 1, 3).reshape(DEC_BATCH, PAST_LEN, NA_WIDTH).astype(BF16)
    kc, vc = token_major(kc), token_major(vc)
    return pl.pallas_call(
        _na_lat_kernel,
        grid=(DEC_BATCH, steps),
        in_specs=[qspec, kvspec, kvspec, cspec, cspec,
                  _const_spec((NA_WIN_ROWS, NA_HEADS, GRID_W, NA_LOC)),
                  pl.BlockSpec(memory_space=pl.ANY)],
        out_specs=qspec,
        out_shape=jax.ShapeDtypeStruct((N_TOK, NA_WIDTH), BF16),
        input_output_aliases={6: 0},
        compiler_params=_cparams(2),
        name="na_lat",
    )(qn, knb, vb, kc, vc, btab, y_in)


def _na_bias_table(rpb):
    col = jnp.arange(GRID_W)
    col_start = jnp.clip(col - NA_WIN_COLS // 2, 0, GRID_W - NA_WIN_COLS)
    ok = (col[None, :] >= col_start[:, None]) & (col[None, :] < col_start[:, None] + NA_WIN_COLS)
    d_col = jnp.clip(col[None, :] - col[:, None], -(NA_WIN_COLS - 1), NA_WIN_COLS - 1) + (NA_WIN_COLS - 1)
    onehot = (d_col[:, :, None] == jnp.arange(2 * NA_WIN_COLS - 1)).astype(F32)
    t = jnp.einsum('hrd,cxd->hcrx', rpb, onehot, precision=HIGHEST)
    t = jnp.where(ok[None, :, None, :], t, NA_MASKED)
    b = jnp.stack([t[:, :, a:a + NA_WIN_ROWS, :] for a in range(NA_WIN_ROWS)])
    return b.reshape(NA_WIN_ROWS, NA_HEADS, GRID_W, NA_LOC)


LIN_HEADS = 4
LIN_DK = 64
LIN_DV = 128
LIN_QK = LIN_HEADS * LIN_DK
LIN_V = LIN_HEADS * LIN_DV


def _log_sigmoid(x):
    return jnp.minimum(x, 0.0) - jnp.log(1.0 + jnp.exp(-jnp.abs(x)))


def _lin_kernel(gqf_ref, gvf_ref, gqb_ref, gvb_ref, hqf_ref, hvf_ref, hqb_ref, hvb_ref,
                sgf_ref, sgb_ref, shf_ref, shb_ref, wa_ref, ba_ref, lbl_ref,
                ogf_ref, ogb_ref, ohf_ref, ohb_ref, ngf_ref, ngb_ref, nhf_ref, nhb_ref, worst_ref,
                st_ref, fb_ref, *, exact):
    j = pl.program_id(0)
    _, blk, _ = _seq_of(j)
    is_prompt = j < PROMPT_BLKS
    s0_refs = (sgf_ref, sgb_ref, shf_ref, shb_ref)

    @pl.when(blk == 0)
    def _():
        for i in range(4):
            st_ref[i] = jnp.where(is_prompt, 0.0, s0_refs[i][0])

    L = LIN_L
    nch = RB // L
    qk_refs = ((gqf_ref, gqb_ref), (hqf_ref, hqb_ref))
    v_refs = ((gvf_ref, gvb_ref), (hvf_ref, hvb_ref))
    o_refs = ((ogf_ref, ogb_ref), (ohf_ref, ohb_ref))
    lane = lax.broadcasted_iota(jnp.int32, (1, LIN_QK), 1)
    head_mask = [(lane >> 6) == h for h in range(LIN_HEADS)]
    r4 = lax.broadcasted_iota(jnp.int32, (LIN_HEADS * L, L), 0) & (L - 1)
    c4 = lax.broadcasted_iota(jnp.int32, (LIN_HEADS * L, L), 1)
    tri4 = [c4 <= r4, c4 >= r4]
    eye = (lax.broadcasted_iota(jnp.int32, (LIN_QK, LIN_QK), 0)
           == lax.broadcasted_iota(jnp.int32, (LIN_QK, LIN_QK), 1))
    tri_bf = [jnp.where(_tri(L, bool(d)), 1.0, 0.0).astype(BF16) for d in range(2)]

    def lower_bound(d):
        l0 = lbl_ref[d, 0:1, :]
        l1 = lbl_ref[d, 1:2, :]
        mx = jnp.maximum(l0, l1)
        e0 = jnp.exp(l0 - mx)
        e1 = jnp.exp(l1 - mx)
        p0 = e0 / (e0 + e1)
        p1 = e1 / (e0 + e1)
        return (p0 + p1) - p0

    lbs = (lower_bound(0), lower_bound(1))
    units = [(m, d, c) for c in range(nch) for d in range(2) for m in range(2)]
    row0 = {u: ((nch - 1 - u[2]) * L if u[1] else u[2] * L) for u in units}

    def gates(u, _):
        m, d, _ = u
        r0 = row0[u]
        x_ref = qk_refs[m][d]
        if m == 0:
            q = x_ref[r0:r0 + L, 0:256] * (GLA_DK ** -0.5)
            k = x_ref[r0:r0 + L, 256:512]
            ga_hi, ga_lo = _split2(x_ref[r0:r0 + L, 512:640])
            wa_hi, wa_lo = _split2(wa_ref[d])
            x = _dot(ga_hi, wa_hi) + _dot(ga_lo, wa_hi) + _dot(ga_hi, wa_lo) + ba_ref[d]
            g = _log_sigmoid(x) / GLA_GATE_NORM
        else:
            q = x_ref[r0:r0 + L, 0:256]
            f = lbs[d] + (1.0 - lbs[d]) * _sigmoid(x_ref[r0:r0 + L, 256 * (d + 1):256 * (d + 2)])
            k = 1.0 - f
            g = jnp.log(f)
        return q, k, _dot_exact_lhs(tri_bf[d], g)

    lasts = []

    def scale(u, p):
        m, d, _ = u
        q, k, cum = p
        last = cum[0:1, :] if d else cum[L - 1:L, :]
        lasts.append(last)
        q_in = q * jnp.exp(cum)
        k_out = (k * jnp.exp(-cum)).astype(BF16)
        k_end_t = (k * jnp.exp(last - cum)).T.astype(BF16)
        qs = jnp.concatenate([jnp.where(hm, q_in, 0.0) for hm in head_mask], axis=0).astype(BF16)
        dcol = jnp.sum(jnp.where(eye, jnp.exp(last), 0.0), axis=1, keepdims=True)
        v = v_refs[m][d][row0[u]:row0[u] + L, :].astype(BF16)
        return qs, k_out, k_end_t, dcol, v

    def products(u, p):
        qs, k_out, k_end_t, dcol, v = p
        upd = jnp.concatenate(
            [_dot(k_end_t[LIN_DK * h:LIN_DK * (h + 1)], v[:, LIN_DV * h:LIN_DV * (h + 1)]) for h in range(LIN_HEADS)],
            axis=0)
        return qs, _dot_nt(qs, k_out), upd, dcol, v

    def exact_scores(u):
        d = u[1]
        q, k, cum = gates(u, None)
        fb_ref[0], fb_ref[1], fb_ref[2] = q, k, cum
        head_sum = jnp.where((lax.broadcasted_iota(jnp.int32, (LIN_QK, 128), 0) >> 6)
                             == lax.broadcasted_iota(jnp.int32, (LIN_QK, 128), 1), 1.0, 0.0).astype(BF16)
        t_col = lax.broadcasted_iota(jnp.int32, (L, 1), 0)
        s_row = lax.broadcasted_iota(jnp.int32, (1, L), 1)

        def key_row(s, acc):
            ks = fb_ref[1, pl.ds(s, 1), :]
            cs = fb_ref[2, pl.ds(s, 1), :]
            w = fb_ref[0] * ks * jnp.exp(jnp.minimum(fb_ref[2] - cs, 0.0))
            cols = _dot_exact_rhs(w, head_sum)
            cols = jnp.where((t_col <= s) if d else (t_col >= s), cols, 0.0)
            return acc + jnp.concatenate(
                [jnp.where(s_row == s, cols[:, h:h + 1], 0.0) for h in range(LIN_HEADS)], axis=0)

        return lax.fori_loop(0, L, key_row, jnp.zeros((LIN_HEADS * L, L), F32)).astype(BF16)

    def mask(u, p):
        qs, a, upd, dcol, v = p
        a = exact_scores(u) if exact else jnp.where(tri4[u[1]], a, 0.0).astype(BF16)
        return qs, a, upd, dcol, v

    def intra(u, p):
        qs, a, upd, dcol, v = p
        o_intra = jnp.concatenate(
            [_dot(a[h * L:(h + 1) * L], v[:, LIN_DV * h:LIN_DV * (h + 1)]) for h in range(LIN_HEADS)], axis=0)
        return qs, o_intra, upd, dcol

    s = [st_ref[i] for i in range(4)]

    def outputs(u, p):
        m, d, _ = u
        qs, o_intra, upd, dcol = p
        i = 2 * m + d
        o = o_intra + _dot(qs, s[i].astype(BF16))
        o_refs[m][d][row0[u]:row0[u] + L, :] = jnp.concatenate(
            [o[h * L:(h + 1) * L] for h in range(LIN_HEADS)], axis=1).astype(BF16)
        s[i] = dcol * s[i] + upd

    _skewed(units, (gates, scale, products, mask, intra, outputs))
    for i in range(4):
        st_ref[i] = s[i]

    @pl.when(is_prompt)
    def _():
        for i, n_ref in enumerate((ngf_ref, ngb_ref, nhf_ref, nhb_ref)):
            n_ref[0] = s[i]
    worst_ref[0] = jnp.broadcast_to(functools.reduce(jnp.minimum, lasts), (SUBLANES, LIN_QK))


def _lin(gqk, gv, hqf, hi, states, wa, ba, lbl):
    st_spec = pl.BlockSpec((1, LIN_QK, LIN_DV), lambda j: (_state_idx(j), 0, 0))
    so_spec = pl.BlockSpec((1, LIN_QK, LIN_DV), lambda j: (jnp.minimum(j, BATCH - 1), 0, 0))
    so_shape = jax.ShapeDtypeStruct((BATCH, LIN_QK, LIN_DV), F32)
    o_shape = jax.ShapeDtypeStruct((N_TOK, LIN_V), BF16)
    worst_spec = pl.BlockSpec((1, SUBLANES, LIN_QK), lambda j: (j, 0, 0))
    worst_shape = jax.ShapeDtypeStruct((N_BLK, SUBLANES, LIN_QK), F32)
    packed = [s.reshape(DEC_BATCH, LIN_QK, LIN_DV) for s in states]

    def run(exact):
        return pl.pallas_call(
            functools.partial(_lin_kernel, exact=exact),
            grid=(N_BLK,),
            in_specs=[_fwd_spec(640), _fwd_spec(LIN_V), _bwd_spec(640), _bwd_spec(LIN_V),
                      _fwd_spec(768), _fwd_spec(LIN_V), _bwd_spec(768), _bwd_spec(LIN_V)]
                     + [st_spec] * 4 + [_const_spec(wa.shape), _const_spec(ba.shape), _const_spec(lbl.shape)],
            out_specs=[_fwd_spec(LIN_V), _bwd_spec(LIN_V), _fwd_spec(LIN_V), _bwd_spec(LIN_V)] + [so_spec] * 4
                      + [worst_spec],
            out_shape=[o_shape] * 4 + [so_shape] * 4 + [worst_shape],
            scratch_shapes=[pltpu.VMEM((4, LIN_QK, LIN_DV), F32), pltpu.VMEM((3, LIN_L, LIN_QK), F32)],
            compiler_params=_cparams(),
            name="lin_exact" if exact else "lin",
        )(gqk, gv, gqk, gv, hqf, hi, hqf, hi, *packed, wa, ba, lbl)

    fast = run(False)
    outs = lax.cond(jnp.min(fast[8]) < -LIN_SAFE_LOG_DECAY, lambda: tuple(run(True)[:8]), lambda: tuple(fast[:8]))
    unpack = lambda s: s.reshape(BATCH, LIN_HEADS, LIN_DK, LIN_DV)
    return outs[:4], [unpack(s) for s in outs[4:]]


def _f32(ref):
    return ref[...].astype(F32)


BF16_ROWS = 16


def _ext_rows(refs):
    t_ref, n_ref, p_ref = refs
    p = _f32(p_ref)
    return jnp.concatenate([_f32(t_ref), _f32(n_ref)[0:SUBLANES], p[p.shape[0] - SUBLANES:]], axis=0)


def _mix0(refs):
    of, ob, z, yb, (nw_ref,) = refs[0:3], refs[3:6], refs[6:9], refs[9:12], refs[12:]
    ya = (_ext_rows(of) + _ext_rows(ob)) * _ext_rows(z)
    ms = jnp.mean(ya * ya, axis=-1, keepdims=True)
    ya = ya * lax.rsqrt(ms + NORM_EPS) * nw_ref[...]
    return ya.astype(BF16), _ext_rows(yb).astype(BF16)


def _head_rms128(o, w):
    parts = []
    for h in range(LIN_HEADS):
        oh = o[:, LIN_DV * h:LIN_DV * (h + 1)]
        ms = jnp.mean(oh * oh, axis=-1, keepdims=True)
        parts.append(oh * lax.rsqrt(ms + NORM_EPS) * w)
    return jnp.concatenate(parts, axis=-1)


def _mix1(refs):
    gf, gb, gg, hf, hb, hg, (gw_ref, hw_ref) = (refs[0:3], refs[3:6], refs[6:9], refs[9:12], refs[12:15],
                                               refs[15:18], refs[18:])
    yc = _head_rms128(_ext_rows(gf) + _ext_rows(gb), gw_ref[...]) * _ext_rows(gg)
    yd = _head_rms128(_ext_rows(hf) + _ext_rows(hb), hw_ref[...]) * _ext_rows(hg)
    return yc.astype(BF16), yd.astype(BF16)


def _outffn_kernel(*refs, layer, n_out):
    i = pl.program_id(0)
    is_prompt = i < PROMPT_TILES
    if layer == 0:
        xp_ref, xs_ref, xn_ref, xv_ref = refs[0:4]
        x0 = jnp.concatenate([_tile_of(xp_ref, xs_ref), xn_ref[...], xv_ref[...]], axis=0)
        n_mix, mix = 13, _mix0
        refs = refs[4:]
    else:
        x0 = _ext_rows(refs[0:3])
        n_mix, mix = 20, _mix1
        refs = refs[3:]
    m_ref, refs = refs[0], refs[1:]
    ya, yb = mix(refs[:n_mix])
    wo_ref, nw_ref, wu_ref, cw_ref, cb_ref, wd_ref = refs[n_mix:n_mix + 6]
    rest = refs[n_mix + 6:]
    o_refs, (hs_ref, act_ref, x1_ref) = rest[:n_out], rest[n_out:]

    x1 = x0 + m_ref[0, 2:3, :] * (_dot(ya, wo_ref[0:512, :]) + _dot(yb, wo_ref[512:1024, :]))
    x1_ref[...] = x1[0:TM]
    nw = nw_ref[...]
    sh = m_ref[0, 3:4, :]
    sc = m_ref[0, 4:5, :]
    hs_ref[0:TM, :] = _norm_mod(x1[0:TM], nw, sh, sc).astype(BF16)
    hs_ref[TM:TM_EXT, :] = (_norm_mod(x1[TM:TM_EXT], nw, sh, sc) * _halo_keep(i)).astype(BF16)

    def conv_cols(c0):
        cs = slice(c0, c0 + FFN_CH)
        return _dwconv_tile(_dot(hs_ref[...], wu_ref[:, cs]), cw_ref[:, cs], cb_ref[:, cs], i)

    for c in range(FFN_DIM // FFN_CH):
        a = conv_cols(c * FFN_CH)
        b = conv_cols(FFN_DIM + c * FFN_CH)
        act_ref[:, c * FFN_CH:(c + 1) * FFN_CH] = (_silu(a) * b).astype(BF16)
    out = x1_ref[...] + m_ref[0, 5:6, :] * _dot(act_ref[...], wd_ref[...])
    if len(o_refs) == 1:
        o_refs[0][...] = out
    else:
        @pl.when(is_prompt)
        def _():
            o_refs[0][...] = out

        @pl.when(jnp.logical_not(is_prompt))
        def _():
            o_refs[1][...] = out


def _outffn(x_args, mix_arrays, mix_params, mrows, w_out, nw, wu, cw, cb, wd, layer):
    per_layer = lambda *shape: pl.BlockSpec((None,) + shape, lambda i: (layer,) + (0,) * len(shape))
    single = dict(pipeline_mode=pl.Buffered(1))
    if layer == 0:
        xp, xs = x_args
        x_ops = [xp, xs, xs, xs]
        x_specs = [_prompt_tile_spec(), _sample_tile_spec()] + _halo_specs(D_MODEL, SUBLANES, N_SAMPLE, PROMPT_TILES)
        out_specs = [_row_spec(D_MODEL)]
        out_shape = [jax.ShapeDtypeStruct((N_TOK, D_MODEL), F32)]
    else:
        x_ops = [x_args[0]] * 3
        x_specs = [_row_spec(D_MODEL)] + _halo_specs(D_MODEL, SUBLANES, N_TOK)
        out_specs = [_prompt_tile_spec(), _sample_tile_spec()]
        out_shape = [jax.ShapeDtypeStruct((N_PROMPT, D_MODEL), F32), jax.ShapeDtypeStruct((N_SAMPLE, D_MODEL), F32)]
    mix_ops, mix_specs = [], []
    for a in mix_arrays:
        mix_ops += [a] * 3
        mix_specs += [_row_spec(a.shape[1])] + _halo_specs(a.shape[1], BF16_ROWS, N_TOK)
    return pl.pallas_call(
        functools.partial(_outffn_kernel, layer=layer, n_out=len(out_specs)),
        grid=(N_TILES,),
        in_specs=x_specs + [_mod_spec()] + mix_specs + [_const_spec(p.shape) for p in mix_params]
                 + [pl.BlockSpec((D_MODEL, D_MODEL), lambda i: (0, 0), **single), _const_spec((1, D_MODEL)),
                    pl.BlockSpec((None, D_MODEL, 2 * FFN_DIM), lambda i: (layer, 0, 0), **single),
                    per_layer(3, 2 * FFN_DIM), per_layer(1, 2 * FFN_DIM),
                    pl.BlockSpec((None, FFN_DIM, D_MODEL), lambda i: (layer, 0, 0), **single)],
        out_specs=out_specs,
        out_shape=out_shape,
        scratch_shapes=[pltpu.VMEM((TM_EXT, D_MODEL), BF16), pltpu.VMEM((TM, FFN_DIM), BF16),
                        pltpu.VMEM((TM, D_MODEL), F32)],
        compiler_params=_cparams(),
        name="outffn",
    )(*x_ops, mrows, *mix_ops, *mix_params, w_out, nw, wu, cw, cb, wd)


def _pad_lanes(v, width=128):
    v = v.reshape(1, -1)
    return jnp.pad(v, ((0, 0), (0, width - v.shape[1])))


def kernel(x_prompt, x_sample, cache_na_k_l0, cache_na_v_l0, state_ssd_fwd_l0, state_ssd_bwd_l0,
           state_gla_fwd_l1, state_gla_bwd_l1, state_hgrn_fwd_l1, state_hgrn_bwd_l1, c,
           c_ctx, w_ada, b_ada, norm_w, ffn_w_up, ffn_conv_w, ffn_conv_b, ffn_w_down,
           w_in_l0, w_out_l0, ssd_conv_w_l0, ssd_conv_b_l0, ssd_dt_bias_l0, ssd_a_log_l0, ssd_d_l0,
           ssd_norm_w_l0, na_q_norm_l0, na_k_norm_l0, na_rpb_l0,
           w_in_l1, w_out_l1, gla_wa2_l1, gla_ba2_l1, gla_norm_w_l1, hgrn_lb_logits, hgrn_norm_w_l1):
    xp = x_prompt.reshape(N_PROMPT, D_MODEL)
    xs = x_sample.reshape(N_SAMPLE, D_MODEL)

    cvec8 = jnp.zeros((SUBLANES, D_MODEL), F32).at[0:DEC_BATCH].set(c).at[CTX_MOD_ROW].set(c_ctx)
    mods = _mods(cvec8, w_ada, b_ada)
    mods = mods.reshape(2, SUBLANES, 6, D_MODEL)
    mods = jnp.pad(mods, ((0, 0), (0, 0), (0, SUBLANES - 6), (0, 0)))

    zpad = lambda n: jnp.zeros((D_MODEL, n), BF16)
    w0b, w1b = w_in_l0.astype(BF16), w_in_l1.astype(BF16)
    w0 = jnp.concatenate([w0b[:, :1296], zpad(112), w0b[:, 1296:]], axis=1)
    w1 = jnp.concatenate([w1b[:, 0:512], w1b[:, 1536:1568], zpad(96), w1b[:, 512:1536], w1b[:, 1568:3360]], axis=1)
    seg = jnp.kron(jnp.eye(NA_HEADS, dtype=F32), jnp.full((NA_HEAD_DIM, NA_HEAD_DIM), 1.0 / NA_HEAD_DIM, F32)).astype(BF16)

    m0 = mods[0]
    z, xc, dt, qn, knb, vb, kn, v = _inproj0(
        xp, xs, m0, norm_w[0, 0].reshape(1, D_MODEL), w0, seg,
        jnp.tile(na_q_norm_l0, NA_HEADS).reshape(1, NA_WIDTH), jnp.tile(na_k_norm_l0, NA_HEADS).reshape(1, NA_WIDTH),
        ssd_conv_w_l0, ssd_conv_b_l0.reshape(1, SSD_XBC))
    of, ob, ssd_f, ssd_b = _ssd(
        xc, dt, state_ssd_fwd_l0, state_ssd_bwd_l0, _pad_lanes(ssd_dt_bias_l0), _pad_lanes(ssd_a_log_l0),
        jnp.repeat(ssd_d_l0, SSD_HEAD_DIM).reshape(1, SSD_WIDTH))
    yb, na_k, na_v = _na_ctx(qn, kn, v)
    yb = _na_lat(qn, knb, vb, cache_na_k_l0, cache_na_v_l0, _na_bias_table(na_rpb_l0), yb)
    ffn_weights = (ffn_w_up.astype(BF16), ffn_conv_w, ffn_conv_b[:, None, :], ffn_w_down.astype(BF16))
    x, = _outffn((xp, xs), (of, ob, z, yb), (ssd_norm_w_l0.reshape(1, SSD_WIDTH),), m0, w_out_l0.astype(BF16),
                 norm_w[0, 1].reshape(1, D_MODEL), *ffn_weights, layer=0)

    m1 = mods[1]
    gqk, gv, gg, hqf, hi, hg = _inproj1(x, m1, norm_w[1, 0].reshape(1, D_MODEL), w1)
    wa_pad = jnp.zeros((2, 128, LIN_QK), F32)
    wa_pad = wa_pad.at[0, 0:GLA_RANK].set(gla_wa2_l1[0]).at[1, GLA_RANK:2 * GLA_RANK].set(gla_wa2_l1[1])
    (gf, gb, hf, hb), (gla_f, gla_b, hgrn_f, hgrn_b) = _lin(
        gqk, gv, hqf, hi, (state_gla_fwd_l1, state_gla_bwd_l1, state_hgrn_fwd_l1, state_hgrn_bwd_l1),
        wa_pad, gla_ba2_l1.reshape(2, 1, LIN_QK), hgrn_lb_logits)
    y_p, y_s = _outffn((x,), (gf, gb, gg, hf, hb, hg),
                       (gla_norm_w_l1.reshape(1, LIN_DV), hgrn_norm_w_l1.reshape(1, LIN_DV)), m1,
                       w_out_l1.astype(BF16), norm_w[1, 1].reshape(1, D_MODEL), *ffn_weights, layer=1)
    return (y_p.reshape(BATCH, SEQ, D_MODEL), y_s.reshape(DEC_BATCH, DEC_SEQ, D_MODEL),
            na_k, na_v, ssd_f, ssd_b, gla_f, gla_b, hgrn_f, hgrn_b)
```

```python
import functools

import jax
import jax.numpy as jnp
from jax import lax
from jax.experimental import pallas as pl
from jax.experimental.pallas import tpu as pltpu

F32 = jnp.float32
BF16 = jnp.bfloat16
HIGHEST = lax.Precision.HIGHEST

D_MODEL = 1024
BATCH = 32
SEQ = 256
DEC_BATCH = 4
DEC_SEQ = 4096
PAST_LEN = 256
GRID_W = 64
NORM_EPS = 1e-6
N_PROMPT = BATCH * SEQ
N_SAMPLE = DEC_BATCH * DEC_SEQ
N_TOK = N_PROMPT + N_SAMPLE

SSD_HEADS = 8
SSD_HEAD_DIM = 64
SSD_STATE = 64
SSD_WIDTH = 512
SSD_BC = 128
SSD_XBC = 768
NA_HEADS = 8
NA_HEAD_DIM = 64
NA_WIDTH = 512
NA_WIN_ROWS = 8
NA_WIN_COLS = 16
NA_SCALE = NA_HEAD_DIM ** -0.5
GLA_DK = 64
GLA_RANK = 16
GLA_GATE_NORM = 16.0
FFN_DIM = 2816

VMEM_LIMIT = 56 * 1024 * 1024
SUBLANES = 8

TM = 512
N_TILES = N_TOK // TM
PROMPT_TILES = N_PROMPT // TM
TILES_PER_SAMPLE = DEC_SEQ // TM
CTX_MOD_ROW = DEC_BATCH

RB = 256
N_BLK = N_TOK // RB
PROMPT_BLKS = N_PROMPT // RB
BLKS_PER_SAMPLE = DEC_SEQ // RB
SSD_L = 128
LIN_L = 64
LIN_SAFE_LOG_DECAY = 60.0
NA_R = 16
NA_ROWS_PER_ITER = 8
FFN_CH = 256
W0_COLS = 2944
W1_COLS = 3456


def _cparams(n_axes=1):
    return pltpu.CompilerParams(dimension_semantics=("arbitrary",) * n_axes,
                                vmem_limit_bytes=VMEM_LIMIT)


def _const_spec(shape):
    nd = len(shape)
    return pl.BlockSpec(shape, lambda *_: (0,) * nd)


def _sigmoid(x):
    return 1.0 / (1.0 + jnp.exp(-x))


def _silu(x):
    return x * _sigmoid(x)


def _softplus(x):
    return jnp.maximum(x, 0.0) + jnp.log(1.0 + jnp.exp(-jnp.abs(x)))


def _mod_row(i):
    return jnp.where(i < PROMPT_TILES, CTX_MOD_ROW, (i - PROMPT_TILES) // TILES_PER_SAMPLE)


def _dot(a, b, **kw):
    return jnp.dot(a, b, preferred_element_type=F32, **kw)


def _dot_nt(a, b):
    return lax.dot_general(a, b, (((1,), (1,)), ((), ())), preferred_element_type=F32)


def _split2(x):
    hi = x.astype(BF16)
    lo = (x - hi.astype(F32)).astype(BF16)
    return hi, lo


def _split3(x):
    hi = x.astype(BF16)
    r = x - hi.astype(F32)
    mid = r.astype(BF16)
    lo = (r - mid.astype(F32)).astype(BF16)
    return hi, mid, lo


def _dot_exact_lhs(a_bf, x):
    hi, lo = _split2(x)
    return _dot(a_bf, hi) + _dot(a_bf, lo)


def _dot_exact_rhs(x, b_bf):
    hi, lo = _split2(x)
    return _dot(hi, b_bf) + _dot(lo, b_bf)


def _skewed(units, stages):
    results = {}
    for step in range(len(units) + len(stages) - 1):
        for k, stage in enumerate(stages):
            i = step - k
            if 0 <= i < len(units):
                results[(k, i)] = stage(units[i], results.pop((k - 1, i), None))


MODS_NB = 1536


def _mods_kernel(c_ref, w_ref, b_ref, o_ref):
    s_hi, s_lo = _split2(_silu(c_ref[...]))
    w_hi, w_lo = _split2(w_ref[0])
    o_ref[0] = _dot(s_hi, w_hi) + _dot(s_lo, w_hi) + _dot(s_hi, w_lo) + b_ref[0]


def _mods(cvec8, w_ada, b_ada):
    depth = w_ada.shape[0]
    nb = 6 * D_MODEL // MODS_NB
    return pl.pallas_call(
        _mods_kernel,
        grid=(depth, nb),
        in_specs=[
            _const_spec((SUBLANES, D_MODEL)),
            pl.BlockSpec((1, D_MODEL, MODS_NB), lambda l, j: (l, 0, j)),
            pl.BlockSpec((1, 1, MODS_NB), lambda l, j: (l, 0, j)),
        ],
        out_specs=pl.BlockSpec((1, SUBLANES, MODS_NB), lambda l, j: (l, 0, j)),
        out_shape=jax.ShapeDtypeStruct((depth, SUBLANES, 6 * D_MODEL), F32),
        compiler_params=_cparams(2),
        name="mods",
    )(cvec8, w_ada, b_ada.reshape(depth, 1, 6 * D_MODEL))


def _norm_mod(x, nw, sh, sc):
    ms = jnp.mean(x * x, axis=-1, keepdims=True)
    y = x * lax.rsqrt(ms + NORM_EPS) * nw
    return y * (1.0 + sc) + sh


def _tile_of(xp_ref, xs_ref):
    return jnp.where(pl.program_id(0) < PROMPT_TILES, xp_ref[...], xs_ref[...])


HALO_ROWS = 2 * SUBLANES
TM_EXT = TM + HALO_ROWS


def _halo_keep(i):
    seq = jnp.where(i < PROMPT_TILES, SEQ, DEC_SEQ)
    keep_prev = jnp.where(((i * TM) & (seq - 1)) == 0, 0.0, 1.0)
    keep_next = jnp.where(((i * TM + TM) & (seq - 1)) == 0, 0.0, 1.0)
    hrow = lax.broadcasted_iota(jnp.int32, (HALO_ROWS, 1), 0)
    return jnp.where(hrow < SUBLANES, keep_next, keep_prev)


def _dwconv_tile(u_all, w, bias, i):
    um = pltpu.roll(u_all, 1, axis=0)[0:TM] * w[0:1]
    up = pltpu.roll(u_all, TM_EXT - 1, axis=0)[0:TM] * w[2:3]
    out = bias + um + u_all[0:TM] * w[1:2] + up
    edge_on = jnp.where(i < PROMPT_TILES, 1.0, 0.0)
    erow = lax.broadcasted_iota(jnp.int32, (HALO_ROWS, 1), 0)
    at_first = jnp.where(erow == SUBLANES, edge_on, 0.0)
    at_last = jnp.where(erow == SUBLANES - 1, edge_on, 0.0)
    pieces, r = [], 0
    for edge in range(SEQ, TM, SEQ):
        sl = slice(edge - SUBLANES, edge + SUBLANES)
        pieces += [out[r:edge - SUBLANES], out[sl] - at_first * um[sl] - at_last * up[sl]]
        r = edge + SUBLANES
    return jnp.concatenate(pieces + [out[r:TM]], axis=0)


def _inproj0_kernel(xp_ref, xs_ref, xn_ref, xv_ref, m_ref, nw_ref, w_ref, seg_ref, qw_ref, kw_ref, cw_ref, cb_ref,
                    z_ref, xc_ref, dt_ref, qn_ref, knb_ref, vb_ref, kn_ref, v_ref, hs_ref):
    i = pl.program_id(0)
    nw, sh, sc = nw_ref[...], m_ref[0, 0:1, :], m_ref[0, 1:2, :]
    hs_ref[0:TM, :] = _norm_mod(_tile_of(xp_ref, xs_ref), nw, sh, sc).astype(BF16)
    halo = _norm_mod(jnp.concatenate([xn_ref[...], xv_ref[...]], axis=0), nw, sh, sc)
    hs_ref[TM:TM_EXT, :] = (halo * _halo_keep(i)).astype(BF16)
    q = _dot(hs_ref[0:TM, :], w_ref[:, 1408:1920])
    k = _dot(hs_ref[0:TM, :], w_ref[:, 1920:2432])
    xc_ref[...] = _silu(_dwconv_tile(_dot(hs_ref[...], w_ref[:, 512:1280]), cw_ref[...], cb_ref[...], i))
    ms_q = _dot((q * q).astype(BF16), seg_ref[...])
    ms_k = _dot((k * k).astype(BF16), seg_ref[...])
    z_ref[...] = _silu(_dot(hs_ref[0:TM, :], w_ref[:, 0:512])).astype(BF16)
    dt_ref[...] = _dot(hs_ref[0:TM, :], w_ref[:, 1280:1408])
    v = _dot(hs_ref[0:TM, :], w_ref[:, 2432:2944])
    vb_ref[...] = v.astype(BF16)
    qn_ref[...] = (q * lax.rsqrt(ms_q + NORM_EPS) * qw_ref[...] * NA_SCALE).astype(BF16)
    kn = k * lax.rsqrt(ms_k + NORM_EPS) * kw_ref[...]
    knb_ref[...] = kn.astype(BF16)

    @pl.when(pl.program_id(0) < PROMPT_TILES)
    def _():
        kn_ref[...] = kn
        v_ref[...] = v


def _row_spec(width):
    return pl.BlockSpec((TM, width), lambda i: (i, 0))


def _mod_spec():
    return pl.BlockSpec((1, SUBLANES, D_MODEL), lambda i: (_mod_row(i), 0, 0))


def _prompt_tile_spec(width=D_MODEL):
    return pl.BlockSpec((TM, width), lambda i: (jnp.minimum(i, PROMPT_TILES - 1), 0))


def _sample_tile_spec(width=D_MODEL):
    return pl.BlockSpec((TM, width), lambda i: (jnp.maximum(i - PROMPT_TILES, 0), 0))


def _halo_specs(width, rows, total_rows, first_tile=0):
    per_tile = TM // rows
    last = total_rows // rows - 1
    nxt = pl.BlockSpec((rows, width), lambda i: (jnp.clip((i - first_tile + 1) * per_tile, 0, last), 0))
    prv = pl.BlockSpec((rows, width), lambda i: (jnp.clip((i - first_tile) * per_tile - 1, 0, last), 0))
    return [nxt, prv]


def _inproj0(xp, xs, mrows, nw, w0, seg, qw, kw, conv_w, conv_b):
    widths = (512, 768, 128, 512, 512, 512)
    dtypes = (BF16, F32, F32, BF16, BF16, BF16)
    nxt, prv = _halo_specs(D_MODEL, SUBLANES, N_SAMPLE, PROMPT_TILES)
    return pl.pallas_call(
        _inproj0_kernel,
        grid=(N_TILES,),
        in_specs=[
            _prompt_tile_spec(), _sample_tile_spec(), nxt, prv, _mod_spec(), _const_spec((1, D_MODEL)),
            _const_spec((D_MODEL, W0_COLS)), _const_spec((512, 512)),
            _const_spec((1, 512)), _const_spec((1, 512)), _const_spec((3, SSD_XBC)), _const_spec((1, SSD_XBC)),
        ],
        out_specs=[_row_spec(w) for w in widths] + [_prompt_tile_spec(NA_WIDTH)] * 2,
        out_shape=[jax.ShapeDtypeStruct((N_TOK, w), d) for w, d in zip(widths, dtypes)]
                  + [jax.ShapeDtypeStruct((N_PROMPT, NA_WIDTH), F32)] * 2,
        scratch_shapes=[pltpu.VMEM((TM_EXT, D_MODEL), BF16)],
        compiler_params=_cparams(),
        name="inproj0",
    )(xp, xs, xs, xs, mrows, nw, w0, seg, qw, kw, conv_w, conv_b)


def _inproj1_kernel(x_ref, m_ref, nw_ref, w_ref, gqk_ref, gv_ref, gg_ref, hqf_ref, hi_ref, hg_ref, hs_ref):
    hs_ref[...] = _norm_mod(x_ref[...], nw_ref[...], m_ref[0, 0:1, :], m_ref[0, 1:2, :]).astype(BF16)
    gqk_ref[...] = _dot(hs_ref[...], w_ref[:, 0:640])
    gv_ref[...] = _dot(hs_ref[...], w_ref[:, 640:1152]).astype(BF16)
    gg_ref[...] = _silu(_dot(hs_ref[...], w_ref[:, 1152:1664])).astype(BF16)
    hqf_ref[...] = _dot(hs_ref[...], w_ref[:, 1664:2432])
    hi_ref[...] = _dot(hs_ref[...], w_ref[:, 2432:2944]).astype(BF16)
    hg_ref[...] = _silu(_dot(hs_ref[...], w_ref[:, 2944:3456])).astype(BF16)


def _inproj1(x, mrows, nw, w1):
    widths = (640, 512, 512, 768, 512, 512)
    dtypes = (F32, BF16, BF16, F32, BF16, BF16)
    return pl.pallas_call(
        _inproj1_kernel,
        grid=(N_TILES,),
        in_specs=[_row_spec(D_MODEL), _mod_spec(), _const_spec((1, D_MODEL)),
                  _const_spec((D_MODEL, W1_COLS))],
        out_specs=[_row_spec(w) for w in widths],
        out_shape=[jax.ShapeDtypeStruct((N_TOK, w), d) for w, d in zip(widths, dtypes)],
        scratch_shapes=[pltpu.VMEM((TM, D_MODEL), BF16)],
        compiler_params=_cparams(),
        name="inproj1",
    )(x, mrows, nw, w1)


def _seq_of(j):
    jj = j - PROMPT_BLKS
    is_p = j < PROMPT_BLKS
    seq = jnp.where(is_p, j, BATCH + jj // BLKS_PER_SAMPLE)
    blk = jnp.where(is_p, 0, jj % BLKS_PER_SAMPLE)
    nblk = jnp.where(is_p, 1, BLKS_PER_SAMPLE)
    return seq, blk, nblk


def _bwd_blk(j):
    _, blk, nblk = _seq_of(j)
    return j - blk + (nblk - 1 - blk)


def _state_idx(j):
    seq, _, _ = _seq_of(j)
    return jnp.maximum(seq - BATCH, 0)


def _fwd_spec(width):
    return pl.BlockSpec((RB, width), lambda j: (j, 0))


def _bwd_spec(width):
    return pl.BlockSpec((RB, width), lambda j: (_bwd_blk(j), 0))


def _tri(n, upper):
    r = lax.broadcasted_iota(jnp.int32, (n, n), 0)
    c = lax.broadcasted_iota(jnp.int32, (n, n), 1)
    return (c >= r) if upper else (c <= r)


def _diag_blocks(x, rows):
    lh = lax.broadcasted_iota(jnp.int32, (1, 4 * SSD_HEAD_DIM), 1) >> 6
    out = x[3 * rows:4 * rows]
    for h in (2, 1, 0):
        out = jnp.where(lh == h, x[h * rows:(h + 1) * rows], out)
    return out


def _ssd_kernel(xf_ref, dtf_ref, xb_ref, dtb_ref, s0f_ref, s0b_ref, dtbias_ref, alog_ref, dsk_ref, ex_ref,
                of_ref, ob_ref, sf_ref, sb_ref, st_ref):
    j = pl.program_id(0)
    _, blk, _ = _seq_of(j)
    is_prompt = j < PROMPT_BLKS

    @pl.when(blk == 0)
    def _():
        st_ref[0] = jnp.where(is_prompt, 0.0, s0f_ref[0])
        st_ref[1] = jnp.where(is_prompt, 0.0, s0b_ref[0])

    a_neg = -jnp.exp(alog_ref[...])
    a_col = jnp.broadcast_to(a_neg, (SUBLANES, 128)).T
    L = SSD_L
    nch = RB // L
    xc_refs = (xf_ref, xb_ref)
    dt_refs = (dtf_ref, dtb_ref)
    o_refs = (of_ref, ob_ref)

    units = [(d, c) for c in range(nch) for d in range(2)]
    row0 = {(d, c): ((nch - 1 - c) * L if d else c * L) for d, c in units}
    tri = [_tri(L, False), _tri(L, True)]
    tri_bf = [jnp.where(t, 1.0, 0.0).astype(BF16) for t in tri]
    lane = lax.broadcasted_iota(jnp.int32, (1, SSD_BC), 1)
    gmask = [lane < SSD_STATE, lane >= SSD_STATE]
    eye_bf = jnp.where(tri[0] & tri[1], 1.0, 0.0).astype(BF16)
    zero_blk = jnp.zeros((SSD_STATE, 4 * SSD_HEAD_DIM), F32)

    def gates(u, _):
        d, r0 = u[0], row0[u]
        dt = _softplus(dt_refs[d][r0:r0 + L, :] + dtbias_ref[...])
        cum = _dot_exact_lhs(tri_bf[d], dt * a_neg)
        dt8 = dt.T[8 * d:8 * d + 8]
        ct8 = _dot_exact_rhs(dt8 * a_col[8 * d:8 * d + 8, 0:1], tri_bf[1 - d])
        parts = _split3(cum[0:SUBLANES] if d else cum[L - SUBLANES:L])
        last_x = _dot(parts[0], ex_ref[d]) + _dot(parts[1], ex_ref[d]) + _dot(parts[2], ex_ref[d])
        e_last_x = jnp.exp(last_x[0:1] if d else last_x[SUBLANES - 1:SUBLANES])
        last_col = ct8[:, 0:1] if d else ct8[:, L - 1:L]
        f_end = jnp.exp(last_col - ct8) * dt8
        bm = xc_refs[d][r0:r0 + L, 512:640].astype(BF16)
        cm = xc_refs[d][r0:r0 + L, 640:768]
        b_t = _dot_nt(eye_bf, bm)
        cb_g = [_dot_nt(jnp.where(gmask[g], cm, 0.0).astype(BF16), bm) for g in range(2)]
        return cum, ct8, dt8, e_last_x, f_end, b_t, cb_g, cm

    def operands(u, p):
        d = u[0]
        cum, ct8, dt8, e_last_x, f_end, b_t, cb_g, cm = p
        lhs, k_t = [], []
        for g in range(2):
            ls, ks = [], []
            for hh in range(4):
                h = 4 * g + hh
                pc = jnp.broadcast_to(cum[:, 8 * d + h:8 * d + h + 1], (L, L))
                dec = jnp.exp(jnp.where(tri[d], pc - ct8[h:h + 1, :], -jnp.inf))
                w = cb_g[g] * dec * dt8[h:h + 1, :]
                ls.append(jnp.concatenate([w.astype(BF16), (cm * jnp.exp(pc)).astype(BF16)], axis=1))
                ks.append((b_t[SSD_STATE * g:SSD_STATE * (g + 1)] * f_end[h:h + 1, :]).astype(BF16))
            lhs.append(jnp.concatenate(ls, axis=0))
            k_t.append(jnp.concatenate(ks, axis=0))
        return lhs, k_t, e_last_x

    def state_update(u, p):
        d, r0 = u[0], row0[u]
        lhs, k_t, e_last_x = p
        x_bf = [xc_refs[d][r0:r0 + L, 256 * g:256 * (g + 1)].astype(BF16) for g in range(2)]
        up = [_diag_blocks(_dot(k_t[g], x_bf[g]), SSD_STATE) for g in range(2)]
        upd = jnp.concatenate([jnp.concatenate([up[0], zero_blk], axis=1),
                               jnp.concatenate([zero_blk, up[1]], axis=1)], axis=0)
        return lhs, x_bf, e_last_x, upd

    s = [st_ref[0], st_ref[1]]

    def outputs(u, p):
        d, r0 = u[0], row0[u]
        lhs, x_bf, e_last_x, upd = p
        s_bf = s[d].astype(BF16)
        o = jnp.concatenate(
            [_diag_blocks(_dot(lhs[g], jnp.concatenate([x_bf[g], s_bf[:, 256 * g:256 * (g + 1)]], axis=0)), L)
             for g in range(2)], axis=1)
        if d == 0:
            o = o + dsk_ref[...] * xf_ref[r0:r0 + L, 0:SSD_WIDTH]
        o_refs[d][r0:r0 + L, :] = o.astype(BF16)
        s[d] = e_last_x * s[d] + upd

    _skewed(units, (gates, operands, state_update, outputs))
    st_ref[0] = s[0]
    st_ref[1] = s[1]

    @pl.when(is_prompt)
    def _():
        sf_ref[0] = s[0]
        sb_ref[0] = s[1]


def _ssd_pack_state(s):
    b = s.shape[0]
    g = s.reshape(b, 2, 4, SSD_STATE, SSD_HEAD_DIM).transpose(0, 1, 3, 2, 4).reshape(b, 2, SSD_STATE, 256)
    z = jnp.zeros((b, SSD_STATE, 256), F32)
    return jnp.concatenate([jnp.concatenate([g[:, 0], z], axis=2), jnp.concatenate([z, g[:, 1]], axis=2)], axis=1)


def _ssd_unpack_state(s):
    b = s.shape[0]
    g = jnp.stack([s[:, 0:SSD_STATE, 0:256], s[:, SSD_STATE:, 256:512]], axis=1)
    g = g.reshape(b, 2, SSD_STATE, 4, SSD_HEAD_DIM).transpose(0, 1, 3, 2, 4)
    return g.reshape(b, SSD_HEADS, SSD_STATE, SSD_HEAD_DIM)


def _ssd(xc, dt, s0f, s0b, dtbias, alog, dsk):
    st_shape = (1, 2 * SSD_STATE, SSD_WIDTH)
    st_spec = pl.BlockSpec(st_shape, lambda j: (_state_idx(j), 0, 0))
    so_spec = pl.BlockSpec(st_shape, lambda j: (jnp.minimum(j, BATCH - 1), 0, 0))
    so_shape = jax.ShapeDtypeStruct((BATCH, 2 * SSD_STATE, SSD_WIDTH), F32)
    col = jnp.arange(128)[:, None]
    lane_head = jnp.arange(SSD_WIDTH)[None, :] // SSD_HEAD_DIM
    expand = jnp.stack([col == lane_head, col == SSD_HEADS + lane_head]).astype(BF16)
    of, ob, sf, sb = pl.pallas_call(
        _ssd_kernel,
        grid=(N_BLK,),
        in_specs=[
            _fwd_spec(SSD_XBC), _fwd_spec(128), _bwd_spec(SSD_XBC), _bwd_spec(128), st_spec, st_spec,
            _const_spec((1, 128)), _const_spec((1, 128)), _const_spec((1, SSD_WIDTH)),
            _const_spec((2, 128, SSD_WIDTH)),
        ],
        out_specs=[_fwd_spec(SSD_WIDTH), _bwd_spec(SSD_WIDTH), so_spec, so_spec],
        out_shape=[jax.ShapeDtypeStruct((N_TOK, SSD_WIDTH), BF16)] * 2 + [so_shape, so_shape],
        scratch_shapes=[pltpu.VMEM((2, 2 * SSD_STATE, SSD_WIDTH), F32)],
        compiler_params=_cparams(),
        name="ssd",
    )(xc, dt, xc, dt, _ssd_pack_state(s0f), _ssd_pack_state(s0b), dtbias, alog, dsk, expand)
    return of, ob, _ssd_unpack_state(sf), _ssd_unpack_state(sb)


NA_PAIRS = NA_HEADS // 2


def _stack_pair(qt):
    lower = lax.broadcasted_iota(jnp.int32, (1, 2 * NA_HEAD_DIM), 1) < NA_HEAD_DIM
    zero = jnp.zeros_like(qt)
    return jnp.concatenate([jnp.where(lower, qt, zero), jnp.where(lower, zero, qt)], axis=0)


def _unstack_pair(x, n):
    lower = lax.broadcasted_iota(jnp.int32, (1, 2 * NA_HEAD_DIM), 1) < NA_HEAD_DIM
    return jnp.where(lower, x[0:n], x[n:2 * n])


def _na_ctx_kernel(q_ref, k_ref, v_ref, y_ref, ko_ref, vo_ref):
    tiles = [slice(128 * i, 128 * (i + 1)) for i in range(NA_PAIRS)]
    s = [_dot_nt(_stack_pair(q_ref[:, ts]), k_ref[:, ts].astype(BF16)) for ts in tiles]
    p, l = [], []
    for i in range(NA_PAIRS):
        e = jnp.exp(s[i] - jnp.max(s[i], axis=-1, keepdims=True))
        l.append(jnp.sum(e, axis=-1, keepdims=True))
        p.append(e.astype(BF16))
    o = [_dot(p[i], v_ref[:, tiles[i]].astype(BF16)) for i in range(NA_PAIRS)]
    for i in range(NA_PAIRS):
        y = _unstack_pair(o[i], SEQ) / _unstack_pair(jnp.broadcast_to(l[i], o[i].shape), SEQ)
        y_ref[:, tiles[i]] = y.astype(BF16)
    for h in range(NA_HEADS):
        sl = slice(NA_HEAD_DIM * h, NA_HEAD_DIM * (h + 1))
        ko_ref[0, h] = k_ref[:, sl]
        vo_ref[0, h] = v_ref[:, sl]


def _na_ctx(qn, kn, v):
    blk = lambda w: pl.BlockSpec((SEQ, w), lambda b: (b, 0))
    hm = pl.BlockSpec((1, NA_HEADS, SEQ, NA_HEAD_DIM), lambda b: (b, 0, 0, 0))
    hm_shape = jax.ShapeDtypeStruct((BATCH, NA_HEADS, SEQ, NA_HEAD_DIM), F32)
    return pl.pallas_call(
        _na_ctx_kernel,
        grid=(BATCH,),
        in_specs=[blk(NA_WIDTH)] * 3,
        out_specs=[blk(NA_WIDTH), hm, hm],
        out_shape=[jax.ShapeDtypeStruct((N_TOK, NA_WIDTH), BF16), hm_shape, hm_shape],
        compiler_params=_cparams(),
        name="na_ctx",
    )(qn, kn, v)


GRID_ROWS = DEC_SEQ // GRID_W
NA_LOC = NA_WIN_ROWS * GRID_W
NA_MASKED = -1e30


def _na_lat_kernel(q_ref, k_ref, v_ref, kc_ref, vc_ref, bt_ref, yin_ref, y_ref):
    del yin_ref
    rb = pl.program_id(1)

    tiles = [slice(128 * i, 128 * (i + 1)) for i in range(NA_PAIRS)]

    def rows(it, carry):
        units, q0, k0, dr0 = [], {}, {}, {}
        for rr_ in range(NA_ROWS_PER_ITER):
            r = it * NA_ROWS_PER_ITER + rr_
            rr = rb * NA_R + r
            rs = jnp.clip(rr - NA_WIN_ROWS // 2, 0, GRID_ROWS - NA_WIN_ROWS)
            dr0[rr_] = rs - rr + (NA_WIN_ROWS - 1)
            q0[rr_] = pl.multiple_of(r * GRID_W, GRID_W)
            k0[rr_] = pl.multiple_of(rs * GRID_W, GRID_W)
            units += [(rr_, i) for i in range(NA_PAIRS)]
        def scores(u, _):
            r, i = u
            qq = _stack_pair(q_ref[pl.ds(q0[r], GRID_W), tiles[i]])
            s_loc = (_dot_nt(qq, k_ref[pl.ds(k0[r], NA_LOC), tiles[i]])
                     + bt_ref[dr0[r], 2 * i:2 * i + 2].reshape(2 * GRID_W, NA_LOC))
            return s_loc, _dot_nt(qq, kc_ref[0, :, tiles[i]])

        def softmax(u, s):
            s_loc, s_ctx = s
            m = jnp.maximum(jnp.max(s_loc, axis=-1, keepdims=True), jnp.max(s_ctx, axis=-1, keepdims=True))
            e_loc = jnp.exp(s_loc - m)
            e_ctx = jnp.exp(s_ctx - m)
            l = jnp.sum(e_loc, axis=-1, keepdims=True) + jnp.sum(e_ctx, axis=-1, keepdims=True)
            return e_loc.astype(BF16), e_ctx.astype(BF16), l

        def weighted(u, p):
            r, i = u
            p_loc, p_ctx, l = p
            o = _dot(p_loc, v_ref[pl.ds(k0[r], NA_LOC), tiles[i]]) + _dot(p_ctx, vc_ref[0, :, tiles[i]])
            y = _unstack_pair(o, GRID_W) / _unstack_pair(jnp.broadcast_to(l, o.shape), GRID_W)
            y_ref[pl.ds(q0[r], GRID_W), tiles[i]] = y.astype(BF16)

        _skewed(units, (scores, softmax, weighted))
        return carry

    lax.fori_loop(0, NA_R // NA_ROWS_PER_ITER, rows, 0)


def _na_lat(qn, knb, vb, kc, vc, btab, y_in):
    rows_per_step = NA_R * GRID_W
    steps = GRID_ROWS // NA_R
    off_q = N_PROMPT // rows_per_step
    off_s = N_PROMPT // DEC_SEQ
    qspec = pl.BlockSpec((rows_per_step, NA_WIDTH), lambda b, r: (off_q + b * steps + r, 0))
    kvspec = pl.BlockSpec((DEC_SEQ, NA_WIDTH), lambda b, r: (off_s + b, 0))
    cspec = pl.BlockSpec((1, PAST_LEN, NA_WIDTH), lambda b, r: (b, 0, 0))
    token_major = lambda a: a.transpose(0, 2, 1, 3).reshape(DEC_BATCH, PAST_LEN, NA_WIDTH).astype(BF16)
    kc, vc = token_major(kc), token_major(vc)
    return pl.pallas_call(
        _na_lat_kernel,
        grid=(DEC_BATCH, steps),
        in_specs=[qspec, kvspec, kvspec, cspec, cspec,
                  _const_spec((NA_WIN_ROWS, NA_HEADS, GRID_W, NA_LOC)),
                  pl.BlockSpec(memory_space=pl.ANY)],
        out_specs=qspec,
        out_shape=jax.ShapeDtypeStruct((N_TOK, NA_WIDTH), BF16),
        input_output_aliases={6: 0},
        compiler_params=_cparams(2),
        name="na_lat",
    )(qn, knb, vb, kc, vc, btab, y_in)


def _na_bias_table(rpb):
    col = jnp.arange(GRID_W)
    col_start = jnp.clip(col - NA_WIN_COLS // 2, 0, GRID_W - NA_WIN_COLS)
    ok = (col[None, :] >= col_start[:, None]) & (col[None, :] < col_start[:, None] + NA_WIN_COLS)
    d_col = jnp.clip(col[None, :] - col[:, None], -(NA_WIN_COLS - 1), NA_WIN_COLS - 1) + (NA_WIN_COLS - 1)
    onehot = (d_col[:, :, None] == jnp.arange(2 * NA_WIN_COLS - 1)).astype(F32)
    t = jnp.einsum('hrd,cxd->hcrx', rpb, onehot, precision=HIGHEST)
    t = jnp.where(ok[None, :, None, :], t, NA_MASKED)
    b = jnp.stack([t[:, :, a:a + NA_WIN_ROWS, :] for a in range(NA_WIN_ROWS)])
    return b.reshape(NA_WIN_ROWS, NA_HEADS, GRID_W, NA_LOC)


LIN_HEADS = 4
LIN_DK = 64
LIN_DV = 128
LIN_QK = LIN_HEADS * LIN_DK
LIN_V = LIN_HEADS * LIN_DV


def _log_sigmoid(x):
    return jnp.minimum(x, 0.0) - jnp.log(1.0 + jnp.exp(-jnp.abs(x)))


def _lin_kernel(gqf_ref, gvf_ref, gqb_ref, gvb_ref, hqf_ref, hvf_ref, hqb_ref, hvb_ref,
                sgf_ref, sgb_ref, shf_ref, shb_ref, wa_ref, ba_ref, lbl_ref,
                ogf_ref, ogb_ref, ohf_ref, ohb_ref, ngf_ref, ngb_ref, nhf_ref, nhb_ref, worst_ref,
                st_ref, fb_ref, *, exact):
    j = pl.program_id(0)
    _, blk, _ = _seq_of(j)
    is_prompt = j < PROMPT_BLKS
    s0_refs = (sgf_ref, sgb_ref, shf_ref, shb_ref)

    @pl.when(blk == 0)
    def _():
        for i in range(4):
            st_ref[i] = jnp.where(is_prompt, 0.0, s0_refs[i][0])

    L = LIN_L
    nch = RB // L
    qk_refs = ((gqf_ref, gqb_ref), (hqf_ref, hqb_ref))
    v_refs = ((gvf_ref, gvb_ref), (hvf_ref, hvb_ref))
    o_refs = ((ogf_ref, ogb_ref), (ohf_ref, ohb_ref))
    lane = lax.broadcasted_iota(jnp.int32, (1, LIN_QK), 1)
    head_mask = [(lane >> 6) == h for h in range(LIN_HEADS)]
    r4 = lax.broadcasted_iota(jnp.int32, (LIN_HEADS * L, L), 0) & (L - 1)
    c4 = lax.broadcasted_iota(jnp.int32, (LIN_HEADS * L, L), 1)
    tri4 = [c4 <= r4, c4 >= r4]
    eye = (lax.broadcasted_iota(jnp.int32, (LIN_QK, LIN_QK), 0)
           == lax.broadcasted_iota(jnp.int32, (LIN_QK, LIN_QK), 1))
    tri_bf = [jnp.where(_tri(L, bool(d)), 1.0, 0.0).astype(BF16) for d in range(2)]

    def lower_bound(d):
        l0 = lbl_ref[d, 0:1, :]
        l1 = lbl_ref[d, 1:2, :]
        mx = jnp.maximum(l0, l1)
        e0 = jnp.exp(l0 - mx)
        e1 = jnp.exp(l1 - mx)
        p0 = e0 / (e0 + e1)
        p1 = e1 / (e0 + e1)
        return (p0 + p1) - p0

    lbs = (lower_bound(0), lower_bound(1))
    units = [(m, d, c) for c in range(nch) for d in range(2) for m in range(2)]
    row0 = {u: ((nch - 1 - u[2]) * L if u[1] else u[2] * L) for u in units}

    def gates(u, _):
        m, d, _ = u
        r0 = row0[u]
        x_ref = qk_refs[m][d]
        if m == 0:
            q = x_ref[r0:r0 + L, 0:256] * (GLA_DK ** -0.5)
            k = x_ref[r0:r0 + L, 256:512]
            ga_hi, ga_lo = _split2(x_ref[r0:r0 + L, 512:640])
            wa_hi, wa_lo = _split2(wa_ref[d])
            x = _dot(ga_hi, wa_hi) + _dot(ga_lo, wa_hi) + _dot(ga_hi, wa_lo) + ba_ref[d]
            g = _log_sigmoid(x) / GLA_GATE_NORM
        else:
            q = x_ref[r0:r0 + L, 0:256]
            f = lbs[d] + (1.0 - lbs[d]) * _sigmoid(x_ref[r0:r0 + L, 256 * (d + 1):256 * (d + 2)])
            k = 1.0 - f
            g = jnp.log(f)
        return q, k, _dot_exact_lhs(tri_bf[d], g)

    lasts = []

    def scale(u, p):
        m, d, _ = u
        q, k, cum = p
        last = cum[0:1, :] if d else cum[L - 1:L, :]
        lasts.append(last)
        q_in = q * jnp.exp(cum)
        k_out = (k * jnp.exp(-cum)).astype(BF16)
        k_end_t = (k * jnp.exp(last - cum)).T.astype(BF16)
        qs = jnp.concatenate([jnp.where(hm, q_in, 0.0) for hm in head_mask], axis=0).astype(BF16)
        dcol = jnp.sum(jnp.where(eye, jnp.exp(last), 0.0), axis=1, keepdims=True)
        v = v_refs[m][d][row0[u]:row0[u] + L, :].astype(BF16)
        return qs, k_out, k_end_t, dcol, v

    def products(u, p):
        qs, k_out, k_end_t, dcol, v = p
        upd = jnp.concatenate(
            [_dot(k_end_t[LIN_DK * h:LIN_DK * (h + 1)], v[:, LIN_DV * h:LIN_DV * (h + 1)]) for h in range(LIN_HEADS)],
            axis=0)
        return qs, _dot_nt(qs, k_out), upd, dcol, v

    def exact_scores(u):
        d = u[1]
        q, k, cum = gates(u, None)
        fb_ref[0], fb_ref[1], fb_ref[2] = q, k, cum
        head_sum = jnp.where((lax.broadcasted_iota(jnp.int32, (LIN_QK, 128), 0) >> 6)
                             == lax.broadcasted_iota(jnp.int32, (LIN_QK, 128), 1), 1.0, 0.0).astype(BF16)
        t_col = lax.broadcasted_iota(jnp.int32, (L, 1), 0)
        s_row = lax.broadcasted_iota(jnp.int32, (1, L), 1)

        def key_row(s, acc):
            ks = fb_ref[1, pl.ds(s, 1), :]
            cs = fb_ref[2, pl.ds(s, 1), :]
            w = fb_ref[0] * ks * jnp.exp(jnp.minimum(fb_ref[2] - cs, 0.0))
            cols = _dot_exact_rhs(w, head_sum)
            cols = jnp.where((t_col <= s) if d else (t_col >= s), cols, 0.0)
            return acc + jnp.concatenate(
                [jnp.where(s_row == s, cols[:, h:h + 1], 0.0) for h in range(LIN_HEADS)], axis=0)

        return lax.fori_loop(0, L, key_row, jnp.zeros((LIN_HEADS * L, L), F32)).astype(BF16)

    def mask(u, p):
        qs, a, upd, dcol, v = p
        a = exact_scores(u) if exact else jnp.where(tri4[u[1]], a, 0.0).astype(BF16)
        return qs, a, upd, dcol, v

    def intra(u, p):
        qs, a, upd, dcol, v = p
        o_intra = jnp.concatenate(
            [_dot(a[h * L:(h + 1) * L], v[:, LIN_DV * h:LIN_DV * (h + 1)]) for h in range(LIN_HEADS)], axis=0)
        return qs, o_intra, upd, dcol

    s = [st_ref[i] for i in range(4)]

    def outputs(u, p):
        m, d, _ = u
        qs, o_intra, upd, dcol = p
        i = 2 * m + d
        o = o_intra + _dot(qs, s[i].astype(BF16))
        o_refs[m][d][row0[u]:row0[u] + L, :] = jnp.concatenate(
            [o[h * L:(h + 1) * L] for h in range(LIN_HEADS)], axis=1).astype(BF16)
        s[i] = dcol * s[i] + upd

    _skewed(units, (gates, scale, products, mask, intra, outputs))
    for i in range(4):
        st_ref[i] = s[i]

    @pl.when(is_prompt)
    def _():
        for i, n_ref in enumerate((ngf_ref, ngb_ref, nhf_ref, nhb_ref)):
            n_ref[0] = s[i].reshape(LIN_HEADS, LIN_DK, LIN_DV)
    worst_ref[0] = jnp.broadcast_to(functools.reduce(jnp.minimum, lasts), (SUBLANES, LIN_QK))


def _lin(gqk, gv, hqf, hi, states, wa, ba, lbl):
    st_spec = pl.BlockSpec((1, LIN_QK, LIN_DV), lambda j: (_state_idx(j), 0, 0))
    so_spec = pl.BlockSpec((1, LIN_HEADS, LIN_DK, LIN_DV),
                           lambda j: (jnp.minimum(j, BATCH - 1), 0, 0, 0))
    so_shape = jax.ShapeDtypeStruct((BATCH, LIN_HEADS, LIN_DK, LIN_DV), F32)
    o_shape = jax.ShapeDtypeStruct((N_TOK, LIN_V), BF16)
    worst_spec = pl.BlockSpec((1, SUBLANES, LIN_QK), lambda j: (j, 0, 0))
    worst_shape = jax.ShapeDtypeStruct((N_BLK, SUBLANES, LIN_QK), F32)
    packed = [s.reshape(DEC_BATCH, LIN_QK, LIN_DV) for s in states]

    def run(exact):
        return pl.pallas_call(
            functools.partial(_lin_kernel, exact=exact),
            grid=(N_BLK,),
            in_specs=[_fwd_spec(640), _fwd_spec(LIN_V), _bwd_spec(640), _bwd_spec(LIN_V),
                      _fwd_spec(768), _fwd_spec(LIN_V), _bwd_spec(768), _bwd_spec(LIN_V)]
                     + [st_spec] * 4 + [_const_spec(wa.shape), _const_spec(ba.shape), _const_spec(lbl.shape)],
            out_specs=[_fwd_spec(LIN_V), _bwd_spec(LIN_V), _fwd_spec(LIN_V), _bwd_spec(LIN_V)] + [so_spec] * 4
                      + [worst_spec],
            out_shape=[o_shape] * 4 + [so_shape] * 4 + [worst_shape],
            scratch_shapes=[pltpu.VMEM((4, LIN_QK, LIN_DV), F32), pltpu.VMEM((3, LIN_L, LIN_QK), F32)],
            compiler_params=_cparams(),
            name="lin_exact" if exact else "lin",
        )(gqk, gv, gqk, gv, hqf, hi, hqf, hi, *packed, wa, ba, lbl)

    fast = run(False)
    outs = lax.cond(jnp.min(fast[8]) < -LIN_SAFE_LOG_DECAY, lambda: tuple(run(True)[:8]), lambda: tuple(fast[:8]))
    return outs[:4], outs[4:]


def _f32(ref):
    return ref[...].astype(F32)


BF16_ROWS = 16


def _ext_rows(refs):
    t_ref, n_ref, p_ref = refs
    p = _f32(p_ref)
    return jnp.concatenate([_f32(t_ref), _f32(n_ref)[0:SUBLANES], p[p.shape[0] - SUBLANES:]], axis=0)


def _mix0(refs):
    of, ob, z, yb, (nw_ref,) = refs[0:3], refs[3:6], refs[6:9], refs[9:12], refs[12:]
    ya = (_ext_rows(of) + _ext_rows(ob)) * _ext_rows(z)
    ms = jnp.mean(ya * ya, axis=-1, keepdims=True)
    ya = ya * lax.rsqrt(ms + NORM_EPS) * nw_ref[...]
    return ya.astype(BF16), _ext_rows(yb).astype(BF16)


def _head_rms128(o, w):
    parts = []
    for h in range(LIN_HEADS):
        oh = o[:, LIN_DV * h:LIN_DV * (h + 1)]
        ms = jnp.mean(oh * oh, axis=-1, keepdims=True)
        parts.append(oh * lax.rsqrt(ms + NORM_EPS) * w)
    return jnp.concatenate(parts, axis=-1)


def _mix1(refs):
    gf, gb, gg, hf, hb, hg, (gw_ref, hw_ref) = (refs[0:3], refs[3:6], refs[6:9], refs[9:12], refs[12:15],
                                               refs[15:18], refs[18:])
    yc = _head_rms128(_ext_rows(gf) + _ext_rows(gb), gw_ref[...]) * _ext_rows(gg)
    yd = _head_rms128(_ext_rows(hf) + _ext_rows(hb), hw_ref[...]) * _ext_rows(hg)
    return yc.astype(BF16), yd.astype(BF16)


def _outffn_kernel(*refs, layer, n_out):
    i = pl.program_id(0)
    is_prompt = i < PROMPT_TILES
    if layer == 0:
        xp_ref, xs_ref, xn_ref, xv_ref = refs[0:4]
        x0 = jnp.concatenate([_tile_of(xp_ref, xs_ref), xn_ref[...], xv_ref[...]], axis=0)
        n_mix, mix = 13, _mix0
        refs = refs[4:]
    else:
        x0 = _ext_rows(refs[0:3])
        n_mix, mix = 20, _mix1
        refs = refs[3:]
    m_ref, refs = refs[0], refs[1:]
    ya, yb = mix(refs[:n_mix])
    wo_ref, nw_ref, wu_ref, cw_ref, cb_ref, wd_ref = refs[n_mix:n_mix + 6]
    rest = refs[n_mix + 6:]
    o_refs, (hs_ref, act_ref, x1_ref) = rest[:n_out], rest[n_out:]

    x1 = x0 + m_ref[0, 2:3, :] * (_dot(ya, wo_ref[0:512, :]) + _dot(yb, wo_ref[512:1024, :]))
    x1_ref[...] = x1[0:TM]
    nw = nw_ref[...]
    sh = m_ref[0, 3:4, :]
    sc = m_ref[0, 4:5, :]
    hs_ref[0:TM, :] = _norm_mod(x1[0:TM], nw, sh, sc).astype(BF16)
    hs_ref[TM:TM_EXT, :] = (_norm_mod(x1[TM:TM_EXT], nw, sh, sc) * _halo_keep(i)).astype(BF16)

    def conv_cols(c0):
        cs = slice(c0, c0 + FFN_CH)
        return _dwconv_tile(_dot(hs_ref[...], wu_ref[:, cs]), cw_ref[:, cs], cb_ref[:, cs], i)

    for c in range(FFN_DIM // FFN_CH):
        a = conv_cols(c * FFN_CH)
        b = conv_cols(FFN_DIM + c * FFN_CH)
        act_ref[:, c * FFN_CH:(c + 1) * FFN_CH] = (_silu(a) * b).astype(BF16)
    out = x1_ref[...] + m_ref[0, 5:6, :] * _dot(act_ref[...], wd_ref[...])
    if len(o_refs) == 1:
        o_refs[0][...] = out
    else:
        @pl.when(is_prompt)
        def _():
            o_refs[0][...] = out

        @pl.when(jnp.logical_not(is_prompt))
        def _():
            o_refs[1][...] = out


def _outffn(x_args, mix_arrays, mix_params, mrows, w_out, nw, wu, cw, cb, wd, layer):
    per_layer = lambda *shape: pl.BlockSpec((None,) + shape, lambda i: (layer,) + (0,) * len(shape))
    single = dict(pipeline_mode=pl.Buffered(1))
    if layer == 0:
        xp, xs = x_args
        x_ops = [xp, xs, xs, xs]
        x_specs = [_prompt_tile_spec(), _sample_tile_spec()] + _halo_specs(D_MODEL, SUBLANES, N_SAMPLE, PROMPT_TILES)
        out_specs = [_row_spec(D_MODEL)]
        out_shape = [jax.ShapeDtypeStruct((N_TOK, D_MODEL), F32)]
    else:
        x_ops = [x_args[0]] * 3
        x_specs = [_row_spec(D_MODEL)] + _halo_specs(D_MODEL, SUBLANES, N_TOK)
        out_specs = [_prompt_tile_spec(), _sample_tile_spec()]
        out_shape = [jax.ShapeDtypeStruct((N_PROMPT, D_MODEL), F32), jax.ShapeDtypeStruct((N_SAMPLE, D_MODEL), F32)]
    mix_ops, mix_specs = [], []
    for a in mix_arrays:
        mix_ops += [a] * 3
        mix_specs += [_row_spec(a.shape[1])] + _halo_specs(a.shape[1], BF16_ROWS, N_TOK)
    return pl.pallas_call(
        functools.partial(_outffn_kernel, layer=layer, n_out=len(out_specs)),
        grid=(N_TILES,),
        in_specs=x_specs + [_mod_spec()] + mix_specs + [_const_spec(p.shape) for p in mix_params]
                 + [pl.BlockSpec((D_MODEL, D_MODEL), lambda i: (0, 0), **single), _const_spec((1, D_MODEL)),
                    pl.BlockSpec((None, D_MODEL, 2 * FFN_DIM), lambda i: (layer, 0, 0), **single),
                    per_layer(3, 2 * FFN_DIM), per_layer(1, 2 * FFN_DIM),
                    pl.BlockSpec((None, FFN_DIM, D_MODEL), lambda i: (layer, 0, 0), **single)],
        out_specs=out_specs,
        out_shape=out_shape,
        scratch_shapes=[pltpu.VMEM((TM_EXT, D_MODEL), BF16), pltpu.VMEM((TM, FFN_DIM), BF16),
                        pltpu.VMEM((TM, D_MODEL), F32)],
        compiler_params=_cparams(),
        name="outffn",
    )(*x_ops, mrows, *mix_ops, *mix_params, w_out, nw, wu, cw, cb, wd)


def _pad_lanes(v, width=128):
    v = v.reshape(1, -1)
    return jnp.pad(v, ((0, 0), (0, width - v.shape[1])))


def kernel(x_prompt, x_sample, cache_na_k_l0, cache_na_v_l0, state_ssd_fwd_l0, state_ssd_bwd_l0,
           state_gla_fwd_l1, state_gla_bwd_l1, state_hgrn_fwd_l1, state_hgrn_bwd_l1, c,
           c_ctx, w_ada, b_ada, norm_w, ffn_w_up, ffn_conv_w, ffn_conv_b, ffn_w_down,
           w_in_l0, w_out_l0, ssd_conv_w_l0, ssd_conv_b_l0, ssd_dt_bias_l0, ssd_a_log_l0, ssd_d_l0,
           ssd_norm_w_l0, na_q_norm_l0, na_k_norm_l0, na_rpb_l0,
           w_in_l1, w_out_l1, gla_wa2_l1, gla_ba2_l1, gla_norm_w_l1, hgrn_lb_logits, hgrn_norm_w_l1):
    xp = x_prompt.reshape(N_PROMPT, D_MODEL)
    xs = x_sample.reshape(N_SAMPLE, D_MODEL)

    cvec8 = jnp.zeros((SUBLANES, D_MODEL), F32).at[0:DEC_BATCH].set(c).at[CTX_MOD_ROW].set(c_ctx)
    mods = _mods(cvec8, w_ada, b_ada)
    mods = mods.reshape(2, SUBLANES, 6, D_MODEL)
    mods = jnp.pad(mods, ((0, 0), (0, 0), (0, SUBLANES - 6), (0, 0)))

    zpad = lambda n: jnp.zeros((D_MODEL, n), BF16)
    w0b, w1b = w_in_l0.astype(BF16), w_in_l1.astype(BF16)
    w0 = jnp.concatenate([w0b[:, :1296], zpad(112), w0b[:, 1296:]], axis=1)
    w1 = jnp.concatenate([w1b[:, 0:512], w1b[:, 1536:1568], zpad(96), w1b[:, 512:1536], w1b[:, 1568:3360]], axis=1)
    seg = jnp.kron(jnp.eye(NA_HEADS, dtype=F32), jnp.full((NA_HEAD_DIM, NA_HEAD_DIM), 1.0 / NA_HEAD_DIM, F32)).astype(BF16)

    m0 = mods[0]
    z, xc, dt, qn, knb, vb, kn, v = _inproj0(
        xp, xs, m0, norm_w[0, 0].reshape(1, D_MODEL), w0, seg,
        jnp.tile(na_q_norm_l0, NA_HEADS).reshape(1, NA_WIDTH), jnp.tile(na_k_norm_l0, NA_HEADS).reshape(1, NA_WIDTH),
        ssd_conv_w_l0, ssd_conv_b_l0.reshape(1, SSD_XBC))
    of, ob, ssd_f, ssd_b = _ssd(
        xc, dt, state_ssd_fwd_l0, state_ssd_bwd_l0, _pad_lanes(ssd_dt_bias_l0), _pad_lanes(ssd_a_log_l0),
        jnp.repeat(ssd_d_l0, SSD_HEAD_DIM).reshape(1, SSD_WIDTH))
    yb, na_k, na_v = _na_ctx(qn, kn, v)
    yb = _na_lat(qn, knb, vb, cache_na_k_l0, cache_na_v_l0, _na_bias_table(na_rpb_l0), yb)
    ffn_weights = (ffn_w_up.astype(BF16), ffn_conv_w, ffn_conv_b[:, None, :], ffn_w_down.astype(BF16))
    x, = _outffn((xp, xs), (of, ob, z, yb), (ssd_norm_w_l0.reshape(1, SSD_WIDTH),), m0, w_out_l0.astype(BF16),
                 norm_w[0, 1].reshape(1, D_MODEL), *ffn_weights, layer=0)

    m1 = mods[1]
    gqk, gv, gg, hqf, hi, hg = _inproj1(x, m1, norm_w[1, 0].reshape(1, D_MODEL), w1)
    wa_pad = jnp.zeros((2, 128, LIN_QK), F32)
    wa_pad = wa_pad.at[0, 0:GLA_RANK].set(gla_wa2_l1[0]).at[1, GLA_RANK:2 * GLA_RANK].set(gla_wa2_l1[1])
    (gf, gb, hf, hb), (gla_f, gla_b, hgrn_f, hgrn_b) = _lin(
        gqk, gv, hqf, hi, (state_gla_fwd_l1, state_gla_bwd_l1, state_hgrn_fwd_l1, state_hgrn_bwd_l1),
        wa_pad, gla_ba2_l1.reshape(2, 1, LIN_QK), hgrn_lb_logits)
    y_p, y_s = _outffn((x,), (gf, gb, gg, hf, hb, hg),
                       (gla_norm_w_l1.reshape(1, LIN_DV), hgrn_norm_w_l1.reshape(1, LIN_DV)), m1,
                       w_out_l1.astype(BF16), norm_w[1, 1].reshape(1, D_MODEL), *ffn_weights, layer=1)
    return (y_p.reshape(BATCH, SEQ, D_MODEL), y_s.reshape(DEC_BATCH, DEC_SEQ, D_MODEL),
            na_k, na_v, ssd_f, ssd_b, gla_f, gla_b, hgrn_f, hgrn_b)
```

```python
import functools

import jax
import jax.numpy as jnp
from jax import lax
from jax.experimental import pallas as pl
from jax.experimental.pallas import tpu as pltpu

F32 = jnp.float32
BF16 = jnp.bfloat16
HIGHEST = lax.Precision.HIGHEST

D_MODEL = 1024
BATCH = 32
SEQ = 256
DEC_BATCH = 4
DEC_SEQ = 4096
PAST_LEN = 256
GRID_W = 64
NORM_EPS = 1e-6
N_PROMPT = BATCH * SEQ
N_SAMPLE = DEC_BATCH * DEC_SEQ
N_TOK = N_PROMPT + N_SAMPLE

SSD_HEADS = 8
SSD_HEAD_DIM = 64
SSD_STATE = 64
SSD_WIDTH = 512
SSD_BC = 128
SSD_XBC = 768
NA_HEADS = 8
NA_HEAD_DIM = 64
NA_WIDTH = 512
NA_WIN_ROWS = 8
NA_WIN_COLS = 16
NA_SCALE = NA_HEAD_DIM ** -0.5
GLA_DK = 64
GLA_RANK = 16
GLA_GATE_NORM = 16.0
FFN_DIM = 2816

VMEM_LIMIT = 56 * 1024 * 1024
SUBLANES = 8

TM = 512
N_TILES = N_TOK // TM
PROMPT_TILES = N_PROMPT // TM
TILES_PER_SAMPLE = DEC_SEQ // TM
CTX_MOD_ROW = DEC_BATCH

RB = 256
N_BLK = N_TOK // RB
PROMPT_BLKS = N_PROMPT // RB
BLKS_PER_SAMPLE = DEC_SEQ // RB
SSD_L = 128
LIN_L = 64
LIN_SAFE_LOG_DECAY = 60.0
NA_R = 8
NA_ROWS_PER_ITER = 8
FFN_CH = 256
W0_COLS = 2944
W1_COLS = 3456


def _cparams(n_axes=1):
    return pltpu.CompilerParams(dimension_semantics=("arbitrary",) * n_axes,
                                vmem_limit_bytes=VMEM_LIMIT)


def _const_spec(shape):
    nd = len(shape)
    return pl.BlockSpec(shape, lambda *_: (0,) * nd)


def _sigmoid(x):
    return 1.0 / (1.0 + jnp.exp(-x))


def _silu(x):
    return x * _sigmoid(x)


def _softplus(x):
    return jnp.maximum(x, 0.0) + jnp.log(1.0 + jnp.exp(-jnp.abs(x)))


def _mod_row(i):
    return jnp.where(i < PROMPT_TILES, CTX_MOD_ROW, (i - PROMPT_TILES) // TILES_PER_SAMPLE)


def _dot(a, b, **kw):
    return jnp.dot(a, b, preferred_element_type=F32, **kw)


def _dot_nt(a, b):
    return lax.dot_general(a, b, (((1,), (1,)), ((), ())), preferred_element_type=F32)


def _split2(x):
    hi = x.astype(BF16)
    lo = (x - hi.astype(F32)).astype(BF16)
    return hi, lo


def _split3(x):
    hi = x.astype(BF16)
    r = x - hi.astype(F32)
    mid = r.astype(BF16)
    lo = (r - mid.astype(F32)).astype(BF16)
    return hi, mid, lo


def _dot_exact_lhs(a_bf, x):
    hi, lo = _split2(x)
    return _dot(a_bf, hi) + _dot(a_bf, lo)


def _dot_exact_rhs(x, b_bf):
    hi, lo = _split2(x)
    return _dot(hi, b_bf) + _dot(lo, b_bf)


def _skewed(units, stages):
    results = {}
    for step in range(len(units) + len(stages) - 1):
        for k, stage in enumerate(stages):
            i = step - k
            if 0 <= i < len(units):
                results[(k, i)] = stage(units[i], results.pop((k - 1, i), None))


MODS_NB = 1536


def _mods_kernel(c_ref, w_ref, b_ref, o_ref):
    s_hi, s_lo = _split2(_silu(c_ref[...]))
    w_hi, w_lo = _split2(w_ref[0])
    o_ref[0] = _dot(s_hi, w_hi) + _dot(s_lo, w_hi) + _dot(s_hi, w_lo) + b_ref[0]


def _mods(cvec8, w_ada, b_ada):
    depth = w_ada.shape[0]
    nb = 6 * D_MODEL // MODS_NB
    return pl.pallas_call(
        _mods_kernel,
        grid=(depth, nb),
        in_specs=[
            _const_spec((SUBLANES, D_MODEL)),
            pl.BlockSpec((1, D_MODEL, MODS_NB), lambda l, j: (l, 0, j)),
            pl.BlockSpec((1, 1, MODS_NB), lambda l, j: (l, 0, j)),
        ],
        out_specs=pl.BlockSpec((1, SUBLANES, MODS_NB), lambda l, j: (l, 0, j)),
        out_shape=jax.ShapeDtypeStruct((depth, SUBLANES, 6 * D_MODEL), F32),
        compiler_params=_cparams(2),
        name="mods",
    )(cvec8, w_ada, b_ada.reshape(depth, 1, 6 * D_MODEL))


def _norm_mod(x, nw, sh, sc):
    ms = jnp.mean(x * x, axis=-1, keepdims=True)
    y = x * lax.rsqrt(ms + NORM_EPS) * nw
    return y * (1.0 + sc) + sh


def _tile_of(xp_ref, xs_ref):
    return jnp.where(pl.program_id(0) < PROMPT_TILES, xp_ref[...], xs_ref[...])


HALO_ROWS = 2 * SUBLANES
TM_EXT = TM + HALO_ROWS


def _halo_keep(i):
    seq = jnp.where(i < PROMPT_TILES, SEQ, DEC_SEQ)
    keep_prev = jnp.where(((i * TM) & (seq - 1)) == 0, 0.0, 1.0)
    keep_next = jnp.where(((i * TM + TM) & (seq - 1)) == 0, 0.0, 1.0)
    hrow = lax.broadcasted_iota(jnp.int32, (HALO_ROWS, 1), 0)
    return jnp.where(hrow < SUBLANES, keep_next, keep_prev)


def _dwconv_tile(u_all, w, bias, i):
    um = pltpu.roll(u_all, 1, axis=0)[0:TM] * w[0:1]
    up = pltpu.roll(u_all, TM_EXT - 1, axis=0)[0:TM] * w[2:3]
    out = bias + um + u_all[0:TM] * w[1:2] + up
    edge_on = jnp.where(i < PROMPT_TILES, 1.0, 0.0)
    erow = lax.broadcasted_iota(jnp.int32, (HALO_ROWS, 1), 0)
    at_first = jnp.where(erow == SUBLANES, edge_on, 0.0)
    at_last = jnp.where(erow == SUBLANES - 1, edge_on, 0.0)
    pieces, r = [], 0
    for edge in range(SEQ, TM, SEQ):
        sl = slice(edge - SUBLANES, edge + SUBLANES)
        pieces += [out[r:edge - SUBLANES], out[sl] - at_first * um[sl] - at_last * up[sl]]
        r = edge + SUBLANES
    return jnp.concatenate(pieces + [out[r:TM]], axis=0)


TOKEN_STRIDE = TM // SUBLANES


def _to_strided(x):
    c = x.shape[1]
    return x.reshape(SUBLANES, TOKEN_STRIDE, c).swapaxes(0, 1).reshape(TM, c)


def _from_strided(x):
    c = x.shape[1]
    return x.reshape(TOKEN_STRIDE, SUBLANES, c).swapaxes(0, 1).reshape(TM, c)


def _dwconv_strided(u_all, w, bias, i):
    u = u_all[0:TM]
    after = jnp.concatenate([u[1:SUBLANES], u_all[TM:TM + 1]], axis=0)
    before = jnp.concatenate([u_all[TM_EXT - 1:TM_EXT], u[TM - SUBLANES:TM - 1]], axis=0)
    um = jnp.concatenate([before, u[0:TM - SUBLANES]], axis=0) * w[0:1]
    up = jnp.concatenate([u[SUBLANES:TM], after], axis=0) * w[2:3]
    out = bias + um + u * w[1:2] + up
    edge_on = jnp.where(i < PROMPT_TILES, 1.0, 0.0)
    r8 = lax.broadcasted_iota(jnp.int32, (SUBLANES, 1), 0)
    at_first = jnp.zeros((SUBLANES, 1), F32)
    at_last = jnp.zeros((SUBLANES, 1), F32)
    for edge in range(SEQ, TM, SEQ):
        at_first = at_first + jnp.where(r8 == edge // TOKEN_STRIDE, edge_on, 0.0)
        at_last = at_last + jnp.where(r8 == (edge - 1) // TOKEN_STRIDE, edge_on, 0.0)
    head = out[0:SUBLANES] - at_first * um[0:SUBLANES]
    tail = out[TM - SUBLANES:TM] - at_last * up[TM - SUBLANES:TM]
    return jnp.concatenate([head, out[SUBLANES:TM - SUBLANES], tail], axis=0)


def _inproj0_kernel(xp_ref, xs_ref, xn_ref, xv_ref, m_ref, nw_ref, w_ref, seg_ref, qw_ref, kw_ref, cw_ref, cb_ref,
                    z_ref, xc_ref, dt_ref, qn_ref, knb_ref, vb_ref, kn_ref, v_ref, hs_ref):
    i = pl.program_id(0)
    nw, sh, sc = nw_ref[...], m_ref[0, 0:1, :], m_ref[0, 1:2, :]
    hs_ref[0:TM, :] = _norm_mod(_tile_of(xp_ref, xs_ref), nw, sh, sc).astype(BF16)
    halo = _norm_mod(jnp.concatenate([xn_ref[...], xv_ref[...]], axis=0), nw, sh, sc)
    hs_ref[TM:TM_EXT, :] = (halo * _halo_keep(i)).astype(BF16)
    q = _dot(hs_ref[0:TM, :], w_ref[:, 1408:1920])
    k = _dot(hs_ref[0:TM, :], w_ref[:, 1920:2432])
    xc_ref[...] = _silu(_dwconv_tile(_dot(hs_ref[...], w_ref[:, 512:1280]), cw_ref[...], cb_ref[...], i))
    ms_q = _dot((q * q).astype(BF16), seg_ref[...])
    ms_k = _dot((k * k).astype(BF16), seg_ref[...])
    z_ref[...] = _silu(_dot(hs_ref[0:TM, :], w_ref[:, 0:512])).astype(BF16)
    dt_ref[...] = _dot(hs_ref[0:TM, :], w_ref[:, 1280:1408])
    v = _dot(hs_ref[0:TM, :], w_ref[:, 2432:2944])
    vb_ref[...] = v.astype(BF16)
    qn_ref[...] = (q * lax.rsqrt(ms_q + NORM_EPS) * qw_ref[...] * NA_SCALE).astype(BF16)
    kn = k * lax.rsqrt(ms_k + NORM_EPS) * kw_ref[...]
    knb_ref[...] = kn.astype(BF16)

    @pl.when(pl.program_id(0) < PROMPT_TILES)
    def _():
        kn_ref[...] = kn
        v_ref[...] = v


def _row_spec(width):
    return pl.BlockSpec((TM, width), lambda i: (i, 0))


def _mod_spec():
    return pl.BlockSpec((1, SUBLANES, D_MODEL), lambda i: (_mod_row(i), 0, 0))


def _prompt_tile_spec(width=D_MODEL):
    return pl.BlockSpec((TM, width), lambda i: (jnp.minimum(i, PROMPT_TILES - 1), 0))


def _sample_tile_spec(width=D_MODEL):
    return pl.BlockSpec((TM, width), lambda i: (jnp.maximum(i - PROMPT_TILES, 0), 0))


def _halo_specs(width, rows, total_rows, first_tile=0):
    per_tile = TM // rows
    last = total_rows // rows - 1
    nxt = pl.BlockSpec((rows, width), lambda i: (jnp.clip((i - first_tile + 1) * per_tile, 0, last), 0))
    prv = pl.BlockSpec((rows, width), lambda i: (jnp.clip((i - first_tile) * per_tile - 1, 0, last), 0))
    return [nxt, prv]


def _inproj0(xp, xs, mrows, nw, w0, seg, qw, kw, conv_w, conv_b):
    widths = (512, 768, 128, 512, 512, 512)
    dtypes = (BF16, F32, F32, BF16, BF16, BF16)
    nxt, prv = _halo_specs(D_MODEL, SUBLANES, N_SAMPLE, PROMPT_TILES)
    return pl.pallas_call(
        _inproj0_kernel,
        grid=(N_TILES,),
        in_specs=[
            _prompt_tile_spec(), _sample_tile_spec(), nxt, prv, _mod_spec(), _const_spec((1, D_MODEL)),
            _const_spec((D_MODEL, W0_COLS)), _const_spec((512, 512)),
            _const_spec((1, 512)), _const_spec((1, 512)), _const_spec((3, SSD_XBC)), _const_spec((1, SSD_XBC)),
        ],
        out_specs=[_row_spec(w) for w in widths] + [_prompt_tile_spec(NA_WIDTH)] * 2,
        out_shape=[jax.ShapeDtypeStruct((N_TOK, w), d) for w, d in zip(widths, dtypes)]
                  + [jax.ShapeDtypeStruct((N_PROMPT, NA_WIDTH), F32)] * 2,
        scratch_shapes=[pltpu.VMEM((TM_EXT, D_MODEL), BF16)],
        compiler_params=_cparams(),
        name="inproj0",
    )(xp, xs, xs, xs, mrows, nw, w0, seg, qw, kw, conv_w, conv_b)


def _inproj1_kernel(x_ref, m_ref, nw_ref, w_ref, gqk_ref, gv_ref, gg_ref, hqf_ref, hi_ref, hg_ref, hs_ref):
    hs_ref[...] = _norm_mod(x_ref[...], nw_ref[...], m_ref[0, 0:1, :], m_ref[0, 1:2, :]).astype(BF16)
    gqk_ref[...] = _dot(hs_ref[...], w_ref[:, 0:640])
    gv_ref[...] = _dot(hs_ref[...], w_ref[:, 640:1152]).astype(BF16)
    gg_ref[...] = _silu(_dot(hs_ref[...], w_ref[:, 1152:1664])).astype(BF16)
    hqf_ref[...] = _dot(hs_ref[...], w_ref[:, 1664:2432])
    hi_ref[...] = _dot(hs_ref[...], w_ref[:, 2432:2944]).astype(BF16)
    hg_ref[...] = _silu(_dot(hs_ref[...], w_ref[:, 2944:3456])).astype(BF16)


def _inproj1(x, mrows, nw, w1):
    widths = (640, 512, 512, 768, 512, 512)
    dtypes = (F32, BF16, BF16, F32, BF16, BF16)
    return pl.pallas_call(
        _inproj1_kernel,
        grid=(N_TILES,),
        in_specs=[_row_spec(D_MODEL), _mod_spec(), _const_spec((1, D_MODEL)),
                  _const_spec((D_MODEL, W1_COLS))],
        out_specs=[_row_spec(w) for w in widths],
        out_shape=[jax.ShapeDtypeStruct((N_TOK, w), d) for w, d in zip(widths, dtypes)],
        scratch_shapes=[pltpu.VMEM((TM, D_MODEL), BF16)],
        compiler_params=_cparams(),
        name="inproj1",
    )(x, mrows, nw, w1)


def _seq_of(j):
    jj = j - PROMPT_BLKS
    is_p = j < PROMPT_BLKS
    seq = jnp.where(is_p, j, BATCH + jj // BLKS_PER_SAMPLE)
    blk = jnp.where(is_p, 0, jj % BLKS_PER_SAMPLE)
    nblk = jnp.where(is_p, 1, BLKS_PER_SAMPLE)
    return seq, blk, nblk


def _bwd_blk(j):
    _, blk, nblk = _seq_of(j)
    return j - blk + (nblk - 1 - blk)


def _state_idx(j):
    seq, _, _ = _seq_of(j)
    return jnp.maximum(seq - BATCH, 0)


def _fwd_spec(width):
    return pl.BlockSpec((RB, width), lambda j: (j, 0))


def _bwd_spec(width):
    return pl.BlockSpec((RB, width), lambda j: (_bwd_blk(j), 0))


def _tri(n, upper):
    r = lax.broadcasted_iota(jnp.int32, (n, n), 0)
    c = lax.broadcasted_iota(jnp.int32, (n, n), 1)
    return (c >= r) if upper else (c <= r)


def _diag_blocks(x, rows):
    lh = lax.broadcasted_iota(jnp.int32, (1, 4 * SSD_HEAD_DIM), 1) >> 6
    out = x[3 * rows:4 * rows]
    for h in (2, 1, 0):
        out = jnp.where(lh == h, x[h * rows:(h + 1) * rows], out)
    return out


def _ssd_kernel(xf_ref, dtf_ref, xb_ref, dtb_ref, s0f_ref, s0b_ref, dtbias_ref, alog_ref, dsk_ref, ex_ref,
                of_ref, ob_ref, sf_ref, sb_ref, st_ref):
    j = pl.program_id(0)
    _, blk, _ = _seq_of(j)
    is_prompt = j < PROMPT_BLKS

    @pl.when(blk == 0)
    def _():
        st_ref[0] = jnp.where(is_prompt, 0.0, s0f_ref[0])
        st_ref[1] = jnp.where(is_prompt, 0.0, s0b_ref[0])

    a_neg = -jnp.exp(alog_ref[...])
    a_col = jnp.broadcast_to(a_neg, (SUBLANES, 128)).T
    L = SSD_L
    nch = RB // L
    xc_refs = (xf_ref, xb_ref)
    dt_refs = (dtf_ref, dtb_ref)
    o_refs = (of_ref, ob_ref)

    units = [(d, c) for c in range(nch) for d in range(2)]
    row0 = {(d, c): ((nch - 1 - c) * L if d else c * L) for d, c in units}
    tri = [_tri(L, False), _tri(L, True)]
    tri_bf = [jnp.where(t, 1.0, 0.0).astype(BF16) for t in tri]
    lane = lax.broadcasted_iota(jnp.int32, (1, SSD_BC), 1)
    gmask = [lane < SSD_STATE, lane >= SSD_STATE]
    eye_bf = jnp.where(tri[0] & tri[1], 1.0, 0.0).astype(BF16)
    zero_blk = jnp.zeros((SSD_STATE, 4 * SSD_HEAD_DIM), F32)

    def gates(u, _):
        d, r0 = u[0], row0[u]
        dt = _softplus(dt_refs[d][r0:r0 + L, :] + dtbias_ref[...])
        cum = _dot_exact_lhs(tri_bf[d], dt * a_neg)
        dt8 = dt.T[8 * d:8 * d + 8]
        ct8 = _dot_exact_rhs(dt8 * a_col[8 * d:8 * d + 8, 0:1], tri_bf[1 - d])
        parts = _split3(cum[0:SUBLANES] if d else cum[L - SUBLANES:L])
        last_x = _dot(parts[0], ex_ref[d]) + _dot(parts[1], ex_ref[d]) + _dot(parts[2], ex_ref[d])
        e_last_x = jnp.exp(last_x[0:1] if d else last_x[SUBLANES - 1:SUBLANES])
        last_col = ct8[:, 0:1] if d else ct8[:, L - 1:L]
        f_end = jnp.exp(last_col - ct8) * dt8
        bm = xc_refs[d][r0:r0 + L, 512:640].astype(BF16)
        cm = xc_refs[d][r0:r0 + L, 640:768]
        b_t = _dot_nt(eye_bf, bm)
        cb_g = [_dot_nt(jnp.where(gmask[g], cm, 0.0).astype(BF16), bm) for g in range(2)]
        return cum, ct8, dt8, e_last_x, f_end, b_t, cb_g, cm

    def operands(u, p):
        d = u[0]
        cum, ct8, dt8, e_last_x, f_end, b_t, cb_g, cm = p
        lhs, k_t = [], []
        for g in range(2):
            ls, ks = [], []
            for hh in range(4):
                h = 4 * g + hh
                pc = jnp.broadcast_to(cum[:, 8 * d + h:8 * d + h + 1], (L, L))
                dec = jnp.exp(jnp.where(tri[d], pc - ct8[h:h + 1, :], -jnp.inf))
                w = cb_g[g] * dec * dt8[h:h + 1, :]
                ls.append(jnp.concatenate([w.astype(BF16), (cm * jnp.exp(pc)).astype(BF16)], axis=1))
                ks.append((b_t[SSD_STATE * g:SSD_STATE * (g + 1)] * f_end[h:h + 1, :]).astype(BF16))
            lhs.append(jnp.concatenate(ls, axis=0))
            k_t.append(jnp.concatenate(ks, axis=0))
        return lhs, k_t, e_last_x

    def state_update(u, p):
        d, r0 = u[0], row0[u]
        lhs, k_t, e_last_x = p
        x_bf = [xc_refs[d][r0:r0 + L, 256 * g:256 * (g + 1)].astype(BF16) for g in range(2)]
        up = [_diag_blocks(_dot(k_t[g], x_bf[g]), SSD_STATE) for g in range(2)]
        upd = jnp.concatenate([jnp.concatenate([up[0], zero_blk], axis=1),
                               jnp.concatenate([zero_blk, up[1]], axis=1)], axis=0)
        return lhs, x_bf, e_last_x, upd

    s = [st_ref[0], st_ref[1]]

    def outputs(u, p):
        d, r0 = u[0], row0[u]
        lhs, x_bf, e_last_x, upd = p
        s_bf = s[d].astype(BF16)
        o = jnp.concatenate(
            [_diag_blocks(_dot(lhs[g], jnp.concatenate([x_bf[g], s_bf[:, 256 * g:256 * (g + 1)]], axis=0)), L)
             for g in range(2)], axis=1)
        if d == 0:
            o = o + dsk_ref[...] * xf_ref[r0:r0 + L, 0:SSD_WIDTH]
        o_refs[d][r0:r0 + L, :] = o.astype(BF16)
        s[d] = e_last_x * s[d] + upd

    _skewed(units, (gates, operands, state_update, outputs))
    st_ref[0] = s[0]
    st_ref[1] = s[1]

    @pl.when(is_prompt)
    def _():
        sf_ref[0] = s[0]
        sb_ref[0] = s[1]


def _ssd_pack_state(s):
    b = s.shape[0]
    g = s.reshape(b, 2, 4, SSD_STATE, SSD_HEAD_DIM).transpose(0, 1, 3, 2, 4).reshape(b, 2, SSD_STATE, 256)
    z = jnp.zeros((b, SSD_STATE, 256), F32)
    return jnp.concatenate([jnp.concatenate([g[:, 0], z], axis=2), jnp.concatenate([z, g[:, 1]], axis=2)], axis=1)


def _ssd_unpack_state(s):
    b = s.shape[0]
    g = jnp.stack([s[:, 0:SSD_STATE, 0:256], s[:, SSD_STATE:, 256:512]], axis=1)
    g = g.reshape(b, 2, SSD_STATE, 4, SSD_HEAD_DIM).transpose(0, 1, 3, 2, 4)
    return g.reshape(b, SSD_HEADS, SSD_STATE, SSD_HEAD_DIM)


def _ssd(xc, dt, s0f, s0b, dtbias, alog, dsk):
    st_shape = (1, 2 * SSD_STATE, SSD_WIDTH)
    st_spec = pl.BlockSpec(st_shape, lambda j: (_state_idx(j), 0, 0))
    so_spec = pl.BlockSpec(st_shape, lambda j: (jnp.minimum(j, BATCH - 1), 0, 0))
    so_shape = jax.ShapeDtypeStruct((BATCH, 2 * SSD_STATE, SSD_WIDTH), F32)
    col = jnp.arange(128)[:, None]
    lane_head = jnp.arange(SSD_WIDTH)[None, :] // SSD_HEAD_DIM
    expand = jnp.stack([col == lane_head, col == SSD_HEADS + lane_head]).astype(BF16)
    of, ob, sf, sb = pl.pallas_call(
        _ssd_kernel,
        grid=(N_BLK,),
        in_specs=[
            _fwd_spec(SSD_XBC), _fwd_spec(128), _bwd_spec(SSD_XBC), _bwd_spec(128), st_spec, st_spec,
            _const_spec((1, 128)), _const_spec((1, 128)), _const_spec((1, SSD_WIDTH)),
            _const_spec((2, 128, SSD_WIDTH)),
        ],
        out_specs=[_fwd_spec(SSD_WIDTH), _bwd_spec(SSD_WIDTH), so_spec, so_spec],
        out_shape=[jax.ShapeDtypeStruct((N_TOK, SSD_WIDTH), BF16)] * 2 + [so_shape, so_shape],
        scratch_shapes=[pltpu.VMEM((2, 2 * SSD_STATE, SSD_WIDTH), F32)],
        compiler_params=_cparams(),
        name="ssd",
    )(xc, dt, xc, dt, _ssd_pack_state(s0f), _ssd_pack_state(s0b), dtbias, alog, dsk, expand)
    return of, ob, _ssd_unpack_state(sf), _ssd_unpack_state(sb)


NA_PAIRS = NA_HEADS // 2


def _stack_pair(qt):
    lower = lax.broadcasted_iota(jnp.int32, (1, 2 * NA_HEAD_DIM), 1) < NA_HEAD_DIM
    zero = jnp.zeros_like(qt)
    return jnp.concatenate([jnp.where(lower, qt, zero), jnp.where(lower, zero, qt)], axis=0)


def _unstack_pair(x, n):
    lower = lax.broadcasted_iota(jnp.int32, (1, 2 * NA_HEAD_DIM), 1) < NA_HEAD_DIM
    return jnp.where(lower, x[0:n], x[n:2 * n])


def _na_ctx_kernel(q_ref, k_ref, v_ref, y_ref, ko_ref, vo_ref):
    tiles = [slice(128 * i, 128 * (i + 1)) for i in range(NA_PAIRS)]
    s = [_dot_nt(_stack_pair(q_ref[:, ts]), k_ref[:, ts].astype(BF16)) for ts in tiles]
    p, l = [], []
    for i in range(NA_PAIRS):
        e = jnp.exp(s[i] - jnp.max(s[i], axis=-1, keepdims=True))
        l.append(jnp.sum(e, axis=-1, keepdims=True))
        p.append(e.astype(BF16))
    o = [_dot(p[i], v_ref[:, tiles[i]].astype(BF16)) for i in range(NA_PAIRS)]
    for i in range(NA_PAIRS):
        y = _unstack_pair(o[i], SEQ) / _unstack_pair(jnp.broadcast_to(l[i], o[i].shape), SEQ)
        y_ref[:, tiles[i]] = y.astype(BF16)
    for h in range(NA_HEADS):
        sl = slice(NA_HEAD_DIM * h, NA_HEAD_DIM * (h + 1))
        ko_ref[0, h] = k_ref[:, sl]
        vo_ref[0, h] = v_ref[:, sl]


def _na_ctx(qn, kn, v):
    blk = lambda w: pl.BlockSpec((SEQ, w), lambda b: (b, 0))
    hm = pl.BlockSpec((1, NA_HEADS, SEQ, NA_HEAD_DIM), lambda b: (b, 0, 0, 0))
    hm_shape = jax.ShapeDtypeStruct((BATCH, NA_HEADS, SEQ, NA_HEAD_DIM), F32)
    return pl.pallas_call(
        _na_ctx_kernel,
        grid=(BATCH,),
        in_specs=[blk(NA_WIDTH)] * 3,
        out_specs=[blk(NA_WIDTH), hm, hm],
        out_shape=[jax.ShapeDtypeStruct((N_TOK, NA_WIDTH), BF16), hm_shape, hm_shape],
        compiler_params=_cparams(),
        name="na_ctx",
    )(qn, kn, v)


GRID_ROWS = DEC_SEQ // GRID_W
NA_LOC = NA_WIN_ROWS * GRID_W
NA_MASKED = -1e30


def _na_lat_kernel(q_ref, k_ref, v_ref, kc_ref, vc_ref, bt_ref, yin_ref, y_ref):
    del yin_ref
    rb = pl.program_id(1)

    tiles = [slice(128 * i, 128 * (i + 1)) for i in range(NA_PAIRS)]

    def rows(it, carry):
        units, q0, k0, dr0 = [], {}, {}, {}
        for rr_ in range(NA_ROWS_PER_ITER):
            r = it * NA_ROWS_PER_ITER + rr_
            rr = rb * NA_R + r
            rs = jnp.clip(rr - NA_WIN_ROWS // 2, 0, GRID_ROWS - NA_WIN_ROWS)
            dr0[rr_] = rs - rr + (NA_WIN_ROWS - 1)
            q0[rr_] = pl.multiple_of(r * GRID_W, GRID_W)
            k0[rr_] = pl.multiple_of(rs * GRID_W, GRID_W)
            units += [(rr_, i) for i in range(NA_PAIRS)]
        def scores(u, _):
            r, i = u
            qq = _stack_pair(q_ref[pl.ds(q0[r], GRID_W), tiles[i]])
            s_loc = (_dot_nt(qq, k_ref[pl.ds(k0[r], NA_LOC), tiles[i]])
                     + bt_ref[dr0[r], 2 * i:2 * i + 2].reshape(2 * GRID_W, NA_LOC))
            return s_loc, _dot_nt(qq, kc_ref[0, :, tiles[i]])

        def softmax(u, s):
            s_loc, s_ctx = s
            m = jnp.maximum(jnp.max(s_loc, axis=-1, keepdims=True), jnp.max(s_ctx, axis=-1, keepdims=True))
            e_loc = jnp.exp(s_loc - m)
            e_ctx = jnp.exp(s_ctx - m)
            l = jnp.sum(e_loc, axis=-1, keepdims=True) + jnp.sum(e_ctx, axis=-1, keepdims=True)
            return e_loc.astype(BF16), e_ctx.astype(BF16), l

        def weighted(u, p):
            r, i = u
            p_loc, p_ctx, l = p
            o = _dot(p_loc, v_ref[pl.ds(k0[r], NA_LOC), tiles[i]]) + _dot(p_ctx, vc_ref[0, :, tiles[i]])
            y = _unstack_pair(o, GRID_W) / _unstack_pair(jnp.broadcast_to(l, o.shape), GRID_W)
            y_ref[pl.ds(q0[r], GRID_W), tiles[i]] = y.astype(BF16)

        _skewed(units, (scores, softmax, weighted))
        return carry

    lax.fori_loop(0, NA_R // NA_ROWS_PER_ITER, rows, 0)


def _na_lat(qn, knb, vb, kc, vc, btab, y_in):
    rows_per_step = NA_R * GRID_W
    steps = GRID_ROWS // NA_R
    off_q = N_PROMPT // rows_per_step
    off_s = N_PROMPT // DEC_SEQ
    qspec = pl.BlockSpec((rows_per_step, NA_WIDTH), lambda b, r: (off_q + b * steps + r, 0))
    kvspec = pl.BlockSpec((DEC_SEQ, NA_WIDTH), lambda b, r: (off_s + b, 0))
    cspec = pl.BlockSpec((1, PAST_LEN, NA_WIDTH), lambda b, r: (b, 0, 0))
    token_major = lambda a: a.transpose(0, 2, 1, 3).reshape(DEC_BATCH, PAST_LEN, NA_WIDTH).astype(BF16)
    kc, vc = token_major(kc), token_major(vc)
    return pl.pallas_call(
        _na_lat_kernel,
        grid=(DEC_BATCH, steps),
        in_specs=[qspec, kvspec, kvspec, cspec, cspec,
                  _const_spec((NA_WIN_ROWS, NA_HEADS, GRID_W, NA_LOC)),
                  pl.BlockSpec(memory_space=pl.ANY)],
        out_specs=qspec,
        out_shape=jax.ShapeDtypeStruct((N_TOK, NA_WIDTH), BF16),
        input_output_aliases={6: 0},
        compiler_params=_cparams(2),
        name="na_lat",
    )(qn, knb, vb, kc, vc, btab, y_in)


def _na_bias_table(rpb):
    col = jnp.arange(GRID_W)
    col_start = jnp.clip(col - NA_WIN_COLS // 2, 0, GRID_W - NA_WIN_COLS)
    ok = (col[None, :] >= col_start[:, None]) & (col[None, :] < col_start[:, None] + NA_WIN_COLS)
    d_col = jnp.clip(col[None, :] - col[:, None], -(NA_WIN_COLS - 1), NA_WIN_COLS - 1) + (NA_WIN_COLS - 1)
    onehot = (d_col[:, :, None] == jnp.arange(2 * NA_WIN_COLS - 1)).astype(F32)
    t = jnp.einsum('hrd,cxd->hcrx', rpb, onehot, precision=HIGHEST)
    t = jnp.where(ok[None, :, None, :], t, NA_MASKED)
    b = jnp.stack([t[:, :, a:a + NA_WIN_ROWS, :] for a in range(NA_WIN_ROWS)])
    return b.reshape(NA_WIN_ROWS, NA_HEADS, GRID_W, NA_LOC)


LIN_HEADS = 4
LIN_DK = 64
LIN_DV = 128
LIN_QK = LIN_HEADS * LIN_DK
LIN_V = LIN_HEADS * LIN_DV


def _log_sigmoid(x):
    return jnp.minimum(x, 0.0) - jnp.log(1.0 + jnp.exp(-jnp.abs(x)))


def _lin_kernel(gqf_ref, gvf_ref, gqb_ref, gvb_ref, hqf_ref, hvf_ref, hqb_ref, hvb_ref,
                sgf_ref, sgb_ref, shf_ref, shb_ref, wa_ref, ba_ref, lbl_ref,
                ogf_ref, ogb_ref, ohf_ref, ohb_ref, ngf_ref, ngb_ref, nhf_ref, nhb_ref, worst_ref,
                st_ref, fb_ref, *, exact):
    j = pl.program_id(0)
    _, blk, _ = _seq_of(j)
    is_prompt = j < PROMPT_BLKS
    s0_refs = (sgf_ref, sgb_ref, shf_ref, shb_ref)

    @pl.when(blk == 0)
    def _():
        for i in range(4):
            st_ref[i] = jnp.where(is_prompt, 0.0, s0_refs[i][0])

    L = LIN_L
    nch = RB // L
    qk_refs = ((gqf_ref, gqb_ref), (hqf_ref, hqb_ref))
    v_refs = ((gvf_ref, gvb_ref), (hvf_ref, hvb_ref))
    o_refs = ((ogf_ref, ogb_ref), (ohf_ref, ohb_ref))
    lane = lax.broadcasted_iota(jnp.int32, (1, LIN_QK), 1)
    head_mask = [(lane >> 6) == h for h in range(LIN_HEADS)]
    r4 = lax.broadcasted_iota(jnp.int32, (LIN_HEADS * L, L), 0) & (L - 1)
    c4 = lax.broadcasted_iota(jnp.int32, (LIN_HEADS * L, L), 1)
    tri4 = [c4 <= r4, c4 >= r4]
    eye = (lax.broadcasted_iota(jnp.int32, (LIN_QK, LIN_QK), 0)
           == lax.broadcasted_iota(jnp.int32, (LIN_QK, LIN_QK), 1))
    tri_bf = [jnp.where(_tri(L, bool(d)), 1.0, 0.0).astype(BF16) for d in range(2)]

    def lower_bound(d):
        l0 = lbl_ref[d, 0:1, :]
        l1 = lbl_ref[d, 1:2, :]
        mx = jnp.maximum(l0, l1)
        e0 = jnp.exp(l0 - mx)
        e1 = jnp.exp(l1 - mx)
        p0 = e0 / (e0 + e1)
        p1 = e1 / (e0 + e1)
        return (p0 + p1) - p0

    lbs = (lower_bound(0), lower_bound(1))
    units = [(m, d, c) for c in range(nch) for d in range(2) for m in range(2)]
    row0 = {u: ((nch - 1 - u[2]) * L if u[1] else u[2] * L) for u in units}

    def gates(u, _):
        m, d, _ = u
        r0 = row0[u]
        x_ref = qk_refs[m][d]
        if m == 0:
            q = x_ref[r0:r0 + L, 0:256] * (GLA_DK ** -0.5)
            k = x_ref[r0:r0 + L, 256:512]
            ga_hi, ga_lo = _split2(x_ref[r0:r0 + L, 512:640])
            wa_hi, wa_lo = _split2(wa_ref[d])
            x = _dot(ga_hi, wa_hi) + _dot(ga_lo, wa_hi) + _dot(ga_hi, wa_lo) + ba_ref[d]
            g = _log_sigmoid(x) / GLA_GATE_NORM
        else:
            q = x_ref[r0:r0 + L, 0:256]
            f = lbs[d] + (1.0 - lbs[d]) * _sigmoid(x_ref[r0:r0 + L, 256 * (d + 1):256 * (d + 2)])
            k = 1.0 - f
            g = jnp.log(f)
        return q, k, _dot_exact_lhs(tri_bf[d], g)

    lasts = []

    def scale(u, p):
        m, d, _ = u
        q, k, cum = p
        last = cum[0:1, :] if d else cum[L - 1:L, :]
        lasts.append(last)
        q_in = q * jnp.exp(cum)
        k_out = (k * jnp.exp(-cum)).astype(BF16)
        k_end_t = (k * jnp.exp(last - cum)).T.astype(BF16)
        qs = jnp.concatenate([jnp.where(hm, q_in, 0.0) for hm in head_mask], axis=0).astype(BF16)
        dcol = jnp.sum(jnp.where(eye, jnp.exp(last), 0.0), axis=1, keepdims=True)
        v = v_refs[m][d][row0[u]:row0[u] + L, :].astype(BF16)
        return qs, k_out, k_end_t, dcol, v

    def products(u, p):
        qs, k_out, k_end_t, dcol, v = p
        upd = jnp.concatenate(
            [_dot(k_end_t[LIN_DK * h:LIN_DK * (h + 1)], v[:, LIN_DV * h:LIN_DV * (h + 1)]) for h in range(LIN_HEADS)],
            axis=0)
        return qs, _dot_nt(qs, k_out), upd, dcol, v

    def exact_scores(u):
        d = u[1]
        q, k, cum = gates(u, None)
        fb_ref[0], fb_ref[1], fb_ref[2] = q, k, cum
        head_sum = jnp.where((lax.broadcasted_iota(jnp.int32, (LIN_QK, 128), 0) >> 6)
                             == lax.broadcasted_iota(jnp.int32, (LIN_QK, 128), 1), 1.0, 0.0).astype(BF16)
        t_col = lax.broadcasted_iota(jnp.int32, (L, 1), 0)
        s_row = lax.broadcasted_iota(jnp.int32, (1, L), 1)

        def key_row(s, acc):
            ks = fb_ref[1, pl.ds(s, 1), :]
            cs = fb_ref[2, pl.ds(s, 1), :]
            w = fb_ref[0] * ks * jnp.exp(jnp.minimum(fb_ref[2] - cs, 0.0))
            cols = _dot_exact_rhs(w, head_sum)
            cols = jnp.where((t_col <= s) if d else (t_col >= s), cols, 0.0)
            return acc + jnp.concatenate(
                [jnp.where(s_row == s, cols[:, h:h + 1], 0.0) for h in range(LIN_HEADS)], axis=0)

        return lax.fori_loop(0, L, key_row, jnp.zeros((LIN_HEADS * L, L), F32)).astype(BF16)

    def mask(u, p):
        qs, a, upd, dcol, v = p
        a = exact_scores(u) if exact else jnp.where(tri4[u[1]], a, 0.0).astype(BF16)
        return qs, a, upd, dcol, v

    def intra(u, p):
        qs, a, upd, dcol, v = p
        o_intra = jnp.concatenate(
            [_dot(a[h * L:(h + 1) * L], v[:, LIN_DV * h:LIN_DV * (h + 1)]) for h in range(LIN_HEADS)], axis=0)
        return qs, o_intra, upd, dcol

    s = [st_ref[i] for i in range(4)]

    def outputs(u, p):
        m, d, _ = u
        qs, o_intra, upd, dcol = p
        i = 2 * m + d
        o = o_intra + _dot(qs, s[i].astype(BF16))
        o_refs[m][d][row0[u]:row0[u] + L, :] = jnp.concatenate(
            [o[h * L:(h + 1) * L] for h in range(LIN_HEADS)], axis=1).astype(BF16)
        s[i] = dcol * s[i] + upd

    _skewed(units, (gates, scale, products, mask, intra, outputs))
    for i in range(4):
        st_ref[i] = s[i]

    @pl.when(is_prompt)
    def _():
        for i, n_ref in enumerate((ngf_ref, ngb_ref, nhf_ref, nhb_ref)):
            n_ref[0] = s[i]
    worst_ref[0] = jnp.broadcast_to(functools.reduce(jnp.minimum, lasts), (SUBLANES, LIN_QK))


def _lin(gqk, gv, hqf, hi, states, wa, ba, lbl):
    st_spec = pl.BlockSpec((1, LIN_QK, LIN_DV), lambda j: (_state_idx(j), 0, 0))
    so_spec = pl.BlockSpec((1, LIN_QK, LIN_DV), lambda j: (jnp.minimum(j, BATCH - 1), 0, 0))
    so_shape = jax.ShapeDtypeStruct((BATCH, LIN_QK, LIN_DV), F32)
    o_shape = jax.ShapeDtypeStruct((N_TOK, LIN_V), BF16)
    worst_spec = pl.BlockSpec((1, SUBLANES, LIN_QK), lambda j: (j, 0, 0))
    worst_shape = jax.ShapeDtypeStruct((N_BLK, SUBLANES, LIN_QK), F32)
    packed = [s.reshape(DEC_BATCH, LIN_QK, LIN_DV) for s in states]

    def run(exact):
        return pl.pallas_call(
            functools.partial(_lin_kernel, exact=exact),
            grid=(N_BLK,),
            in_specs=[_fwd_spec(640), _fwd_spec(LIN_V), _bwd_spec(640), _bwd_spec(LIN_V),
                      _fwd_spec(768), _fwd_spec(LIN_V), _bwd_spec(768), _bwd_spec(LIN_V)]
                     + [st_spec] * 4 + [_const_spec(wa.shape), _const_spec(ba.shape), _const_spec(lbl.shape)],
            out_specs=[_fwd_spec(LIN_V), _bwd_spec(LIN_V), _fwd_spec(LIN_V), _bwd_spec(LIN_V)] + [so_spec] * 4
                      + [worst_spec],
            out_shape=[o_shape] * 4 + [so_shape] * 4 + [worst_shape],
            scratch_shapes=[pltpu.VMEM((4, LIN_QK, LIN_DV), F32), pltpu.VMEM((3, LIN_L, LIN_QK), F32)],
            compiler_params=_cparams(),
            name="lin_exact" if exact else "lin",
        )(gqk, gv, gqk, gv, hqf, hi, hqf, hi, *packed, wa, ba, lbl)

    fast = run(False)
    outs = lax.cond(jnp.min(fast[8]) < -LIN_SAFE_LOG_DECAY, lambda: tuple(run(True)[:8]), lambda: tuple(fast[:8]))
    unpack = lambda s: s.reshape(BATCH, LIN_HEADS, LIN_DK, LIN_DV)
    return outs[:4], [unpack(s) for s in outs[4:]]


def _f32(ref):
    return ref[...].astype(F32)


BF16_ROWS = 16


def _ext_rows(refs):
    t_ref, n_ref, p_ref = refs
    p = _f32(p_ref)
    return jnp.concatenate([_f32(t_ref), _f32(n_ref)[0:SUBLANES], p[p.shape[0] - SUBLANES:]], axis=0)


def _mix0(refs):
    of, ob, z, yb, (nw_ref,) = refs[0:3], refs[3:6], refs[6:9], refs[9:12], refs[12:]
    ya = (_ext_rows(of) + _ext_rows(ob)) * _ext_rows(z)
    ms = jnp.mean(ya * ya, axis=-1, keepdims=True)
    ya = ya * lax.rsqrt(ms + NORM_EPS) * nw_ref[...]
    return ya.astype(BF16), _ext_rows(yb).astype(BF16)


def _head_rms128(o, w):
    parts = []
    for h in range(LIN_HEADS):
        oh = o[:, LIN_DV * h:LIN_DV * (h + 1)]
        ms = jnp.mean(oh * oh, axis=-1, keepdims=True)
        parts.append(oh * lax.rsqrt(ms + NORM_EPS) * w)
    return jnp.concatenate(parts, axis=-1)


def _mix1(refs):
    gf, gb, gg, hf, hb, hg, (gw_ref, hw_ref) = (refs[0:3], refs[3:6], refs[6:9], refs[9:12], refs[12:15],
                                               refs[15:18], refs[18:])
    yc = _head_rms128(_ext_rows(gf) + _ext_rows(gb), gw_ref[...]) * _ext_rows(gg)
    yd = _head_rms128(_ext_rows(hf) + _ext_rows(hb), hw_ref[...]) * _ext_rows(hg)
    return yc.astype(BF16), yd.astype(BF16)


def _outffn_kernel(*refs, layer, n_out):
    i = pl.program_id(0)
    is_prompt = i < PROMPT_TILES
    if layer == 0:
        xp_ref, xs_ref, xn_ref, xv_ref = refs[0:4]
        x0 = jnp.concatenate([_tile_of(xp_ref, xs_ref), xn_ref[...], xv_ref[...]], axis=0)
        n_mix, mix = 13, _mix0
        refs = refs[4:]
    else:
        x0 = _ext_rows(refs[0:3])
        n_mix, mix = 20, _mix1
        refs = refs[3:]
    m_ref, refs = refs[0], refs[1:]
    ya, yb = mix(refs[:n_mix])
    wo_ref, nw_ref, wu_ref, cw_ref, cb_ref, wd_ref = refs[n_mix:n_mix + 6]
    rest = refs[n_mix + 6:]
    o_refs, (hs_ref, act_ref, x1_ref) = rest[:n_out], rest[n_out:]

    x1 = x0 + m_ref[0, 2:3, :] * (_dot(ya, wo_ref[0:512, :]) + _dot(yb, wo_ref[512:1024, :]))
    x1_ref[...] = x1[0:TM]
    nw = nw_ref[...]
    sh = m_ref[0, 3:4, :]
    sc = m_ref[0, 4:5, :]
    hs_ref[0:TM, :] = _to_strided(_norm_mod(x1[0:TM], nw, sh, sc)).astype(BF16)
    hs_ref[TM:TM_EXT, :] = (_norm_mod(x1[TM:TM_EXT], nw, sh, sc) * _halo_keep(i)).astype(BF16)

    def conv_cols(c0):
        cs = slice(c0, c0 + FFN_CH)
        return _dwconv_strided(_dot(hs_ref[...], wu_ref[:, cs]), cw_ref[:, cs], cb_ref[:, cs], i)

    for c in range(FFN_DIM // FFN_CH):
        a = conv_cols(c * FFN_CH)
        b = conv_cols(FFN_DIM + c * FFN_CH)
        act_ref[:, c * FFN_CH:(c + 1) * FFN_CH] = (_silu(a) * b).astype(BF16)
    out = x1_ref[...] + m_ref[0, 5:6, :] * _from_strided(_dot(act_ref[...], wd_ref[...]))
    if len(o_refs) == 1:
        o_refs[0][...] = out
    else:
        @pl.when(is_prompt)
        def _():
            o_refs[0][...] = out

        @pl.when(jnp.logical_not(is_prompt))
        def _():
            o_refs[1][...] = out


def _outffn(x_args, mix_arrays, mix_params, mrows, w_out, nw, wu, cw, cb, wd, layer):
    per_layer = lambda *shape: pl.BlockSpec((None,) + shape, lambda i: (layer,) + (0,) * len(shape))
    single = dict(pipeline_mode=pl.Buffered(1))
    if layer == 0:
        xp, xs = x_args
        x_ops = [xp, xs, xs, xs]
        x_specs = [_prompt_tile_spec(), _sample_tile_spec()] + _halo_specs(D_MODEL, SUBLANES, N_SAMPLE, PROMPT_TILES)
        out_specs = [_row_spec(D_MODEL)]
        out_shape = [jax.ShapeDtypeStruct((N_TOK, D_MODEL), F32)]
    else:
        x_ops = [x_args[0]] * 3
        x_specs = [_row_spec(D_MODEL)] + _halo_specs(D_MODEL, SUBLANES, N_TOK)
        out_specs = [_prompt_tile_spec(), _sample_tile_spec()]
        out_shape = [jax.ShapeDtypeStruct((N_PROMPT, D_MODEL), F32), jax.ShapeDtypeStruct((N_SAMPLE, D_MODEL), F32)]
    mix_ops, mix_specs = [], []
    for a in mix_arrays:
        mix_ops += [a] * 3
        mix_specs += [_row_spec(a.shape[1])] + _halo_specs(a.shape[1], BF16_ROWS, N_TOK)
    return pl.pallas_call(
        functools.partial(_outffn_kernel, layer=layer, n_out=len(out_specs)),
        grid=(N_TILES,),
        in_specs=x_specs + [_mod_spec()] + mix_specs + [_const_spec(p.shape) for p in mix_params]
                 + [pl.BlockSpec((D_MODEL, D_MODEL), lambda i: (0, 0), **single), _const_spec((1, D_MODEL)),
                    pl.BlockSpec((None, D_MODEL, 2 * FFN_DIM), lambda i: (layer, 0, 0), **single),
                    per_layer(3, 2 * FFN_DIM), per_layer(1, 2 * FFN_DIM),
                    pl.BlockSpec((None, FFN_DIM, D_MODEL), lambda i: (layer, 0, 0), **single)],
        out_specs=out_specs,
        out_shape=out_shape,
        scratch_shapes=[pltpu.VMEM((TM_EXT, D_MODEL), BF16), pltpu.VMEM((TM, FFN_DIM), BF16),
                        pltpu.VMEM((TM, D_MODEL), F32)],
        compiler_params=_cparams(),
        name="outffn",
    )(*x_ops, mrows, *mix_ops, *mix_params, w_out, nw, wu, cw, cb, wd)


def _pad_lanes(v, width=128):
    v = v.reshape(1, -1)
    return jnp.pad(v, ((0, 0), (0, width - v.shape[1])))


def kernel(x_prompt, x_sample, cache_na_k_l0, cache_na_v_l0, state_ssd_fwd_l0, state_ssd_bwd_l0,
           state_gla_fwd_l1, state_gla_bwd_l1, state_hgrn_fwd_l1, state_hgrn_bwd_l1, c,
           c_ctx, w_ada, b_ada, norm_w, ffn_w_up, ffn_conv_w, ffn_conv_b, ffn_w_down,
           w_in_l0, w_out_l0, ssd_conv_w_l0, ssd_conv_b_l0, ssd_dt_bias_l0, ssd_a_log_l0, ssd_d_l0,
           ssd_norm_w_l0, na_q_norm_l0, na_k_norm_l0, na_rpb_l0,
           w_in_l1, w_out_l1, gla_wa2_l1, gla_ba2_l1, gla_norm_w_l1, hgrn_lb_logits, hgrn_norm_w_l1):
    xp = x_prompt.reshape(N_PROMPT, D_MODEL)
    xs = x_sample.reshape(N_SAMPLE, D_MODEL)

    cvec8 = jnp.zeros((SUBLANES, D_MODEL), F32).at[0:DEC_BATCH].set(c).at[CTX_MOD_ROW].set(c_ctx)
    mods = _mods(cvec8, w_ada, b_ada)
    mods = mods.reshape(2, SUBLANES, 6, D_MODEL)
    mods = jnp.pad(mods, ((0, 0), (0, 0), (0, SUBLANES - 6), (0, 0)))

    zpad = lambda n: jnp.zeros((D_MODEL, n), BF16)
    w0b, w1b = w_in_l0.astype(BF16), w_in_l1.astype(BF16)
    w0 = jnp.concatenate([w0b[:, :1296], zpad(112), w0b[:, 1296:]], axis=1)
    w1 = jnp.concatenate([w1b[:, 0:512], w1b[:, 1536:1568], zpad(96), w1b[:, 512:1536], w1b[:, 1568:3360]], axis=1)
    seg = jnp.kron(jnp.eye(NA_HEADS, dtype=F32), jnp.full((NA_HEAD_DIM, NA_HEAD_DIM), 1.0 / NA_HEAD_DIM, F32)).astype(BF16)

    m0 = mods[0]
    z, xc, dt, qn, knb, vb, kn, v = _inproj0(
        xp, xs, m0, norm_w[0, 0].reshape(1, D_MODEL), w0, seg,
        jnp.tile(na_q_norm_l0, NA_HEADS).reshape(1, NA_WIDTH), jnp.tile(na_k_norm_l0, NA_HEADS).reshape(1, NA_WIDTH),
        ssd_conv_w_l0, ssd_conv_b_l0.reshape(1, SSD_XBC))
    of, ob, ssd_f, ssd_b = _ssd(
        xc, dt, state_ssd_fwd_l0, state_ssd_bwd_l0, _pad_lanes(ssd_dt_bias_l0), _pad_lanes(ssd_a_log_l0),
        jnp.repeat(ssd_d_l0, SSD_HEAD_DIM).reshape(1, SSD_WIDTH))
    yb, na_k, na_v = _na_ctx(qn, kn, v)
    yb = _na_lat(qn, knb, vb, cache_na_k_l0, cache_na_v_l0, _na_bias_table(na_rpb_l0), yb)
    ffn_weights = (ffn_w_up.astype(BF16), ffn_conv_w, ffn_conv_b[:, None, :], ffn_w_down.astype(BF16))
    x, = _outffn((xp, xs), (of, ob, z, yb), (ssd_norm_w_l0.reshape(1, SSD_WIDTH),), m0, w_out_l0.astype(BF16),
                 norm_w[0, 1].reshape(1, D_MODEL), *ffn_weights, layer=0)

    m1 = mods[1]
    gqk, gv, gg, hqf, hi, hg = _inproj1(x, m1, norm_w[1, 0].reshape(1, D_MODEL), w1)
    wa_pad = jnp.zeros((2, 128, LIN_QK), F32)
    wa_pad = wa_pad.at[0, 0:GLA_RANK].set(gla_wa2_l1[0]).at[1, GLA_RANK:2 * GLA_RANK].set(gla_wa2_l1[1])
    (gf, gb, hf, hb), (gla_f, gla_b, hgrn_f, hgrn_b) = _lin(
        gqk, gv, hqf, hi, (state_gla_fwd_l1, state_gla_bwd_l1, state_hgrn_fwd_l1, state_hgrn_bwd_l1),
        wa_pad, gla_ba2_l1.reshape(2, 1, LIN_QK), hgrn_lb_logits)
    y_p, y_s = _outffn((x,), (gf, gb, gg, hf, hb, hg),
                       (gla_norm_w_l1.reshape(1, LIN_DV), hgrn_norm_w_l1.reshape(1, LIN_DV)), m1,
                       w_out_l1.astype(BF16), norm_w[1, 1].reshape(1, D_MODEL), *ffn_weights, layer=1)
    return (y_p.reshape(BATCH, SEQ, D_MODEL), y_s.reshape(DEC_BATCH, DEC_SEQ, D_MODEL),
            na_k, na_v, ssd_f, ssd_b, gla_f, gla_b, hgrn_f, hgrn_b)
```
